```python
import math
import jax, jax.numpy as jnp
from jax import lax
import numpy as np

D_MODEL = 1024
BATCH = 8
SEQ = 2048
DEPTH = 1
DEC_BATCH = 128
DEC_SEQ = 8
PAST_LEN = 16384
PAGE_SIZE = 128

HGRN_HEADS = 4
HGRN_DK = 128
HGRN_DV = 128
RET_HEADS = 4
RET_DK = 128
RET_DV = 128
HGRN_WIDTH = HGRN_HEADS * HGRN_DK
HGRN_VWIDTH = HGRN_HEADS * HGRN_DV
RET_WIDTH = RET_HEADS * RET_DK
RET_VWIDTH = RET_HEADS * RET_DV
MIX_WIDTH = HGRN_VWIDTH + RET_VWIDTH
IN_COLS = 2 * HGRN_WIDTH + 2 * HGRN_VWIDTH + 2 * RET_WIDTH + 2 * RET_VWIDTH
HGRN_CHUNK = 64
RET_CHUNK = 128
ROPE_BASE = 10000.0
N_MEM = 256
X_HEADS = 4
X_HEAD_DIM = D_MODEL // X_HEADS
N_EXPERTS = 32
TOP_K = 4
D_FF = D_MODEL
SWIGLU_LIMIT = 7.0
SWIGLU_ALPHA = 1.702
EPS = 1e-6

kernel_name = "hymba_hgrn2_retnet_moe_decode_step"


def rms_norm(x, g):
    xf = x.astype(jnp.float32)
    y = xf * lax.rsqrt(jnp.mean(xf * xf, axis=-1, keepdims=True) + EPS)
    return (y * g.astype(jnp.float32)).astype(x.dtype)


def head_rms_norm(x, g):
    y = x * lax.rsqrt(jnp.mean(x * x, axis=-1, keepdims=True) + EPS)
    return y * g.astype(jnp.float32)


def head_group_norm(x, g):
    mu = jnp.mean(x, axis=-1, keepdims=True)
    xc = x - mu
    var = jnp.mean(xc * xc, axis=-1, keepdims=True)
    return xc * lax.rsqrt(var + EPS) * g.astype(jnp.float32)


def _split_heads(a, n_heads):
    b, t, _ = a.shape
    return a.reshape(b, t, n_heads, -1).transpose(0, 2, 1, 3)


def _merge_heads(a):
    b, h, t, d = a.shape
    return a.transpose(0, 2, 1, 3).reshape(b, t, h * d)


def _to_chunks(a, c):
    b, h, t, d = a.shape
    return a.reshape(b, h, t // c, c, d).transpose(2, 0, 1, 3, 4)


def _from_chunks(a):
    n, b, h, c, d = a.shape
    return a.transpose(1, 2, 0, 3, 4).reshape(b, h, n * c, d)


def rotary(x, pos):
    half = x.shape[-1] // 2
    inv = jnp.power(ROPE_BASE, -jnp.arange(half, dtype=jnp.float32) / half)
    ang = pos.astype(jnp.float32)[:, None] * inv[None, :]
    cos, sin = jnp.cos(ang), jnp.sin(ang)
    x1, x2 = x[..., :half], x[..., half:]
    return jnp.concatenate([x1 * cos - x2 * sin, x2 * cos + x1 * sin], axis=-1)


def hgrn2_chunked(q, k, v, log_f, s0):
    t = q.shape[2]
    c = min(HGRN_CHUNK, t)
    causal = jnp.tril(jnp.ones((c, c), dtype=bool))[:, :, None]

    def step(S, inp):
        qc, kc, vc, gc = inp
        lc = jnp.cumsum(gc, axis=2)
        diff = lc[:, :, :, None, :] - lc[:, :, None, :, :]
        decay = jnp.exp(jnp.where(causal, diff, -jnp.inf))
        scores = jnp.einsum('bhtk,bhsk,bhtsk->bhts', qc, kc, decay)
        o = (jnp.einsum('bhts,bhsv->bhtv', scores, vc)
             + jnp.einsum('bhtk,bhkv->bhtv', qc * jnp.exp(lc), S))
        last = lc[:, :, -1:, :]
        S = (jnp.exp(last[:, :, 0, :])[..., None] * S
             + jnp.einsum('bhsk,bhsv->bhkv', kc * jnp.exp(last - lc), vc))
        return S, o

    S, o = lax.scan(step, s0, (_to_chunks(q, c), _to_chunks(k, c), _to_chunks(v, c), _to_chunks(log_f, c)))
    return _from_chunks(o), S


def retention_chunked(q, k, v, s0):
    h, t = q.shape[1], q.shape[2]
    c = min(RET_CHUNK, t)
    log_g = jnp.log1p(-jnp.exp2(-5.0 - jnp.arange(h, dtype=jnp.float32)))
    idx = jnp.arange(c, dtype=jnp.float32)
    rel = idx[:, None] - idx[None, :]
    dmat = jnp.where(rel >= 0, jnp.exp(log_g[:, None, None] * jnp.maximum(rel, 0.0)), 0.0)
    q_in = jnp.exp(log_g[:, None] * (idx + 1.0))[:, :, None]
    k_out = jnp.exp(log_g[:, None] * (c - 1.0 - idx))[:, :, None]
    g_all = jnp.exp(log_g * c)[:, None, None]

    def step(S, inp):
        qc, kc, vc = inp
        sc = jnp.einsum('bhtk,bhsk->bhts', qc, kc) * dmat
        o = (jnp.einsum('bhts,bhsv->bhtv', sc, vc)
             + jnp.einsum('bhtk,bhkv->bhtv', qc * q_in, S))
        S = g_all * S + jnp.einsum('bhsk,bhsv->bhkv', kc * k_out, vc)
        return S, o

    S, o = lax.scan(step, s0, (_to_chunks(q, c), _to_chunks(k, c), _to_chunks(v, c)))
    return _from_chunks(o), S


def token_mixer(hn, s_hgrn, s_ret, pos, lb, w_in, g_hgrn, g_ret, w_out):
    f32 = jnp.float32
    proj = (hn @ w_in).astype(f32)
    sizes = [HGRN_WIDTH, HGRN_WIDTH, HGRN_VWIDTH, HGRN_VWIDTH,
             RET_WIDTH, RET_WIDTH, RET_VWIDTH, RET_VWIDTH]
    cuts = list(np.cumsum(sizes)[:-1])
    hq, hf, hi, hg, rq, rk, rv, rg = jnp.split(proj, cuts, axis=-1)
    f = lb + (1.0 - lb) * jax.nn.sigmoid(hf)
    q_h = _split_heads(hq, HGRN_HEADS) * HGRN_DK ** -0.5
    k_h = _split_heads(1.0 - f, HGRN_HEADS)
    lf_h = _split_heads(jnp.log(f), HGRN_HEADS)
    v_h = _split_heads(hi, HGRN_HEADS)
    o_h, s_hgrn_new = hgrn2_chunked(q_h, k_h, v_h, lf_h, s_hgrn.astype(f32))
    o_h = _merge_heads(head_rms_norm(o_h, g_hgrn)) * jax.nn.sigmoid(hg)
    q_r = rotary(_split_heads(rq, RET_HEADS), pos)
    k_r = rotary(_split_heads(rk, RET_HEADS), pos) * RET_DK ** -0.5
    v_r = _split_heads(rv, RET_HEADS)
    o_r, s_ret_new = retention_chunked(q_r, k_r, v_r, s_ret.astype(f32))
    o_r = _merge_heads(head_group_norm(o_r, g_ret)) * jax.nn.silu(rg)
    mixed = jnp.concatenate([o_h, o_r], axis=-1).astype(hn.dtype)
    return mixed @ w_out, s_hgrn_new, s_ret_new


def mem_kv(mem, g, w_mk, w_mv):
    mn = rms_norm(mem, g)
    return _split_heads(mn @ w_mk, X_HEADS), _split_heads(mn @ w_mv, X_HEADS)


def cross_attend(hn, mk, mv, w_xq, w_xo):
    q = _split_heads(hn @ w_xq, X_HEADS).astype(jnp.float32)
    s = jnp.einsum('bhtd,bhmd->bhtm', q, mk.astype(jnp.float32)) * X_HEAD_DIM ** -0.5
    p = jax.nn.softmax(s, axis=-1)
    o = jnp.einsum('bhtm,bhmd->bhtd', p, mv.astype(jnp.float32))
    return _merge_heads(o).astype(hn.dtype) @ w_xo


def moe(hn, w_router, b_router, w_gate_up, b_gate_up, w_down, b_down):
    b, t, d = hn.shape
    x = hn.reshape(b * t, d)
    logits = (x @ w_router + b_router).astype(jnp.float32)
    top_v, top_i = lax.top_k(logits, TOP_K)
    p = jax.nn.softmax(top_v, axis=-1)
    gates = jnp.sum(jax.nn.one_hot(top_i, N_EXPERTS, dtype=jnp.float32) * p[..., None], axis=1)

    def expert(acc, inp):
        wgu, bgu, wd, bd, g = inp
        gu = x @ wgu + bgu
        glu = jnp.minimum(gu[:, 0::2], SWIGLU_LIMIT)
        lin = jnp.clip(gu[:, 1::2], -SWIGLU_LIMIT, SWIGLU_LIMIT)
        hmid = glu * jax.nn.sigmoid(SWIGLU_ALPHA * glu) * (lin + 1.0)
        return acc + g[:, None] * (hmid @ wd + bd).astype(jnp.float32), None

    acc0 = jnp.zeros((b * t, d), jnp.float32)
    acc, _ = lax.scan(expert, acc0, (w_gate_up, b_gate_up, w_down, b_down, gates.T))
    return acc.astype(hn.dtype).reshape(b, t, d)


def decoder_layer(x, s_hgrn, s_ret, mk, mv, pos, lb,
                  norm_mix, w_in, g_hgrn, g_ret, w_out, norm_x, w_xq, w_xo,
                  norm_ffn, w_router, b_router, w_gate_up, b_gate_up, w_down, b_down):
    mix, s_hgrn_new, s_ret_new = token_mixer(rms_norm(x, norm_mix), s_hgrn, s_ret, pos, lb,
                                             w_in, g_hgrn, g_ret, w_out)
    h = x + mix
    h = h + cross_attend(rms_norm(h, norm_x), mk, mv, w_xq, w_xo)
    h = h + moe(rms_norm(h, norm_ffn), w_router, b_router, w_gate_up, b_gate_up, w_down, b_down)
    return h, s_hgrn_new, s_ret_new


def setup_inputs(seed: int = 0) -> dict:
    key = jax.random.key(seed)
    ks = jax.random.split(key, 32)
    f32 = jnp.float32

    def nrm(k, shape, s):
        return jax.random.normal(k, shape, f32) * s

    def gain(k, shape):
        return 1.0 + 0.02 * jax.random.normal(k, shape, f32)

    return {
        "x_prompt": nrm(ks[0], (BATCH, SEQ, D_MODEL), 1.0),
        "x_sample": nrm(ks[1], (DEC_BATCH, DEC_SEQ, D_MODEL), 1.0),
        "state_hgrn": nrm(ks[2], (DEPTH, DEC_BATCH, HGRN_HEADS, HGRN_DK, HGRN_DV), 0.5),
        "state_ret": nrm(ks[3], (DEPTH, DEC_BATCH, RET_HEADS, RET_DK, RET_DV), 1.0),
        "cache_mem_k": nrm(ks[4], (DEPTH, DEC_BATCH, X_HEADS, N_MEM, X_HEAD_DIM), 1.0),
        "cache_mem_v": nrm(ks[5], (DEPTH, DEC_BATCH, X_HEADS, N_MEM, X_HEAD_DIM), 1.0),
        "mem_prompt": nrm(ks[6], (BATCH, N_MEM, D_MODEL), 1.0),
        "norm_mix": gain(ks[7], (DEPTH, D_MODEL)),
        "w_in": nrm(ks[8], (DEPTH, D_MODEL, IN_COLS), D_MODEL ** -0.5),
        "hgrn_lb_logits": nrm(ks[9], (DEPTH + 1, HGRN_WIDTH), 0.5),
        "hgrn_out_norm": gain(ks[10], (DEPTH, HGRN_DV)),
        "ret_out_norm": gain(ks[11], (DEPTH, RET_DV)),
        "w_out": nrm(ks[12], (DEPTH, MIX_WIDTH, D_MODEL), 0.5 * MIX_WIDTH ** -0.5),
        "norm_x": gain(ks[13], (DEPTH, D_MODEL)),
        "norm_mem": gain(ks[14], (DEPTH, D_MODEL)),
        "w_xq": nrm(ks[15], (DEPTH, D_MODEL, D_MODEL), D_MODEL ** -0.5),
        "w_mk": nrm(ks[16], (DEPTH, D_MODEL, D_MODEL), D_MODEL ** -0.5),
        "w_mv": nrm(ks[17], (DEPTH, D_MODEL, D_MODEL), D_MODEL ** -0.5),
        "w_xo": nrm(ks[18], (DEPTH, D_MODEL, D_MODEL), 0.5 * D_MODEL ** -0.5),
        "norm_ffn": gain(ks[19], (DEPTH, D_MODEL)),
        "w_router": nrm(ks[20], (DEPTH, D_MODEL, N_EXPERTS), D_MODEL ** -0.5),
        "b_router": nrm(ks[21], (DEPTH, N_EXPERTS), 0.01),
        "w_gate_up": nrm(ks[22], (DEPTH, N_EXPERTS, D_MODEL, 2 * D_FF), D_MODEL ** -0.5),
        "b_gate_up": nrm(ks[23], (DEPTH, N_EXPERTS, 2 * D_FF), 0.02),
        "w_down": nrm(ks[24], (DEPTH, N_EXPERTS, D_FF, D_MODEL), 0.5 * D_FF ** -0.5),
        "b_down": nrm(ks[25], (DEPTH, N_EXPERTS, D_MODEL), 0.02),
        "norm_final": gain(ks[26], (D_MODEL,)),
    }


def reference(x_prompt, x_sample, state_hgrn, state_ret, cache_mem_k, cache_mem_v, mem_prompt,
              norm_mix, w_in, hgrn_lb_logits, hgrn_out_norm, ret_out_norm, w_out,
              norm_x, norm_mem, w_xq, w_mk, w_mv, w_xo,
              norm_ffn, w_router, b_router, w_gate_up, b_gate_up, w_down, b_down, norm_final):
    f32 = jnp.float32
    b_p, t_p = x_prompt.shape[0], x_prompt.shape[1]
    t_s = x_sample.shape[1]
    pos_p = jnp.arange(t_p, dtype=jnp.int32)
    pos_s = PAST_LEN + jnp.arange(t_s, dtype=jnp.int32)
    lbs = jnp.cumsum(jax.nn.softmax(hgrn_lb_logits.astype(f32), axis=0), axis=0)
    hp, hs = x_prompt, x_sample
    sh_p, sr_p, mk_p_all, mv_p_all, sh_s, sr_s = [], [], [], [], [], []
    for l in range(DEPTH):
        lw = (norm_mix[l], w_in[l], hgrn_out_norm[l], ret_out_norm[l], w_out[l], norm_x[l], w_xq[l], w_xo[l],
              norm_ffn[l], w_router[l], b_router[l], w_gate_up[l], b_gate_up[l], w_down[l], b_down[l])
        mk_p, mv_p = mem_kv(mem_prompt, norm_mem[l], w_mk[l], w_mv[l])
        zh = jnp.zeros((b_p, HGRN_HEADS, HGRN_DK, HGRN_DV), f32)
        zr = jnp.zeros((b_p, RET_HEADS, RET_DK, RET_DV), f32)
        hp, shp, srp = decoder_layer(hp, zh, zr, mk_p, mv_p, pos_p, lbs[l], *lw)
        hs, shs, srs = decoder_layer(hs, state_hgrn[l], state_ret[l], cache_mem_k[l], cache_mem_v[l],
                                     pos_s, lbs[l], *lw)
        sh_p.append(shp); sr_p.append(srp); mk_p_all.append(mk_p); mv_p_all.append(mv_p)
        sh_s.append(shs); sr_s.append(srs)
    y_prompt = rms_norm(hp, norm_final)
    y_sample = rms_norm(hs, norm_final)
    return (y_prompt, y_sample,
            jnp.stack(sh_p), jnp.stack(sr_p), jnp.stack(mk_p_all), jnp.stack(mv_p_all),
            jnp.stack(sh_s), jnp.stack(sr_s))
```

```python
import functools

import numpy as np
import jax
import jax.numpy as jnp
from jax import lax
from jax.experimental import pallas as pl
from jax.experimental.pallas import tpu as pltpu

F32 = jnp.float32
BF16 = jnp.bfloat16

D = 1024
HEADS = 4
DH = 128
GW = HEADS * DH
IN_COLS = 8 * GW
ROPE_BASE = 10000.0
N_MEM = 256
X_HEADS = 4
X_DH = D // X_HEADS
N_EXPERTS = 32
TOP_K = 4
LIMIT = 7.0
ALPHA = 1.702
EPS = 1e-6
EXP_CLAMP = 80.0

HGRN_CHUNK = 64
RET_CHUNK = 128
T_TILE = 256
S_BATCH = 8
ROW_TILE = 512
Q_TILE = 512
A_BATCH = 4
R_TILE = 512
M_TILE = 256
DISPATCH_TILE = 256
COMBINE_TILE = 128
VMEM_LIMIT = 56 * 1024 * 1024


def _cparams(*sem):
    return pltpu.CompilerParams(dimension_semantics=sem, vmem_limit_bytes=VMEM_LIMIT)


def _dot(a, b):
    return jnp.dot(a.astype(BF16), b.astype(BF16), preferred_element_type=F32)


def _dot_nt(a, b):
    return lax.dot_general(a.astype(BF16), b.astype(BF16), (((1,), (1,)), ((), ())),
                           preferred_element_type=F32)


def _dot_tn(a, b):
    return lax.dot_general(a, b, (((0,), (0,)), ((), ())), preferred_element_type=F32)


def _rms(x, g):
    return x * lax.rsqrt(jnp.mean(x * x, axis=-1, keepdims=True) + EPS) * g


def _cumsum_rows(tri, v):
    hi = v.astype(BF16)
    lo = (v - hi.astype(F32)).astype(BF16)
    return (jnp.dot(tri, hi, preferred_element_type=F32)
            + jnp.dot(tri, lo, preferred_element_type=F32))


def _lower_bound(lbl_ref):
    l = lbl_ref[...]
    m = jnp.max(l, axis=0, keepdims=True)
    e = jnp.exp(l - m)
    return e[0:1, :] / jnp.sum(e, axis=0, keepdims=True)


def _row_to_col(row):
    n = row.shape[1]
    eye = lax.broadcasted_iota(jnp.int32, (n, n), 0) == lax.broadcasted_iota(jnp.int32, (n, n), 1)
    return jnp.sum(jnp.where(eye, jnp.broadcast_to(row, (n, n)), 0.0), axis=1, keepdims=True)


def _rotary(x, cs, sn):
    return x * cs + pltpu.roll(x, DH // 2, 1) * sn


def _hgrn_out(o, gh, gate):
    on = o * lax.rsqrt(jnp.mean(o * o, axis=-1, keepdims=True) + EPS) * gh
    return on * jax.nn.sigmoid(gate)


def _ret_out(o, gr, gate):
    mu = jnp.mean(o, axis=-1, keepdims=True)
    xc = o - mu
    var = jnp.mean(xc * xc, axis=-1, keepdims=True)
    return xc * lax.rsqrt(var + EPS) * gr * (gate * jax.nn.sigmoid(gate))


def _mixer_prompt_kernel(x_ref, nm_ref, win_ref, lbl_ref, gh_ref, gr_ref, cos_ref, sin_ref,
                         dmat_ref, qin_ref, kout_ref, gall_ref, tri_ref,
                         mixed_ref, sh_ref, sr_ref, proj_ref):
    @pl.when(pl.program_id(1) == 0)
    def _():
        sh_ref[...] = jnp.zeros_like(sh_ref)
        sr_ref[...] = jnp.zeros_like(sr_ref)

    xn = _rms(x_ref[0], nm_ref[...]).astype(BF16)
    proj_ref[...] = jnp.dot(xn, win_ref[...], preferred_element_type=F32)
    lb = _lower_bound(lbl_ref)
    gh = gh_ref[...]
    gr = gr_ref[...]
    c = HGRN_CHUNK
    causal = (lax.broadcasted_iota(jnp.int32, (c, c), 0) >= lax.broadcasted_iota(jnp.int32, (c, c), 1))

    def hgrn_step(ci, carry):
        rows = pl.ds(pl.multiple_of(ci * c, c), c)
        hq = proj_ref[rows, 0:GW] * (DH ** -0.5)
        f = lb + (1.0 - lb) * jax.nn.sigmoid(proj_ref[rows, GW:2 * GW])
        kk = 1.0 - f
        lc = _cumsum_rows(tri_ref[...], jnp.log(f))
        mid = lc[c // 2 - 1:c // 2, :]
        last = lc[c - 1:c, :]
        qd = hq * jnp.exp(jnp.minimum(lc - mid, EXP_CLAMP))
        kd = kk * jnp.exp(jnp.minimum(mid - lc, EXP_CLAMP))
        qs = hq * jnp.exp(lc)
        ks = kk * jnp.exp(last - lc)
        dl = jnp.exp(last)
        for h in range(HEADS):
            sl = slice(h * DH, (h + 1) * DH)
            v = proj_ref[rows, 2 * GW + h * DH:2 * GW + (h + 1) * DH].astype(BF16)
            a = jnp.where(causal, _dot_nt(qd[:, sl], kd[:, sl]), 0.0)
            s_old = sh_ref[0, h]
            o = _dot(a, v) + _dot(qs[:, sl], s_old)
            sh_ref[0, h] = _row_to_col(dl[:, sl]) * s_old + _dot_tn(ks[:, sl].astype(BF16), v)
            gate = proj_ref[rows, 3 * GW + h * DH:3 * GW + (h + 1) * DH]
            mixed_ref[0, rows, sl] = _hgrn_out(o, gh, gate).astype(BF16)
        return carry

    lax.fori_loop(0, T_TILE // c, hgrn_step, 0)

    cr = RET_CHUNK

    def ret_step(ci, carry):
        rows = pl.ds(pl.multiple_of(ci * cr, cr), cr)
        cs = cos_ref[rows, :]
        sn = sin_ref[rows, :]
        for h in range(HEADS):
            base = 4 * GW + h * DH
            q = _rotary(proj_ref[rows, base:base + DH], cs, sn)
            k = _rotary(proj_ref[rows, base + GW:base + GW + DH], cs, sn) * (DH ** -0.5)
            v = proj_ref[rows, base + 2 * GW:base + 2 * GW + DH].astype(BF16)
            sc = _dot_nt(q, k) * dmat_ref[h]
            s_old = sr_ref[0, h]
            o = _dot(sc, v) + _dot(q * qin_ref[h], s_old)
            sr_ref[0, h] = gall_ref[h] * s_old + _dot_tn((k * kout_ref[h]).astype(BF16), v)
            gate = proj_ref[rows, base + 3 * GW:base + 3 * GW + DH]
            mixed_ref[0, rows, GW + h * DH:GW + (h + 1) * DH] = _ret_out(o, gr, gate).astype(BF16)
        return carry

    lax.fori_loop(0, T_TILE // cr, ret_step, 0)


def _mixer_prompt(x, nm, win, lbl, gh, gr, cos, sin, dmat, qin, kout, gall, tri):
    b, t, _ = x.shape
    full = lambda shape: pl.BlockSpec(shape, lambda i, j: (0,) * len(shape))
    state = pl.BlockSpec((1, HEADS, DH, DH), lambda i, j: (i, 0, 0, 0))
    return pl.pallas_call(
        _mixer_prompt_kernel,
        grid=(b, t // T_TILE),
        in_specs=[
            pl.BlockSpec((1, T_TILE, D), lambda i, j: (i, j, 0)),
            full((1, D)), full((D, IN_COLS)), full(lbl.shape), full((1, DH)), full((1, DH)),
            pl.BlockSpec((T_TILE, DH), lambda i, j: (j, 0)),
            pl.BlockSpec((T_TILE, DH), lambda i, j: (j, 0)),
            full(dmat.shape), full(qin.shape), full(kout.shape), full(gall.shape), full(tri.shape),
        ],
        out_specs=[pl.BlockSpec((1, T_TILE, D), lambda i, j: (i, j, 0)), state, state],
        out_shape=[jax.ShapeDtypeStruct((b, t, D), BF16),
                   jax.ShapeDtypeStruct((b, HEADS, DH, DH), F32),
                   jax.ShapeDtypeStruct((b, HEADS, DH, DH), F32)],
        scratch_shapes=[pltpu.VMEM((T_TILE, IN_COLS), F32)],
        compiler_params=_cparams("parallel", "arbitrary"),
        name="mixer_prompt",
    )(x, nm, win, lbl, gh, gr, cos, sin, dmat, qin, kout, gall, tri)


def _mixer_sample_kernel(x_ref, nm_ref, win_ref, lbl_ref, gh_ref, gr_ref, cos_ref, sin_ref,
                         dmat_ref, qin_ref, kout_ref, gall_ref, tri_ref, ones_ref,
                         shin_ref, srin_ref, mixed_ref, sh_ref, sr_ref, proj_ref):
    nb, ts, _ = x_ref.shape
    rows_n = nb * ts
    xn = _rms(x_ref[...].reshape(rows_n, D), nm_ref[...]).astype(BF16)
    proj_ref[...] = jnp.dot(xn, win_ref[...], preferred_element_type=F32)
    lb = _lower_bound(lbl_ref)
    gh = gh_ref[...]
    gr = gr_ref[...]
    causal = tri_ref[...] > 0

    hq = proj_ref[:, 0:GW] * (DH ** -0.5)
    f = lb + (1.0 - lb) * jax.nn.sigmoid(proj_ref[:, GW:2 * GW])
    kk = 1.0 - f
    lf = jnp.log(f)
    lc = _cumsum_rows(tri_ref[...], lf)
    last = _cumsum_rows(ones_ref[...], lf)
    qd = hq * jnp.exp(lc)
    kd = kk * jnp.exp(jnp.minimum(-lc, EXP_CLAMP))
    ks = kk * jnp.exp(last - lc)
    dl = jnp.exp(last)
    for h in range(HEADS):
        sl = slice(h * DH, (h + 1) * DH)
        v = proj_ref[:, 2 * GW + h * DH:2 * GW + (h + 1) * DH]
        a = jnp.where(causal, _dot_nt(qd[:, sl], kd[:, sl]), 0.0)
        o_intra = _dot(a, v)
        gate = proj_ref[:, 3 * GW + h * DH:3 * GW + (h + 1) * DH]
        for b in range(nb):
            r = slice(b * ts, (b + 1) * ts)
            s_old = shin_ref[b, h]
            o = o_intra[r] + _dot(qd[r, sl], s_old)
            sh_ref[b, h] = (_row_to_col(dl[b * ts:b * ts + 1, sl]) * s_old
                            + _dot_tn(ks[r, sl], v[r]))
            mixed_ref[r, sl] = _hgrn_out(o, gh, gate[r])

    cs = cos_ref[...]
    sn = sin_ref[...]
    for h in range(HEADS):
        base = 4 * GW + h * DH
        q = _rotary(proj_ref[:, base:base + DH], cs, sn)
        k = _rotary(proj_ref[:, base + GW:base + GW + DH], cs, sn) * (DH ** -0.5)
        v = proj_ref[:, base + 2 * GW:base + 2 * GW + DH]
        sc = _dot_nt(q, k) * dmat_ref[h]
        o_intra = _dot(sc, v)
        qi = q * qin_ref[h]
        ko = k * kout_ref[h]
        gate = proj_ref[:, base + 3 * GW:base + 3 * GW + DH]
        for b in range(nb):
            r = slice(b * ts, (b + 1) * ts)
            s_old = srin_ref[b, h]
            o = o_intra[r] + _dot(qi[r], s_old)
            sr_ref[b, h] = gall_ref[h] * s_old + _dot_tn(ko[r], v[r])
            mixed_ref[r, GW + h * DH:GW + (h + 1) * DH] = _ret_out(o, gr, gate[r])


def _mixer_sample(x, nm, win, lbl, gh, gr, cos, sin, dmat, qin, kout, gall, tri, ones, sh, sr):
    b, t, _ = x.shape
    nb = S_BATCH
    full = lambda shape: pl.BlockSpec(shape, lambda i: (0,) * len(shape))
    state = pl.BlockSpec((nb, HEADS, DH, DH), lambda i: (i, 0, 0, 0))
    return pl.pallas_call(
        _mixer_sample_kernel,
        grid=(b // nb,),
        in_specs=[
            pl.BlockSpec((nb, t, D), lambda i: (i, 0, 0)),
            full((1, D)), full((D, IN_COLS)), full(lbl.shape), full((1, DH)), full((1, DH)),
            full(cos.shape), full(sin.shape),
            full(dmat.shape), full(qin.shape), full(kout.shape), full(gall.shape),
            full(tri.shape), full(ones.shape), state, state,
        ],
        out_specs=[pl.BlockSpec((nb * t, D), lambda i: (i, 0)), state, state],
        out_shape=[jax.ShapeDtypeStruct((b * t, D), F32),
                   jax.ShapeDtypeStruct(sh.shape, F32),
                   jax.ShapeDtypeStruct(sr.shape, F32)],
        scratch_shapes=[pltpu.VMEM((nb * t, IN_COLS), F32)],
        compiler_params=_cparams("parallel"),
        name="mixer_sample",
    )(x, nm, win, lbl, gh, gr, cos, sin, dmat, qin, kout, gall, tri, ones, sh, sr)


def _memkv_kernel(mem_ref, g_ref, wk_ref, wv_ref, k_ref, v_ref):
    mn = _rms(mem_ref[0], g_ref[...]).astype(BF16)
    k = jnp.dot(mn, wk_ref[...], preferred_element_type=F32)
    v = jnp.dot(mn, wv_ref[...], preferred_element_type=F32)
    for h in range(X_HEADS):
        k_ref[0, h] = k[:, h * X_DH:(h + 1) * X_DH]
        v_ref[0, h] = v[:, h * X_DH:(h + 1) * X_DH]


def _memkv(mem, g, wk, wv):
    b = mem.shape[0]
    full = lambda shape: pl.BlockSpec(shape, lambda i: (0,) * len(shape))
    kv = pl.BlockSpec((1, X_HEADS, N_MEM, X_DH), lambda i: (i, 0, 0, 0))
    shape = jax.ShapeDtypeStruct((b, X_HEADS, N_MEM, X_DH), F32)
    return pl.pallas_call(
        _memkv_kernel,
        grid=(b,),
        in_specs=[pl.BlockSpec((1, N_MEM, D), lambda i: (i, 0, 0)), full((1, D)), full((D, D)), full((D, D))],
        out_specs=[kv, kv],
        out_shape=[shape, shape],
        compiler_params=_cparams("parallel"),
        name="mem_kv",
    )(mem, g, wk, wv)


def _pre_attn_kernel(x_ref, mixed_ref, wout_ref, nx_ref, wxq_ref, h1_ref, q_ref):
    h1 = x_ref[...] + _dot(mixed_ref[...], wout_ref[...])
    h1_ref[...] = h1
    hn = _rms(h1, nx_ref[...])
    q_ref[...] = _dot(hn, wxq_ref[...]).astype(q_ref.dtype)


def _pre_attn(x, mixed, wout, nx, wxq, q_dtype):
    n = x.shape[0]
    row = pl.BlockSpec((ROW_TILE, D), lambda i: (i, 0))
    full = lambda shape: pl.BlockSpec(shape, lambda i: (0,) * len(shape))
    return pl.pallas_call(
        _pre_attn_kernel,
        grid=(n // ROW_TILE,),
        in_specs=[row, row, full((D, D)), full((1, D)), full((D, D))],
        out_specs=[row, row],
        out_shape=[jax.ShapeDtypeStruct((n, D), F32), jax.ShapeDtypeStruct((n, D), q_dtype)],
        compiler_params=_cparams("parallel"),
        name="pre_attn",
    )(x, mixed, wout, nx, wxq)


def _post_attn_kernel(h1_ref, o_ref, wxo_ref, nf_ref, wr_ref, br_ref, h2_ref, hn_ref, lg_ref):
    h2 = h1_ref[...] + _dot(o_ref[...], wxo_ref[...])
    h2_ref[...] = h2
    hn = _rms(h2, nf_ref[...])
    hn_ref[...] = hn
    lg_ref[...] = _dot(hn, wr_ref[...]) + br_ref[...]


def _post_attn(h1, o, wxo, nf, wr, br):
    n = h1.shape[0]
    row = pl.BlockSpec((ROW_TILE, D), lambda i: (i, 0))
    full = lambda shape: pl.BlockSpec(shape, lambda i: (0,) * len(shape))
    return pl.pallas_call(
        _post_attn_kernel,
        grid=(n // ROW_TILE,),
        in_specs=[row, row, full((D, D)), full((1, D)), full((D, 128)), full((1, 128))],
        out_specs=[row, row, pl.BlockSpec((ROW_TILE, 128), lambda i: (i, 0))],
        out_shape=[jax.ShapeDtypeStruct((n, D), F32), jax.ShapeDtypeStruct((n, D), F32),
                   jax.ShapeDtypeStruct((n, 128), F32)],
        compiler_params=_cparams("parallel"),
        name="post_attn",
    )(h1, o, wxo, nf, wr, br)


def _softmax_rows(s):
    m = jnp.max(s, axis=-1, keepdims=True)
    e = jnp.exp(s - m)
    return e / jnp.sum(e, axis=-1, keepdims=True)


def _attn_prompt_kernel(q_ref, k_ref, v_ref, o_ref):
    for h in range(X_HEADS):
        sl = slice(h * X_DH, (h + 1) * X_DH)
        s = _dot_nt(q_ref[:, sl], k_ref[0, h]) * (X_DH ** -0.5)
        o_ref[:, sl] = _dot(_softmax_rows(s), v_ref[0, h]).astype(o_ref.dtype)


def _attn_prompt(q, mk, mv, seq):
    n = q.shape[0]
    per_seq = seq // Q_TILE
    kv = pl.BlockSpec((1, X_HEADS, N_MEM, X_DH), lambda i, j: (i, 0, 0, 0))
    qs = pl.BlockSpec((Q_TILE, D), lambda i, j: (i * per_seq + j, 0))
    return pl.pallas_call(
        _attn_prompt_kernel,
        grid=(n // seq, per_seq),
        in_specs=[qs, kv, kv],
        out_specs=qs,
        out_shape=jax.ShapeDtypeStruct((n, D), BF16),
        compiler_params=_cparams("parallel", "arbitrary"),
        name="attn_prompt",
    )(q, mk, mv)


def _attn_sample_kernel(q_ref, k_ref, v_ref, o_ref):
    nb = k_ref.shape[0]
    ts = q_ref.shape[0] // nb
    for b in range(nb):
        r = slice(b * ts, (b + 1) * ts)
        for h in range(X_HEADS):
            sl = slice(h * X_DH, (h + 1) * X_DH)
            s = _dot_nt(q_ref[r, sl], k_ref[b, h]) * (X_DH ** -0.5)
            o_ref[r, sl] = _dot(_softmax_rows(s), v_ref[b, h])


def _attn_sample(q, ck, cv, ts):
    n = q.shape[0]
    nb = A_BATCH
    kv = pl.BlockSpec((nb, X_HEADS, N_MEM, X_DH), lambda i: (i, 0, 0, 0))
    qs = pl.BlockSpec((nb * ts, D), lambda i: (i, 0))
    return pl.pallas_call(
        _attn_sample_kernel,
        grid=(n // (nb * ts),),
        in_specs=[qs, kv, kv],
        out_specs=qs,
        out_shape=jax.ShapeDtypeStruct((n, D), F32),
        compiler_params=_cparams("parallel"),
        name="attn_sample",
    )(q, ck, cv)


def _router_kernel(lg_ref, stril_ref, idx_ref, p_ref, rank_ref, cnt_ref, carry_ref):
    @pl.when(pl.program_id(0) == 0)
    def _():
        carry_ref[...] = jnp.zeros_like(carry_ref)

    n = lg_ref.shape[0]
    lane = lax.broadcasted_iota(jnp.int32, (n, 128), 1)
    lane_f = lane.astype(F32)
    l = jnp.where(lane < N_EXPERTS, lg_ref[...], -jnp.inf)
    tops, idxs, hots = [], [], []
    for _ in range(TOP_K):
        m = jnp.max(l, axis=1, keepdims=True)
        idx = jnp.min(jnp.where(l == m, lane_f, 128.0), axis=1, keepdims=True)
        hot = lane_f == idx
        l = jnp.where(hot, -jnp.inf, l)
        tops.append(m)
        idxs.append(idx)
        hots.append(hot)
    sel = jnp.where(hots[0] | hots[1] | hots[2] | hots[3], 1.0, 0.0)
    before = jnp.dot(stril_ref[...], sel.astype(BF16), preferred_element_type=F32) + carry_ref[...]
    carry_ref[...] += jnp.sum(sel, axis=0, keepdims=True)
    cnt_ref[...] = carry_ref[...]
    es = [jnp.exp(t - tops[0]) for t in tops]
    den = es[0] + es[1] + es[2] + es[3]
    idx_out = jnp.zeros((n, 128), jnp.int32)
    p_out = jnp.zeros((n, 128), F32)
    rank_out = jnp.zeros((n, 128), jnp.int32)
    for k in range(TOP_K):
        rank = jnp.sum(jnp.where(hots[k], before, 0.0), axis=1, keepdims=True).astype(jnp.int32)
        idx_out = jnp.where(lane == k, idxs[k].astype(jnp.int32), idx_out)
        p_out = jnp.where(lane == k, es[k] / den, p_out)
        rank_out = jnp.where(lane == k, rank, rank_out)
    idx_ref[...] = idx_out
    p_ref[...] = p_out
    rank_ref[...] = rank_out


def _router(logits, stril):
    n = logits.shape[0]
    row = pl.BlockSpec((R_TILE, 128), lambda i: (i, 0))
    one = pl.BlockSpec((1, 128), lambda i: (0, 0))
    return pl.pallas_call(
        _router_kernel,
        grid=(n // R_TILE,),
        in_specs=[row, pl.BlockSpec((R_TILE, R_TILE), lambda i: (0, 0))],
        out_specs=[row, row, row, one],
        out_shape=[jax.ShapeDtypeStruct((n, 128), jnp.int32), jax.ShapeDtypeStruct((n, 128), F32),
                   jax.ShapeDtypeStruct((n, 128), jnp.int32), jax.ShapeDtypeStruct((1, 128), F32)],
        scratch_shapes=[pltpu.VMEM((1, 128), F32)],
        compiler_params=_cparams("arbitrary"),
        name="router",
    )(logits, stril)


def _dest_kernel(idx_ref, rank_ref, offs_ref, dest_ref):
    n = idx_ref.shape[0]
    lane = lax.broadcasted_iota(jnp.int32, (n, 128), 1)
    idx = idx_ref[...]
    out = jnp.zeros((n, 128), jnp.int32)
    for k in range(TOP_K):
        hot = lane == idx[:, k:k + 1]
        off = jnp.sum(jnp.where(hot, offs_ref[...], 0.0), axis=1, keepdims=True)
        out = jnp.where(lane == k, off.astype(jnp.int32), out)
    dest_ref[...] = out + rank_ref[...]


def _dest(idx, rank, offs):
    n = idx.shape[0]
    row = pl.BlockSpec((R_TILE, 128), lambda i: (i, 0))
    return pl.pallas_call(
        _dest_kernel,
        grid=(n // R_TILE,),
        in_specs=[row, row, pl.BlockSpec((1, 128), lambda i: (0, 0))],
        out_specs=row,
        out_shape=jax.ShapeDtypeStruct((n, 128), jnp.int32),
        compiler_params=_cparams("parallel"),
        name="route_dest",
    )(idx, rank, offs)


def _dispatch_kernel(p_tiles, dest_p_ref, dest_s_ref, hn_p_ref, hn_s_ref, xs_ref, sem):
    nt = DISPATCH_TILE
    i = pl.program_id(0)

    def scatter(dest_ref, hn_ref, tile):
        base = tile * (nt * TOP_K)

        def copy(t, k):
            return pltpu.make_async_copy(hn_ref.at[pl.ds(t, 1)],
                                         xs_ref.at[pl.ds(dest_ref[base + t * TOP_K + k], 1)], sem)

        def start(t, carry):
            for k in range(TOP_K):
                copy(t, k).start()
            return carry

        def wait(t, carry):
            for k in range(TOP_K):
                copy(t, k).wait()
            return carry

        lax.fori_loop(0, nt, start, 0)
        lax.fori_loop(0, nt, wait, 0)

    @pl.when(i < p_tiles)
    def _():
        scatter(dest_p_ref, hn_p_ref, i)

    @pl.when(i >= p_tiles)
    def _():
        scatter(dest_s_ref, hn_s_ref, i - p_tiles)


def _dispatch(dest_p, dest_s, hn_p, hn_s):
    p_tiles = hn_p.shape[0] // DISPATCH_TILE
    s_tiles = hn_s.shape[0] // DISPATCH_TILE
    n_rows = (hn_p.shape[0] + hn_s.shape[0]) * TOP_K
    return pl.pallas_call(
        functools.partial(_dispatch_kernel, p_tiles),
        grid_spec=pltpu.PrefetchScalarGridSpec(
            num_scalar_prefetch=2,
            grid=(p_tiles + s_tiles,),
            in_specs=[pl.BlockSpec((DISPATCH_TILE, D), lambda i, dp, ds: (jnp.minimum(i, p_tiles - 1), 0)),
                      pl.BlockSpec((DISPATCH_TILE, D), lambda i, dp, ds: (jnp.maximum(i - p_tiles, 0), 0))],
            out_specs=pl.BlockSpec(memory_space=pl.ANY),
            scratch_shapes=[pltpu.SemaphoreType.DMA],
        ),
        out_shape=jax.ShapeDtypeStruct((n_rows, D), F32),
        compiler_params=_cparams("arbitrary"),
        name="moe_dispatch",
    )(dest_p, dest_s, hn_p, hn_s)


def _experts_kernel(tile_ref, exp_ref, lo_ref, hi_ref, first_ref,
                    xs_ref, wg_ref, bg_ref, wl_ref, bl_ref, wd_ref, bd_ref, ys_ref):
    i = pl.program_id(0)
    lo = lo_ref[i]
    hi = hi_ref[i]

    @pl.when(hi > lo)
    def _():
        x = xs_ref[...].astype(BF16)
        glu = jnp.minimum(jnp.dot(x, wg_ref[0], preferred_element_type=F32) + bg_ref[0], LIMIT)
        lin = jnp.clip(jnp.dot(x, wl_ref[0], preferred_element_type=F32) + bl_ref[0], -LIMIT, LIMIT)
        hmid = glu * jax.nn.sigmoid(ALPHA * glu) * (lin + 1.0)
        y = _dot(hmid, wd_ref[0]) + bd_ref[0]
        row = tile_ref[i] * M_TILE + lax.broadcasted_iota(jnp.int32, (M_TILE, 1), 0)
        mine = (row >= lo) & (row < hi)

        @pl.when(first_ref[i] == 1)
        def _():
            ys_ref[...] = jnp.where(mine, y, 0.0)

        @pl.when(first_ref[i] == 0)
        def _():
            ys_ref[...] = jnp.where(mine, y, ys_ref[...])


def _experts(meta, xs, wg, bg, wl, bl, wd, bd):
    tile, expert, lo, hi, first = meta
    rows = pl.BlockSpec((M_TILE, D), lambda i, t, e, l, h, f: (t[i], 0))
    wspec = pl.BlockSpec((1, D, D), lambda i, t, e, l, h, f: (e[i], 0, 0))
    bspec = pl.BlockSpec((1, 1, D), lambda i, t, e, l, h, f: (e[i], 0, 0))
    return pl.pallas_call(
        _experts_kernel,
        grid_spec=pltpu.PrefetchScalarGridSpec(
            num_scalar_prefetch=5,
            grid=(tile.shape[0],),
            in_specs=[rows, wspec, bspec, wspec, bspec, wspec, bspec],
            out_specs=rows,
        ),
        out_shape=jax.ShapeDtypeStruct(xs.shape, F32),
        compiler_params=_cparams("arbitrary"),
        name="moe_experts",
    )(tile, expert, lo, hi, first, xs, wg, bg, wl, bl, wd, bd)


def _combine_kernel(dest_ref, ys_ref, h2_ref, p_ref, nf_ref, y_ref, buf_ref, sem):
    nt = h2_ref.shape[0]
    base = pl.program_id(0) * (nt * TOP_K)

    def copy(t, k):
        return pltpu.make_async_copy(ys_ref.at[pl.ds(dest_ref[base + t * TOP_K + k], 1)],
                                     buf_ref.at[k, pl.ds(t, 1)], sem)

    def start(t, carry):
        for k in range(TOP_K):
            copy(t, k).start()
        return carry

    def wait(t, carry):
        for k in range(TOP_K):
            copy(t, k).wait()
        return carry

    lax.fori_loop(0, nt, start, 0)
    lax.fori_loop(0, nt, wait, 0)
    p = p_ref[...]
    h3 = h2_ref[...]
    for k in range(TOP_K):
        h3 = h3 + p[:, k:k + 1] * buf_ref[k]
    y_ref[...] = _rms(h3, nf_ref[...])


def _combine(dest_flat, ys, h2, p, nf):
    n = h2.shape[0]
    row = pl.BlockSpec((COMBINE_TILE, D), lambda i, d: (i, 0))
    return pl.pallas_call(
        _combine_kernel,
        grid_spec=pltpu.PrefetchScalarGridSpec(
            num_scalar_prefetch=1,
            grid=(n // COMBINE_TILE,),
            in_specs=[pl.BlockSpec(memory_space=pl.ANY), row,
                      pl.BlockSpec((COMBINE_TILE, 128), lambda i, d: (i, 0)),
                      pl.BlockSpec((1, D), lambda i, d: (0, 0))],
            out_specs=row,
            scratch_shapes=[pltpu.VMEM((TOP_K, COMBINE_TILE, D), F32), pltpu.SemaphoreType.DMA],
        ),
        out_shape=jax.ShapeDtypeStruct((n, D), F32),
        compiler_params=_cparams("arbitrary"),
        name="moe_combine",
    )(dest_flat, ys, h2, p, nf)


def _rope_tables(pos):
    half = DH // 2
    inv = np.power(ROPE_BASE, -np.arange(half, dtype=np.float64) / half)
    ang = pos.astype(np.float64)[:, None] * inv[None, :]
    cos = np.concatenate([np.cos(ang), np.cos(ang)], axis=1)
    sin = np.concatenate([-np.sin(ang), np.sin(ang)], axis=1)
    return jnp.asarray(cos, F32), jnp.asarray(sin, F32)


def _retention_tables(c, reps):
    log_g = np.log1p(-np.exp2(-5.0 - np.arange(HEADS, dtype=np.float64)))
    idx = np.arange(c, dtype=np.float64)
    rel = idx[:, None] - idx[None, :]
    dmat = np.where(rel >= 0, np.exp(log_g[:, None, None] * np.maximum(rel, 0.0)), 0.0)
    big = np.zeros((HEADS, c * reps, c * reps))
    for r in range(reps):
        big[:, r * c:(r + 1) * c, r * c:(r + 1) * c] = dmat
    q_in = np.tile(np.exp(log_g[:, None] * (idx + 1.0)), (1, reps))
    k_out = np.tile(np.exp(log_g[:, None] * (c - 1.0 - idx)), (1, reps))
    g_all = np.exp(log_g * c)
    bc = lambda a: jnp.asarray(np.broadcast_to(a[..., None], a.shape + (DH,)), F32)
    return jnp.asarray(big, F32), bc(q_in), bc(k_out), bc(g_all[:, None])


def _block_tri(c, reps, strict=False):
    idx = np.arange(c * reps)
    same = (idx[:, None] // c) == (idx[None, :] // c)
    low = idx[:, None] > idx[None, :] if strict else idx[:, None] >= idx[None, :]
    return jnp.asarray(same & low, BF16), jnp.asarray(same, BF16)


def _expert_schedule(counts, n_rows):
    n_tiles = n_rows // M_TILE
    n_items = n_tiles + N_EXPERTS - 1
    ends = jnp.cumsum(counts)
    starts = ends - counts
    first_tile = starts // M_TILE
    tiles_e = jnp.where(counts > 0, (ends - 1) // M_TILE - first_tile + 1, 0)
    item_end = jnp.cumsum(tiles_e)
    item_start = item_end - tiles_e
    total = item_end[-1]
    i = jnp.arange(n_items, dtype=jnp.int32)
    live = i < total
    ic = jnp.minimum(i, total - 1)
    e = jnp.minimum(jnp.searchsorted(item_end, ic, side="right"), N_EXPERTS - 1).astype(jnp.int32)
    tile = (first_tile[e] + ic - item_start[e]).astype(jnp.int32)
    lo = jnp.where(live, jnp.maximum(starts[e], tile * M_TILE), 0).astype(jnp.int32)
    hi = jnp.where(live, jnp.minimum(ends[e], (tile + 1) * M_TILE), 0).astype(jnp.int32)
    prev = jnp.concatenate([jnp.full((1,), -1, jnp.int32), tile[:-1]])
    first = (live & (tile != prev)).astype(jnp.int32)
    return tile, e, lo, hi, first


def kernel(x_prompt, x_sample, state_hgrn, state_ret, cache_mem_k, cache_mem_v, mem_prompt, norm_mix, w_in, hgrn_lb_logits, hgrn_out_norm, ret_out_norm, w_out, norm_x, norm_mem, w_xq, w_mk, w_mv, w_xo, norm_ffn, w_router, b_router, w_gate_up, b_gate_up, w_down, b_down, norm_final):
    bp, tp, _ = x_prompt.shape
    bs, ts, _ = x_sample.shape
    n_p, n_s = bp * tp, bs * ts
    past_len = 16384

    row = lambda a: a.reshape(1, -1).astype(F32)
    win = w_in[0].astype(BF16)
    wout = w_out[0].astype(BF16)
    wxq = w_xq[0].astype(BF16)
    wxo = w_xo[0].astype(BF16)
    wmk = w_mk[0].astype(BF16)
    wmv = w_mv[0].astype(BF16)
    wr = jnp.pad(w_router[0], ((0, 0), (0, 128 - N_EXPERTS))).astype(BF16)
    br = jnp.pad(b_router[0], (0, 128 - N_EXPERTS)).reshape(1, 128).astype(F32)
    wg = w_gate_up[0][:, :, 0::2].astype(BF16)
    wl = w_gate_up[0][:, :, 1::2].astype(BF16)
    bg = b_gate_up[0][:, 0::2].reshape(N_EXPERTS, 1, D)
    bl = b_gate_up[0][:, 1::2].reshape(N_EXPERTS, 1, D)
    wd = w_down[0].astype(BF16)
    bd = b_down[0].reshape(N_EXPERTS, 1, D)
    nm, nx, nmem, nffn, nfin = row(norm_mix[0]), row(norm_x[0]), row(norm_mem[0]), row(norm_ffn[0]), row(norm_final)
    gh, gr = row(hgrn_out_norm[0]), row(ret_out_norm[0])
    lbl = hgrn_lb_logits.astype(F32)

    cos_p, sin_p = _rope_tables(np.arange(tp))
    dmat_p, qin_p, kout_p, gall_p = _retention_tables(RET_CHUNK, 1)
    tri_p, _ = _block_tri(HGRN_CHUNK, 1)
    mixed_p, sh_p, sr_p = _mixer_prompt(x_prompt, nm, win, lbl, gh, gr, cos_p, sin_p,
                                        dmat_p, qin_p, kout_p, gall_p, tri_p)
    cos_s, sin_s = _rope_tables(np.tile(past_len + np.arange(ts), S_BATCH))
    dmat_s, qin_s, kout_s, gall_s = _retention_tables(ts, S_BATCH)
    tri_s, ones_s = _block_tri(ts, S_BATCH)
    mixed_s, sh_s, sr_s = _mixer_sample(x_sample, nm, win, lbl, gh, gr, cos_s, sin_s,
                                        dmat_s, qin_s, kout_s, gall_s, tri_s, ones_s,
                                        state_hgrn[0], state_ret[0])

    mk_p, mv_p = _memkv(mem_prompt, nmem, wmk, wmv)
    h1_p, q_p = _pre_attn(x_prompt.reshape(n_p, D), mixed_p.reshape(n_p, D), wout, nx, wxq, BF16)
    h1_s, q_s = _pre_attn(x_sample.reshape(n_s, D), mixed_s, wout, nx, wxq, F32)
    o_p = _attn_prompt(q_p, mk_p, mv_p, tp)
    o_s = _attn_sample(q_s, cache_mem_k[0], cache_mem_v[0], ts)
    h2_p, hn_p, lg_p = _post_attn(h1_p, o_p, wxo, nffn, wr, br)
    h2_s, hn_s, lg_s = _post_attn(h1_s, o_s, wxo, nffn, wr, br)

    stril, _ = _block_tri(R_TILE, 1, strict=True)
    idx, p, rank, counts = _router(jnp.concatenate([lg_p, lg_s], axis=0), stril)
    counts = counts[0, :N_EXPERTS].astype(jnp.int32)
    offs = jnp.pad(jnp.cumsum(counts) - counts, (0, 128 - N_EXPERTS)).reshape(1, 128).astype(F32)
    dest = _dest(idx, rank, offs)[:, :TOP_K].reshape(-1)
    dest_p, dest_s = dest[:n_p * TOP_K], dest[n_p * TOP_K:]

    n_rows = (n_p + n_s) * TOP_K
    xs = _dispatch(dest_p, dest_s, hn_p, hn_s)
    ys = _experts(_expert_schedule(counts, n_rows), xs, wg, bg, wl, bl, wd, bd)
    y_p = _combine(dest_p, ys, h2_p, p[:n_p], nfin)
    y_s = _combine(dest_s, ys, h2_s, p[n_p:], nfin)

    return (y_p.reshape(bp, tp, D), y_s.reshape(bs, ts, D),
            sh_p[None], sr_p[None], mk_p[None], mv_p[None], sh_s[None], sr_s[None])
```

```python
import functools

import numpy as np
import jax
import jax.numpy as jnp
from jax import lax
from jax.experimental import pallas as pl
from jax.experimental.pallas import tpu as pltpu

F32 = jnp.float32
BF16 = jnp.bfloat16

D = 1024
HEADS = 4
DH = 128
GW = HEADS * DH
IN_COLS = 8 * GW
ROPE_BASE = 10000.0
N_MEM = 256
X_HEADS = 4
X_DH = D // X_HEADS
N_EXPERTS = 32
TOP_K = 4
LIMIT = 7.0
ALPHA = 1.702
EPS = 1e-6
EXP_CLAMP = 80.0

HGRN_CHUNK = 64
RET_CHUNK = 128
T_TILE = 256
S_BATCH = 8
ROW_TILE = 512
Q_TILE = 512
A_BATCH = 4
R_TILE = 512
M_TILE = 256
DISPATCH_TILE = 256
COMBINE_TILE = 128
VMEM_LIMIT = 56 * 1024 * 1024


def _cparams(*sem):
    return pltpu.CompilerParams(dimension_semantics=sem, vmem_limit_bytes=VMEM_LIMIT)


def _dot(a, b):
    return jnp.dot(a.astype(BF16), b.astype(BF16), preferred_element_type=F32)


def _dot_nt(a, b):
    return lax.dot_general(a.astype(BF16), b.astype(BF16), (((1,), (1,)), ((), ())),
                           preferred_element_type=F32)


def _dot_tn(a, b):
    return lax.dot_general(a, b, (((0,), (0,)), ((), ())), preferred_element_type=F32)


def _rms(x, g):
    return x * lax.rsqrt(jnp.mean(x * x, axis=-1, keepdims=True) + EPS) * g


def _cumsum_rows(tri, v):
    hi = v.astype(BF16)
    lo = (v - hi.astype(F32)).astype(BF16)
    return (jnp.dot(tri, hi, preferred_element_type=F32)
            + jnp.dot(tri, lo, preferred_element_type=F32))


def _lower_bound(lbl_ref):
    l = lbl_ref[...]
    m = jnp.max(l, axis=0, keepdims=True)
    e = jnp.exp(l - m)
    return e[0:1, :] / jnp.sum(e, axis=0, keepdims=True)


def _row_to_col(row):
    n = row.shape[1]
    eye = lax.broadcasted_iota(jnp.int32, (n, n), 0) == lax.broadcasted_iota(jnp.int32, (n, n), 1)
    return jnp.sum(jnp.where(eye, jnp.broadcast_to(row, (n, n)), 0.0), axis=1, keepdims=True)


def _rotary(x, cs, sn):
    return x * cs + pltpu.roll(x, DH // 2, 1) * sn


def _hgrn_out(o, gh, gate):
    on = o * lax.rsqrt(jnp.mean(o * o, axis=-1, keepdims=True) + EPS) * gh
    return on * jax.nn.sigmoid(gate)


def _ret_out(o, gr, gate):
    mu = jnp.mean(o, axis=-1, keepdims=True)
    xc = o - mu
    var = jnp.mean(xc * xc, axis=-1, keepdims=True)
    return xc * lax.rsqrt(var + EPS) * gr * (gate * jax.nn.sigmoid(gate))


def _mixer_prompt_kernel(x_ref, nm_ref, win_ref, lbl_ref, gh_ref, gr_ref, cos_ref, sin_ref,
                         dmat_ref, qin_ref, kout_ref, gall_ref, tri_ref,
                         mixed_ref, sh_ref, sr_ref, proj_ref):
    @pl.when(pl.program_id(1) == 0)
    def _():
        sh_ref[...] = jnp.zeros_like(sh_ref)
        sr_ref[...] = jnp.zeros_like(sr_ref)

    xn = _rms(x_ref[0], nm_ref[...]).astype(BF16)
    proj_ref[...] = jnp.dot(xn, win_ref[...], preferred_element_type=F32)
    lb = _lower_bound(lbl_ref)
    gh = gh_ref[...]
    gr = gr_ref[...]
    c = HGRN_CHUNK
    causal = (lax.broadcasted_iota(jnp.int32, (c, c), 0) >= lax.broadcasted_iota(jnp.int32, (c, c), 1))

    def hgrn_step(ci, carry):
        rows = pl.ds(pl.multiple_of(ci * c, c), c)
        hq = proj_ref[rows, 0:GW] * (DH ** -0.5)
        f = lb + (1.0 - lb) * jax.nn.sigmoid(proj_ref[rows, GW:2 * GW])
        kk = 1.0 - f
        lc = _cumsum_rows(tri_ref[...], jnp.log(f))
        mid = lc[c // 2 - 1:c // 2, :]
        last = lc[c - 1:c, :]
        qd = hq * jnp.exp(jnp.minimum(lc - mid, EXP_CLAMP))
        kd = kk * jnp.exp(jnp.minimum(mid - lc, EXP_CLAMP))
        qs = hq * jnp.exp(lc)
        ks = kk * jnp.exp(last - lc)
        dl = jnp.exp(last)
        for h in range(HEADS):
            sl = slice(h * DH, (h + 1) * DH)
            v = proj_ref[rows, 2 * GW + h * DH:2 * GW + (h + 1) * DH].astype(BF16)
            a = jnp.where(causal, _dot_nt(qd[:, sl], kd[:, sl]), 0.0)
            s_old = sh_ref[0, h]
            o = _dot(a, v) + _dot(qs[:, sl], s_old)
            sh_ref[0, h] = _row_to_col(dl[:, sl]) * s_old + _dot_tn(ks[:, sl].astype(BF16), v)
            gate = proj_ref[rows, 3 * GW + h * DH:3 * GW + (h + 1) * DH]
            mixed_ref[0, rows, sl] = _hgrn_out(o, gh, gate).astype(BF16)
        return carry

    lax.fori_loop(0, T_TILE // c, hgrn_step, 0, unroll=True)

    cr = RET_CHUNK

    def ret_step(ci, carry):
        rows = pl.ds(pl.multiple_of(ci * cr, cr), cr)
        cs = cos_ref[rows, :]
        sn = sin_ref[rows, :]
        for h in range(HEADS):
            base = 4 * GW + h * DH
            q = _rotary(proj_ref[rows, base:base + DH], cs, sn)
            k = _rotary(proj_ref[rows, base + GW:base + GW + DH], cs, sn) * (DH ** -0.5)
            v = proj_ref[rows, base + 2 * GW:base + 2 * GW + DH].astype(BF16)
            sc = _dot_nt(q, k) * dmat_ref[h]
            s_old = sr_ref[0, h]
            o = _dot(sc, v) + _dot(q * qin_ref[h], s_old)
            sr_ref[0, h] = gall_ref[h] * s_old + _dot_tn((k * kout_ref[h]).astype(BF16), v)
            gate = proj_ref[rows, base + 3 * GW:base + 3 * GW + DH]
            mixed_ref[0, rows, GW + h * DH:GW + (h + 1) * DH] = _ret_out(o, gr, gate).astype(BF16)
        return carry

    lax.fori_loop(0, T_TILE // cr, ret_step, 0, unroll=True)


def _mixer_prompt(x, nm, win, lbl, gh, gr, cos, sin, dmat, qin, kout, gall, tri):
    b, t, _ = x.shape
    full = lambda shape: pl.BlockSpec(shape, lambda i, j: (0,) * len(shape))
    state = pl.BlockSpec((1, HEADS, DH, DH), lambda i, j: (i, 0, 0, 0))
    return pl.pallas_call(
        _mixer_prompt_kernel,
        grid=(b, t // T_TILE),
        in_specs=[
            pl.BlockSpec((1, T_TILE, D), lambda i, j: (i, j, 0)),
            full((1, D)), full((D, IN_COLS)), full(lbl.shape), full((1, DH)), full((1, DH)),
            pl.BlockSpec((T_TILE, DH), lambda i, j: (j, 0)),
            pl.BlockSpec((T_TILE, DH), lambda i, j: (j, 0)),
            full(dmat.shape), full(qin.shape), full(kout.shape), full(gall.shape), full(tri.shape),
        ],
        out_specs=[pl.BlockSpec((1, T_TILE, D), lambda i, j: (i, j, 0)), state, state],
        out_shape=[jax.ShapeDtypeStruct((b, t, D), BF16),
                   jax.ShapeDtypeStruct((b, HEADS, DH, DH), F32),
                   jax.ShapeDtypeStruct((b, HEADS, DH, DH), F32)],
        scratch_shapes=[pltpu.VMEM((T_TILE, IN_COLS), F32)],
        compiler_params=_cparams("parallel", "arbitrary"),
        name="mixer_prompt",
    )(x, nm, win, lbl, gh, gr, cos, sin, dmat, qin, kout, gall, tri)


def _mixer_sample_kernel(x_ref, nm_ref, win_ref, lbl_ref, gh_ref, gr_ref, cos_ref, sin_ref,
                         dmat_ref, qin_ref, kout_ref, gall_ref, tri_ref, ones_ref,
                         shin_ref, srin_ref, mixed_ref, sh_ref, sr_ref, proj_ref):
    nb, ts, _ = x_ref.shape
    rows_n = nb * ts
    xn = _rms(x_ref[...].reshape(rows_n, D), nm_ref[...]).astype(BF16)
    proj_ref[...] = jnp.dot(xn, win_ref[...], preferred_element_type=F32)
    lb = _lower_bound(lbl_ref)
    gh = gh_ref[...]
    gr = gr_ref[...]
    causal = tri_ref[...] > 0

    hq = proj_ref[:, 0:GW] * (DH ** -0.5)
    f = lb + (1.0 - lb) * jax.nn.sigmoid(proj_ref[:, GW:2 * GW])
    kk = 1.0 - f
    lf = jnp.log(f)
    lc = _cumsum_rows(tri_ref[...], lf)
    last = _cumsum_rows(ones_ref[...], lf)
    qd = hq * jnp.exp(lc)
    kd = kk * jnp.exp(jnp.minimum(-lc, EXP_CLAMP))
    ks = kk * jnp.exp(last - lc)
    dl = jnp.exp(last)
    for h in range(HEADS):
        sl = slice(h * DH, (h + 1) * DH)
        v = proj_ref[:, 2 * GW + h * DH:2 * GW + (h + 1) * DH]
        a = jnp.where(causal, _dot_nt(qd[:, sl], kd[:, sl]), 0.0)
        o_intra = _dot(a, v)
        gate = proj_ref[:, 3 * GW + h * DH:3 * GW + (h + 1) * DH]
        for b in range(nb):
            r = slice(b * ts, (b + 1) * ts)
            s_old = shin_ref[b, h]
            o = o_intra[r] + _dot(qd[r, sl], s_old)
            sh_ref[b, h] = (_row_to_col(dl[b * ts:b * ts + 1, sl]) * s_old
                            + _dot_tn(ks[r, sl], v[r]))
            mixed_ref[r, sl] = _hgrn_out(o, gh, gate[r])

    cs = cos_ref[...]
    sn = sin_ref[...]
    for h in range(HEADS):
        base = 4 * GW + h * DH
        q = _rotary(proj_ref[:, base:base + DH], cs, sn)
        k = _rotary(proj_ref[:, base + GW:base + GW + DH], cs, sn) * (DH ** -0.5)
        v = proj_ref[:, base + 2 * GW:base + 2 * GW + DH]
        sc = _dot_nt(q, k) * dmat_ref[h]
        o_intra = _dot(sc, v)
        qi = q * qin_ref[h]
        ko = k * kout_ref[h]
        gate = proj_ref[:, base + 3 * GW:base + 3 * GW + DH]
        for b in range(nb):
            r = slice(b * ts, (b + 1) * ts)
            s_old = srin_ref[b, h]
            o = o_intra[r] + _dot(qi[r], s_old)
            sr_ref[b, h] = gall_ref[h] * s_old + _dot_tn(ko[r], v[r])
            mixed_ref[r, GW + h * DH:GW + (h + 1) * DH] = _ret_out(o, gr, gate[r])


def _mixer_sample(x, nm, win, lbl, gh, gr, cos, sin, dmat, qin, kout, gall, tri, ones, sh, sr):
    b, t, _ = x.shape
    nb = S_BATCH
    full = lambda shape: pl.BlockSpec(shape, lambda i: (0,) * len(shape))
    state = pl.BlockSpec((nb, HEADS, DH, DH), lambda i: (i, 0, 0, 0))
    return pl.pallas_call(
        _mixer_sample_kernel,
        grid=(b // nb,),
        in_specs=[
            pl.BlockSpec((nb, t, D), lambda i: (i, 0, 0)),
            full((1, D)), full((D, IN_COLS)), full(lbl.shape), full((1, DH)), full((1, DH)),
            full(cos.shape), full(sin.shape),
            full(dmat.shape), full(qin.shape), full(kout.shape), full(gall.shape),
            full(tri.shape), full(ones.shape), state, state,
        ],
        out_specs=[pl.BlockSpec((nb * t, D), lambda i: (i, 0)), state, state],
        out_shape=[jax.ShapeDtypeStruct((b * t, D), F32),
                   jax.ShapeDtypeStruct(sh.shape, F32),
                   jax.ShapeDtypeStruct(sr.shape, F32)],
        scratch_shapes=[pltpu.VMEM((nb * t, IN_COLS), F32)],
        compiler_params=_cparams("parallel"),
        name="mixer_sample",
    )(x, nm, win, lbl, gh, gr, cos, sin, dmat, qin, kout, gall, tri, ones, sh, sr)


def _memkv_kernel(mem_ref, g_ref, wk_ref, wv_ref, k_ref, v_ref):
    mn = _rms(mem_ref[0], g_ref[...]).astype(BF16)
    k = jnp.dot(mn, wk_ref[...], preferred_element_type=F32)
    v = jnp.dot(mn, wv_ref[...], preferred_element_type=F32)
    for h in range(X_HEADS):
        k_ref[0, h] = k[:, h * X_DH:(h + 1) * X_DH]
        v_ref[0, h] = v[:, h * X_DH:(h + 1) * X_DH]


def _memkv(mem, g, wk, wv):
    b = mem.shape[0]
    full = lambda shape: pl.BlockSpec(shape, lambda i: (0,) * len(shape))
    kv = pl.BlockSpec((1, X_HEADS, N_MEM, X_DH), lambda i: (i, 0, 0, 0))
    shape = jax.ShapeDtypeStruct((b, X_HEADS, N_MEM, X_DH), F32)
    return pl.pallas_call(
        _memkv_kernel,
        grid=(b,),
        in_specs=[pl.BlockSpec((1, N_MEM, D), lambda i: (i, 0, 0)), full((1, D)), full((D, D)), full((D, D))],
        out_specs=[kv, kv],
        out_shape=[shape, shape],
        compiler_params=_cparams("parallel"),
        name="mem_kv",
    )(mem, g, wk, wv)


def _pre_attn_kernel(x_ref, mixed_ref, wout_ref, nx_ref, wxq_ref, h1_ref, q_ref):
    h1 = x_ref[...] + _dot(mixed_ref[...], wout_ref[...])
    h1_ref[...] = h1
    hn = _rms(h1, nx_ref[...])
    q_ref[...] = _dot(hn, wxq_ref[...]).astype(q_ref.dtype)


def _pre_attn(x, mixed, wout, nx, wxq, q_dtype):
    n = x.shape[0]
    row = pl.BlockSpec((ROW_TILE, D), lambda i: (i, 0))
    full = lambda shape: pl.BlockSpec(shape, lambda i: (0,) * len(shape))
    return pl.pallas_call(
        _pre_attn_kernel,
        grid=(n // ROW_TILE,),
        in_specs=[row, row, full((D, D)), full((1, D)), full((D, D))],
        out_specs=[row, row],
        out_shape=[jax.ShapeDtypeStruct((n, D), F32), jax.ShapeDtypeStruct((n, D), q_dtype)],
        compiler_params=_cparams("parallel"),
        name="pre_attn",
    )(x, mixed, wout, nx, wxq)


def _post_attn_kernel(h1_ref, o_ref, wxo_ref, nf_ref, wr_ref, br_ref, h2_ref, hn_ref, lg_ref):
    h2 = h1_ref[...] + _dot(o_ref[...], wxo_ref[...])
    h2_ref[...] = h2
    hn = _rms(h2, nf_ref[...])
    hn_ref[...] = hn
    lg_ref[...] = _dot(hn, wr_ref[...]) + br_ref[...]


def _post_attn(h1, o, wxo, nf, wr, br):
    n = h1.shape[0]
    row = pl.BlockSpec((ROW_TILE, D), lambda i: (i, 0))
    full = lambda shape: pl.BlockSpec(shape, lambda i: (0,) * len(shape))
    return pl.pallas_call(
        _post_attn_kernel,
        grid=(n // ROW_TILE,),
        in_specs=[row, row, full((D, D)), full((1, D)), full((D, 128)), full((1, 128))],
        out_specs=[row, row, pl.BlockSpec((ROW_TILE, 128), lambda i: (i, 0))],
        out_shape=[jax.ShapeDtypeStruct((n, D), F32), jax.ShapeDtypeStruct((n, D), F32),
                   jax.ShapeDtypeStruct((n, 128), F32)],
        compiler_params=_cparams("parallel"),
        name="post_attn",
    )(h1, o, wxo, nf, wr, br)


def _softmax_rows(s):
    m = jnp.max(s, axis=-1, keepdims=True)
    e = jnp.exp(s - m)
    return e / jnp.sum(e, axis=-1, keepdims=True)


def _attn_prompt_kernel(q_ref, k_ref, v_ref, o_ref):
    for h in range(X_HEADS):
        sl = slice(h * X_DH, (h + 1) * X_DH)
        s = _dot_nt(q_ref[:, sl], k_ref[0, h]) * (X_DH ** -0.5)
        o_ref[:, sl] = _dot(_softmax_rows(s), v_ref[0, h]).astype(o_ref.dtype)


def _attn_prompt(q, mk, mv, seq):
    n = q.shape[0]
    per_seq = seq // Q_TILE
    kv = pl.BlockSpec((1, X_HEADS, N_MEM, X_DH), lambda i, j: (i, 0, 0, 0))
    qs = pl.BlockSpec((Q_TILE, D), lambda i, j: (i * per_seq + j, 0))
    return pl.pallas_call(
        _attn_prompt_kernel,
        grid=(n // seq, per_seq),
        in_specs=[qs, kv, kv],
        out_specs=qs,
        out_shape=jax.ShapeDtypeStruct((n, D), BF16),
        compiler_params=_cparams("parallel", "arbitrary"),
        name="attn_prompt",
    )(q, mk, mv)


def _attn_sample_kernel(q_ref, k_ref, v_ref, o_ref):
    nb = k_ref.shape[0]
    ts = q_ref.shape[0] // nb
    for b in range(nb):
        r = slice(b * ts, (b + 1) * ts)
        for h in range(X_HEADS):
            sl = slice(h * X_DH, (h + 1) * X_DH)
            s = _dot_nt(q_ref[r, sl], k_ref[b, h]) * (X_DH ** -0.5)
            o_ref[r, sl] = _dot(_softmax_rows(s), v_ref[b, h])


def _attn_sample(q, ck, cv, ts):
    n = q.shape[0]
    nb = A_BATCH
    kv = pl.BlockSpec((nb, X_HEADS, N_MEM, X_DH), lambda i: (i, 0, 0, 0))
    qs = pl.BlockSpec((nb * ts, D), lambda i: (i, 0))
    return pl.pallas_call(
        _attn_sample_kernel,
        grid=(n // (nb * ts),),
        in_specs=[qs, kv, kv],
        out_specs=qs,
        out_shape=jax.ShapeDtypeStruct((n, D), F32),
        compiler_params=_cparams("parallel"),
        name="attn_sample",
    )(q, ck, cv)


def _router_kernel(lg_ref, stril_ref, idx_ref, p_ref, rank_ref, cnt_ref, carry_ref):
    @pl.when(pl.program_id(0) == 0)
    def _():
        carry_ref[...] = jnp.zeros_like(carry_ref)

    n = lg_ref.shape[0]
    lane = lax.broadcasted_iota(jnp.int32, (n, 128), 1)
    lane_f = lane.astype(F32)
    l = jnp.where(lane < N_EXPERTS, lg_ref[...], -jnp.inf)
    tops, idxs, hots = [], [], []
    for _ in range(TOP_K):
        m = jnp.max(l, axis=1, keepdims=True)
        idx = jnp.min(jnp.where(l == m, lane_f, 128.0), axis=1, keepdims=True)
        hot = lane_f == idx
        l = jnp.where(hot, -jnp.inf, l)
        tops.append(m)
        idxs.append(idx)
        hots.append(hot)
    sel = jnp.where(hots[0] | hots[1] | hots[2] | hots[3], 1.0, 0.0)
    before = jnp.dot(stril_ref[...], sel.astype(BF16), preferred_element_type=F32) + carry_ref[...]
    carry_ref[...] += jnp.sum(sel, axis=0, keepdims=True)
    cnt_ref[...] = carry_ref[...]
    es = [jnp.exp(t - tops[0]) for t in tops]
    den = es[0] + es[1] + es[2] + es[3]
    idx_out = jnp.zeros((n, 128), jnp.int32)
    p_out = jnp.zeros((n, 128), F32)
    rank_out = jnp.zeros((n, 128), jnp.int32)
    for k in range(TOP_K):
        rank = jnp.sum(jnp.where(hots[k], before, 0.0), axis=1, keepdims=True).astype(jnp.int32)
        idx_out = jnp.where(lane == k, idxs[k].astype(jnp.int32), idx_out)
        p_out = jnp.where(lane == k, es[k] / den, p_out)
        rank_out = jnp.where(lane == k, rank, rank_out)
    idx_ref[...] = idx_out
    p_ref[...] = p_out
    rank_ref[...] = rank_out


def _router(logits, stril):
    n = logits.shape[0]
    row = pl.BlockSpec((R_TILE, 128), lambda i: (i, 0))
    one = pl.BlockSpec((1, 128), lambda i: (0, 0))
    return pl.pallas_call(
        _router_kernel,
        grid=(n // R_TILE,),
        in_specs=[row, pl.BlockSpec((R_TILE, R_TILE), lambda i: (0, 0))],
        out_specs=[row, row, row, one],
        out_shape=[jax.ShapeDtypeStruct((n, 128), jnp.int32), jax.ShapeDtypeStruct((n, 128), F32),
                   jax.ShapeDtypeStruct((n, 128), jnp.int32), jax.ShapeDtypeStruct((1, 128), F32)],
        scratch_shapes=[pltpu.VMEM((1, 128), F32)],
        compiler_params=_cparams("arbitrary"),
        name="router",
    )(logits, stril)


def _dest_kernel(idx_ref, rank_ref, offs_ref, dest_ref):
    n = idx_ref.shape[0]
    lane = lax.broadcasted_iota(jnp.int32, (n, 128), 1)
    idx = idx_ref[...]
    out = jnp.zeros((n, 128), jnp.int32)
    for k in range(TOP_K):
        hot = lane == idx[:, k:k + 1]
        off = jnp.sum(jnp.where(hot, offs_ref[...], 0.0), axis=1, keepdims=True)
        out = jnp.where(lane == k, off.astype(jnp.int32), out)
    dest_ref[...] = out + rank_ref[...]


def _dest(idx, rank, offs):
    n = idx.shape[0]
    row = pl.BlockSpec((R_TILE, 128), lambda i: (i, 0))
    return pl.pallas_call(
        _dest_kernel,
        grid=(n // R_TILE,),
        in_specs=[row, row, pl.BlockSpec((1, 128), lambda i: (0, 0))],
        out_specs=row,
        out_shape=jax.ShapeDtypeStruct((n, 128), jnp.int32),
        compiler_params=_cparams("parallel"),
        name="route_dest",
    )(idx, rank, offs)


def _dispatch_kernel(p_tiles, dest_p_ref, dest_s_ref, hn_p_ref, hn_s_ref, xs_ref, sem):
    nt = DISPATCH_TILE
    i = pl.program_id(0)

    def scatter(dest_ref, hn_ref, tile):
        base = tile * (nt * TOP_K)

        def copy(t, k):
            return pltpu.make_async_copy(hn_ref.at[pl.ds(t, 1)],
                                         xs_ref.at[pl.ds(dest_ref[base + t * TOP_K + k], 1)], sem)

        def start(t, carry):
            for k in range(TOP_K):
                copy(t, k).start(priority=k % 2)
            return carry

        def wait(t, carry):
            for k in range(TOP_K):
                copy(t, k).wait()
            return carry

        lax.fori_loop(0, nt, start, 0, unroll=4)
        lax.fori_loop(0, nt, wait, 0, unroll=4)

    @pl.when(i < p_tiles)
    def _():
        scatter(dest_p_ref, hn_p_ref, i)

    @pl.when(i >= p_tiles)
    def _():
        scatter(dest_s_ref, hn_s_ref, i - p_tiles)


def _dispatch(dest_p, dest_s, hn_p, hn_s):
    p_tiles = hn_p.shape[0] // DISPATCH_TILE
    s_tiles = hn_s.shape[0] // DISPATCH_TILE
    n_rows = (hn_p.shape[0] + hn_s.shape[0]) * TOP_K
    return pl.pallas_call(
        functools.partial(_dispatch_kernel, p_tiles),
        grid_spec=pltpu.PrefetchScalarGridSpec(
            num_scalar_prefetch=2,
            grid=(p_tiles + s_tiles,),
            in_specs=[pl.BlockSpec((DISPATCH_TILE, D), lambda i, dp, ds: (jnp.minimum(i, p_tiles - 1), 0)),
                      pl.BlockSpec((DISPATCH_TILE, D), lambda i, dp, ds: (jnp.maximum(i - p_tiles, 0), 0))],
            out_specs=pl.BlockSpec(memory_space=pl.ANY),
            scratch_shapes=[pltpu.SemaphoreType.DMA],
        ),
        out_shape=jax.ShapeDtypeStruct((n_rows, D), F32),
        compiler_params=_cparams("arbitrary"),
        name="moe_dispatch",
    )(dest_p, dest_s, hn_p, hn_s)


def _experts_kernel(tile_ref, exp_ref, lo_ref, hi_ref, first_ref, fresh_ref,
                    xs_ref, wgu_ref, bg_ref, bl_ref, wd_ref, bd_ref, perm_ref, ys_ref,
                    wg_s, wl_s, wd_s):
    i = pl.program_id(0)
    lo = lo_ref[i]
    hi = hi_ref[i]

    @pl.when(fresh_ref[i] == 1)
    def _():
        pw = perm_ref.shape[0]
        for c in range(2 * D // pw):
            blk = wgu_ref[0, :, c * pw:(c + 1) * pw].astype(BF16)
            sep = jnp.dot(blk, perm_ref[...], preferred_element_type=F32)
            wg_s[:, c * pw // 2:(c + 1) * pw // 2] = sep[:, :pw // 2].astype(BF16)
            wl_s[:, c * pw // 2:(c + 1) * pw // 2] = sep[:, pw // 2:].astype(BF16)
        wd_s[...] = wd_ref[0].astype(BF16)

    @pl.when(hi > lo)
    def _():
        x = xs_ref[...].astype(BF16)
        glu = jnp.minimum(jnp.dot(x, wg_s[...], preferred_element_type=F32) + bg_ref[0], LIMIT)
        lin = jnp.clip(jnp.dot(x, wl_s[...], preferred_element_type=F32) + bl_ref[0], -LIMIT, LIMIT)
        hmid = glu * jax.nn.sigmoid(ALPHA * glu) * (lin + 1.0)
        y = _dot(hmid, wd_s[...]) + bd_ref[0]
        row = tile_ref[i] * M_TILE + lax.broadcasted_iota(jnp.int32, (M_TILE, 1), 0)
        mine = (row >= lo) & (row < hi)

        @pl.when(first_ref[i] == 1)
        def _():
            ys_ref[...] = jnp.where(mine, y, 0.0)

        @pl.when(first_ref[i] == 0)
        def _():
            ys_ref[...] = jnp.where(mine, y, ys_ref[...])


def _experts(meta, xs, wgu, bg, bl, wd, bd, perm):
    tile, expert, lo, hi, first, fresh = meta
    rows = pl.BlockSpec((M_TILE, D), lambda i, t, e, *_: (t[i], 0))
    bspec = pl.BlockSpec((1, 1, D), lambda i, t, e, *_: (e[i], 0, 0))
    return pl.pallas_call(
        _experts_kernel,
        grid_spec=pltpu.PrefetchScalarGridSpec(
            num_scalar_prefetch=6,
            grid=(tile.shape[0],),
            in_specs=[rows,
                      pl.BlockSpec((1, D, 2 * D), lambda i, t, e, *_: (e[i], 0, 0)), bspec, bspec,
                      pl.BlockSpec((1, D, D), lambda i, t, e, *_: (e[i], 0, 0)), bspec,
                      pl.BlockSpec(perm.shape, lambda i, *_: (0, 0))],
            out_specs=rows,
            scratch_shapes=[pltpu.VMEM((D, D), BF16)] * 3,
        ),
        out_shape=jax.ShapeDtypeStruct(xs.shape, F32),
        compiler_params=_cparams("arbitrary"),
        name="moe_experts",
    )(tile, expert, lo, hi, first, fresh, xs, wgu, bg, bl, wd, bd, perm)


def _combine_kernel(dest_ref, ys_ref, h2_ref, p_ref, nf_ref, y_ref, buf_ref, sem):
    nt = h2_ref.shape[0]
    base = pl.program_id(0) * (nt * TOP_K)

    def copy(t, k):
        return pltpu.make_async_copy(ys_ref.at[pl.ds(dest_ref[base + t * TOP_K + k], 1)],
                                     buf_ref.at[k, pl.ds(t, 1)], sem)

    def start(t, carry):
        for k in range(TOP_K):
            copy(t, k).start(priority=k % 2)
        return carry

    def wait(t, carry):
        for k in range(TOP_K):
            copy(t, k).wait()
        return carry

    lax.fori_loop(0, nt, start, 0, unroll=4)
    lax.fori_loop(0, nt, wait, 0, unroll=4)
    p = p_ref[...]
    h3 = h2_ref[...]
    for k in range(TOP_K):
        h3 = h3 + p[:, k:k + 1] * buf_ref[k]
    y_ref[...] = _rms(h3, nf_ref[...])


def _combine(dest_flat, ys, h2, p, nf):
    n = h2.shape[0]
    row = pl.BlockSpec((COMBINE_TILE, D), lambda i, d: (i, 0))
    return pl.pallas_call(
        _combine_kernel,
        grid_spec=pltpu.PrefetchScalarGridSpec(
            num_scalar_prefetch=1,
            grid=(n // COMBINE_TILE,),
            in_specs=[pl.BlockSpec(memory_space=pl.ANY), row,
                      pl.BlockSpec((COMBINE_TILE, 128), lambda i, d: (i, 0)),
                      pl.BlockSpec((1, D), lambda i, d: (0, 0))],
            out_specs=row,
            scratch_shapes=[pltpu.VMEM((TOP_K, COMBINE_TILE, D), F32), pltpu.SemaphoreType.DMA],
        ),
        out_shape=jax.ShapeDtypeStruct((n, D), F32),
        compiler_params=_cparams("arbitrary"),
        name="moe_combine",
    )(dest_flat, ys, h2, p, nf)


def _rope_tables(pos):
    half = DH // 2
    inv = np.power(ROPE_BASE, -np.arange(half, dtype=np.float64) / half)
    ang = pos.astype(np.float64)[:, None] * inv[None, :]
    cos = np.concatenate([np.cos(ang), np.cos(ang)], axis=1)
    sin = np.concatenate([-np.sin(ang), np.sin(ang)], axis=1)
    return jnp.asarray(cos, F32), jnp.asarray(sin, F32)


def _retention_tables(c, reps):
    log_g = np.log1p(-np.exp2(-5.0 - np.arange(HEADS, dtype=np.float64)))
    idx = np.arange(c, dtype=np.float64)
    rel = idx[:, None] - idx[None, :]
    dmat = np.where(rel >= 0, np.exp(log_g[:, None, None] * np.maximum(rel, 0.0)), 0.0)
    big = np.zeros((HEADS, c * reps, c * reps))
    for r in range(reps):
        big[:, r * c:(r + 1) * c, r * c:(r + 1) * c] = dmat
    q_in = np.tile(np.exp(log_g[:, None] * (idx + 1.0)), (1, reps))
    k_out = np.tile(np.exp(log_g[:, None] * (c - 1.0 - idx)), (1, reps))
    g_all = np.exp(log_g * c)
    bc = lambda a: jnp.asarray(np.broadcast_to(a[..., None], a.shape + (DH,)), F32)
    return jnp.asarray(big, F32), bc(q_in), bc(k_out), bc(g_all[:, None])


def _block_tri(c, reps, strict=False):
    idx = np.arange(c * reps)
    same = (idx[:, None] // c) == (idx[None, :] // c)
    low = idx[:, None] > idx[None, :] if strict else idx[:, None] >= idx[None, :]
    return jnp.asarray(same & low, BF16), jnp.asarray(same, BF16)


def _expert_schedule(counts, n_rows):
    n_tiles = n_rows // M_TILE
    n_items = n_tiles + N_EXPERTS - 1
    ends = jnp.cumsum(counts)
    starts = ends - counts
    first_tile = starts // M_TILE
    tiles_e = jnp.where(counts > 0, (ends - 1) // M_TILE - first_tile + 1, 0)
    item_end = jnp.cumsum(tiles_e)
    item_start = item_end - tiles_e
    total = item_end[-1]
    i = jnp.arange(n_items, dtype=jnp.int32)
    live = i < total
    ic = jnp.minimum(i, total - 1)
    e = jnp.sum((item_end[None, :] <= ic[:, None]).astype(jnp.int32), axis=1)
    e = jnp.minimum(e, N_EXPERTS - 1)
    tile = (first_tile[e] + ic - item_start[e]).astype(jnp.int32)
    lo = jnp.where(live, jnp.maximum(starts[e], tile * M_TILE), 0).astype(jnp.int32)
    hi = jnp.where(live, jnp.minimum(ends[e], (tile + 1) * M_TILE), 0).astype(jnp.int32)
    shifted = lambda a: jnp.concatenate([jnp.full((1,), -1, jnp.int32), a[:-1]])
    first = (live & (tile != shifted(tile))).astype(jnp.int32)
    fresh = (live & (e != shifted(e))).astype(jnp.int32)
    return tile, e, lo, hi, first, fresh


def kernel(x_prompt, x_sample, state_hgrn, state_ret, cache_mem_k, cache_mem_v, mem_prompt, norm_mix, w_in, hgrn_lb_logits, hgrn_out_norm, ret_out_norm, w_out, norm_x, norm_mem, w_xq, w_mk, w_mv, w_xo, norm_ffn, w_router, b_router, w_gate_up, b_gate_up, w_down, b_down, norm_final):
    bp, tp, _ = x_prompt.shape
    bs, ts, _ = x_sample.shape
    n_p, n_s = bp * tp, bs * ts
    past_len = 16384

    row = lambda a: a.reshape(1, -1).astype(F32)
    win = w_in[0].astype(BF16)
    wout = w_out[0].astype(BF16)
    wxq = w_xq[0].astype(BF16)
    wxo = w_xo[0].astype(BF16)
    wmk = w_mk[0].astype(BF16)
    wmv = w_mv[0].astype(BF16)
    wr = jnp.pad(w_router[0], ((0, 0), (0, 128 - N_EXPERTS))).astype(BF16)
    br = jnp.pad(b_router[0], (0, 128 - N_EXPERTS)).reshape(1, 128).astype(F32)
    bg = b_gate_up[0][:, 0::2].reshape(N_EXPERTS, 1, D)
    bl = b_gate_up[0][:, 1::2].reshape(N_EXPERTS, 1, D)
    bd = b_down[0].reshape(N_EXPERTS, 1, D)
    pw = 256
    perm_np = np.zeros((pw, pw), np.float32)
    perm_np[np.arange(pw), (np.arange(pw) % 2) * (pw // 2) + np.arange(pw) // 2] = 1.0
    perm = jnp.asarray(perm_np, BF16)
    nm, nx, nmem, nffn, nfin = row(norm_mix[0]), row(norm_x[0]), row(norm_mem[0]), row(norm_ffn[0]), row(norm_final)
    gh, gr = row(hgrn_out_norm[0]), row(ret_out_norm[0])
    lbl = hgrn_lb_logits.astype(F32)

    cos_p, sin_p = _rope_tables(np.arange(tp))
    dmat_p, qin_p, kout_p, gall_p = _retention_tables(RET_CHUNK, 1)
    tri_p, _ = _block_tri(HGRN_CHUNK, 1)
    mixed_p, sh_p, sr_p = _mixer_prompt(x_prompt, nm, win, lbl, gh, gr, cos_p, sin_p,
                                        dmat_p, qin_p, kout_p, gall_p, tri_p)
    cos_s, sin_s = _rope_tables(np.tile(past_len + np.arange(ts), S_BATCH))
    dmat_s, qin_s, kout_s, gall_s = _retention_tables(ts, S_BATCH)
    tri_s, ones_s = _block_tri(ts, S_BATCH)
    mixed_s, sh_s, sr_s = _mixer_sample(x_sample, nm, win, lbl, gh, gr, cos_s, sin_s,
                                        dmat_s, qin_s, kout_s, gall_s, tri_s, ones_s,
                                        state_hgrn[0], state_ret[0])

    mk_p, mv_p = _memkv(mem_prompt, nmem, wmk, wmv)
    h1_p, q_p = _pre_attn(x_prompt.reshape(n_p, D), mixed_p.reshape(n_p, D), wout, nx, wxq, BF16)
    h1_s, q_s = _pre_attn(x_sample.reshape(n_s, D), mixed_s, wout, nx, wxq, F32)
    o_p = _attn_prompt(q_p, mk_p, mv_p, tp)
    o_s = _attn_sample(q_s, cache_mem_k[0], cache_mem_v[0], ts)
    h2_p, hn_p, lg_p = _post_attn(h1_p, o_p, wxo, nffn, wr, br)
    h2_s, hn_s, lg_s = _post_attn(h1_s, o_s, wxo, nffn, wr, br)

    stril, _ = _block_tri(R_TILE, 1, strict=True)
    idx, p, rank, counts = _router(jnp.concatenate([lg_p, lg_s], axis=0), stril)
    counts = counts[0, :N_EXPERTS].astype(jnp.int32)
    offs = jnp.pad(jnp.cumsum(counts) - counts, (0, 128 - N_EXPERTS)).reshape(1, 128).astype(F32)
    dest = _dest(idx, rank, offs)[:, :TOP_K].reshape(-1)
    dest_p, dest_s = dest[:n_p * TOP_K], dest[n_p * TOP_K:]

    n_rows = (n_p + n_s) * TOP_K
    xs = _dispatch(dest_p, dest_s, hn_p, hn_s)
    ys = _experts(_expert_schedule(counts, n_rows), xs, w_gate_up[0], bg, bl, w_down[0], bd, perm)
    y_p = _combine(dest_p, ys, h2_p, p[:n_p], nfin)
    y_s = _combine(dest_s, ys, h2_s, p[n_p:], nfin)

    return (y_p.reshape(bp, tp, D), y_s.reshape(bs, ts, D),
            sh_p[None], sr_p[None], mk_p[None], mv_p[None], sh_s[None], sr_s[None])
```

```python
import functools

import numpy as np
import jax
import jax.numpy as jnp
from jax import lax
from jax.experimental import pallas as pl
from jax.experimental.pallas import tpu as pltpu

F32 = jnp.float32
BF16 = jnp.bfloat16

D = 1024
HEADS = 4
DH = 128
GW = HEADS * DH
IN_COLS = 8 * GW
ROPE_BASE = 10000.0
N_MEM = 256
X_HEADS = 4
X_DH = D // X_HEADS
N_EXPERTS = 32
TOP_K = 4
LIMIT = 7.0
ALPHA = 1.702
EPS = 1e-6
EXP_CLAMP = 80.0

HGRN_CHUNK = 64
RET_CHUNK = 128
T_TILE = 256
S_BATCH = 8
ROW_TILE = 512
Q_TILE = 512
A_BATCH = 8
R_TILE = 512
M_TILE = 256
DISPATCH_TILE = 256
COMBINE_TILE = 128
VMEM_LIMIT = 56 * 1024 * 1024


def _cparams(*sem):
    return pltpu.CompilerParams(dimension_semantics=sem, vmem_limit_bytes=VMEM_LIMIT)


def _dot(a, b):
    return jnp.dot(a.astype(BF16), b.astype(BF16), preferred_element_type=F32)


def _dot_nt(a, b):
    return lax.dot_general(a.astype(BF16), b.astype(BF16), (((1,), (1,)), ((), ())),
                           preferred_element_type=F32)


def _dot_tn(a, b):
    return lax.dot_general(a, b, (((0,), (0,)), ((), ())), preferred_element_type=F32)


def _rms(x, g):
    return x * lax.rsqrt(jnp.mean(x * x, axis=-1, keepdims=True) + EPS) * g


def _cumsum_rows(tri, v):
    hi = v.astype(BF16)
    lo = (v - hi.astype(F32)).astype(BF16)
    return (jnp.dot(tri, hi, preferred_element_type=F32)
            + jnp.dot(tri, lo, preferred_element_type=F32))


def _lower_bound(lbl_ref):
    l = lbl_ref[...]
    m = jnp.max(l, axis=0, keepdims=True)
    e = jnp.exp(l - m)
    return e[0:1, :] / jnp.sum(e, axis=0, keepdims=True)


def _row_to_col(row):
    n = row.shape[1]
    eye = lax.broadcasted_iota(jnp.int32, (n, n), 0) == lax.broadcasted_iota(jnp.int32, (n, n), 1)
    return jnp.sum(jnp.where(eye, jnp.broadcast_to(row, (n, n)), 0.0), axis=1, keepdims=True)


def _rotary(x, cs, sn):
    return x * cs + pltpu.roll(x, DH // 2, 1) * sn


def _hgrn_out(o, gh, gate):
    on = o * lax.rsqrt(jnp.mean(o * o, axis=-1, keepdims=True) + EPS) * gh
    return on * jax.nn.sigmoid(gate)


def _ret_out(o, gr, gate):
    mu = jnp.mean(o, axis=-1, keepdims=True)
    xc = o - mu
    var = jnp.mean(xc * xc, axis=-1, keepdims=True)
    return xc * lax.rsqrt(var + EPS) * gr * (gate * jax.nn.sigmoid(gate))


def _mixer_prompt_kernel(x_ref, nm_ref, win_ref, lbl_ref, gh_ref, gr_ref, cos_ref, sin_ref,
                         dmat_ref, qin_ref, kout_ref, gall_ref, tri_ref,
                         mixed_ref, sh_ref, sr_ref, proj_ref):
    @pl.when(pl.program_id(1) == 0)
    def _():
        sh_ref[...] = jnp.zeros_like(sh_ref)
        sr_ref[...] = jnp.zeros_like(sr_ref)

    xn = _rms(x_ref[0], nm_ref[...]).astype(BF16)
    proj_ref[...] = jnp.dot(xn, win_ref[...], preferred_element_type=F32)
    lb = _lower_bound(lbl_ref)
    gh = gh_ref[...]
    gr = gr_ref[...]
    c = HGRN_CHUNK
    causal = (lax.broadcasted_iota(jnp.int32, (c, c), 0) >= lax.broadcasted_iota(jnp.int32, (c, c), 1))

    def hgrn_step(ci, carry):
        rows = pl.ds(pl.multiple_of(ci * c, c), c)
        hq = proj_ref[rows, 0:GW] * (DH ** -0.5)
        f = lb + (1.0 - lb) * jax.nn.sigmoid(proj_ref[rows, GW:2 * GW])
        kk = 1.0 - f
        lc = _cumsum_rows(tri_ref[...], jnp.log(f))
        mid = lc[c // 2 - 1:c // 2, :]
        last = lc[c - 1:c, :]
        qd = hq * jnp.exp(jnp.minimum(lc - mid, EXP_CLAMP))
        kd = kk * jnp.exp(jnp.minimum(mid - lc, EXP_CLAMP))
        qs = hq * jnp.exp(lc)
        ks = kk * jnp.exp(last - lc)
        dl = jnp.exp(last)
        for h in range(HEADS):
            sl = slice(h * DH, (h + 1) * DH)
            v = proj_ref[rows, 2 * GW + h * DH:2 * GW + (h + 1) * DH].astype(BF16)
            a = jnp.where(causal, _dot_nt(qd[:, sl], kd[:, sl]), 0.0)
            s_old = sh_ref[0, h]
            o = _dot(a, v) + _dot(qs[:, sl], s_old)
            sh_ref[0, h] = _row_to_col(dl[:, sl]) * s_old + _dot_tn(ks[:, sl].astype(BF16), v)
            gate = proj_ref[rows, 3 * GW + h * DH:3 * GW + (h + 1) * DH]
            mixed_ref[0, rows, sl] = _hgrn_out(o, gh, gate).astype(BF16)
        return carry

    lax.fori_loop(0, T_TILE // c, hgrn_step, 0, unroll=True)

    cr = RET_CHUNK

    def ret_step(ci, carry):
        rows = pl.ds(pl.multiple_of(ci * cr, cr), cr)
        cs = cos_ref[rows, :]
        sn = sin_ref[rows, :]
        for h in range(HEADS):
            base = 4 * GW + h * DH
            q = _rotary(proj_ref[rows, base:base + DH], cs, sn)
            k = _rotary(proj_ref[rows, base + GW:base + GW + DH], cs, sn) * (DH ** -0.5)
            v = proj_ref[rows, base + 2 * GW:base + 2 * GW + DH].astype(BF16)
            sc = _dot_nt(q, k) * dmat_ref[h]
            s_old = sr_ref[0, h]
            o = _dot(sc, v) + _dot(q * qin_ref[h], s_old)
            sr_ref[0, h] = gall_ref[h] * s_old + _dot_tn((k * kout_ref[h]).astype(BF16), v)
            gate = proj_ref[rows, base + 3 * GW:base + 3 * GW + DH]
            mixed_ref[0, rows, GW + h * DH:GW + (h + 1) * DH] = _ret_out(o, gr, gate).astype(BF16)
        return carry

    lax.fori_loop(0, T_TILE // cr, ret_step, 0, unroll=True)


def _mixer_prompt(x, nm, win, lbl, gh, gr, cos, sin, dmat, qin, kout, gall, tri):
    b, t, _ = x.shape
    full = lambda shape: pl.BlockSpec(shape, lambda i, j: (0,) * len(shape))
    state = pl.BlockSpec((1, HEADS, DH, DH), lambda i, j: (i, 0, 0, 0))
    return pl.pallas_call(
        _mixer_prompt_kernel,
        grid=(b, t // T_TILE),
        in_specs=[
            pl.BlockSpec((1, T_TILE, D), lambda i, j: (i, j, 0)),
            full((1, D)), full((D, IN_COLS)), full(lbl.shape), full((1, DH)), full((1, DH)),
            pl.BlockSpec((T_TILE, DH), lambda i, j: (j, 0)),
            pl.BlockSpec((T_TILE, DH), lambda i, j: (j, 0)),
            full(dmat.shape), full(qin.shape), full(kout.shape), full(gall.shape), full(tri.shape),
        ],
        out_specs=[pl.BlockSpec((1, T_TILE, D), lambda i, j: (i, j, 0)), state, state],
        out_shape=[jax.ShapeDtypeStruct((b, t, D), BF16),
                   jax.ShapeDtypeStruct((b, HEADS, DH, DH), F32),
                   jax.ShapeDtypeStruct((b, HEADS, DH, DH), F32)],
        scratch_shapes=[pltpu.VMEM((T_TILE, IN_COLS), F32)],
        compiler_params=_cparams("parallel", "arbitrary"),
        name="mixer_prompt",
    )(x, nm, win, lbl, gh, gr, cos, sin, dmat, qin, kout, gall, tri)


def _mixer_sample_kernel(x_ref, nm_ref, win_ref, lbl_ref, gh_ref, gr_ref, cos_ref, sin_ref,
                         dmat_ref, qin_ref, kout_ref, gall_ref, tri_ref, ones_ref,
                         shin_ref, srin_ref, mixed_ref, sh_ref, sr_ref, proj_ref):
    nb, ts, _ = x_ref.shape
    rows_n = nb * ts
    xn = _rms(x_ref[...].reshape(rows_n, D), nm_ref[...]).astype(BF16)
    proj_ref[...] = jnp.dot(xn, win_ref[...], preferred_element_type=F32)
    lb = _lower_bound(lbl_ref)
    gh = gh_ref[...]
    gr = gr_ref[...]
    causal = tri_ref[...] > 0

    hq = proj_ref[:, 0:GW] * (DH ** -0.5)
    f = lb + (1.0 - lb) * jax.nn.sigmoid(proj_ref[:, GW:2 * GW])
    kk = 1.0 - f
    lf = jnp.log(f)
    lc = _cumsum_rows(tri_ref[...], lf)
    last = _cumsum_rows(ones_ref[...], lf)
    qd = hq * jnp.exp(lc)
    kd = kk * jnp.exp(jnp.minimum(-lc, EXP_CLAMP))
    ks = kk * jnp.exp(last - lc)
    dl = jnp.exp(last)
    for h in range(HEADS):
        sl = slice(h * DH, (h + 1) * DH)
        v = proj_ref[:, 2 * GW + h * DH:2 * GW + (h + 1) * DH]
        a = jnp.where(causal, _dot_nt(qd[:, sl], kd[:, sl]), 0.0)
        o_intra = _dot(a, v)
        gate = proj_ref[:, 3 * GW + h * DH:3 * GW + (h + 1) * DH]
        for b in range(nb):
            r = slice(b * ts, (b + 1) * ts)
            s_old = shin_ref[b, h]
            o = o_intra[r] + _dot(qd[r, sl], s_old)
            sh_ref[b, h] = (_row_to_col(dl[b * ts:b * ts + 1, sl]) * s_old
                            + _dot_tn(ks[r, sl], v[r]))
            mixed_ref[r, sl] = _hgrn_out(o, gh, gate[r])

    cs = cos_ref[...]
    sn = sin_ref[...]
    for h in range(HEADS):
        base = 4 * GW + h * DH
        q = _rotary(proj_ref[:, base:base + DH], cs, sn)
        k = _rotary(proj_ref[:, base + GW:base + GW + DH], cs, sn) * (DH ** -0.5)
        v = proj_ref[:, base + 2 * GW:base + 2 * GW + DH]
        sc = _dot_nt(q, k) * dmat_ref[h]
        o_intra = _dot(sc, v)
        qi = q * qin_ref[h]
        ko = k * kout_ref[h]
        gate = proj_ref[:, base + 3 * GW:base + 3 * GW + DH]
        for b in range(nb):
            r = slice(b * ts, (b + 1) * ts)
            s_old = srin_ref[b, h]
            o = o_intra[r] + _dot(qi[r], s_old)
            sr_ref[b, h] = gall_ref[h] * s_old + _dot_tn(ko[r], v[r])
            mixed_ref[r, GW + h * DH:GW + (h + 1) * DH] = _ret_out(o, gr, gate[r])


def _mixer_sample(x, nm, win, lbl, gh, gr, cos, sin, dmat, qin, kout, gall, tri, ones, sh, sr):
    b, t, _ = x.shape
    nb = S_BATCH
    full = lambda shape: pl.BlockSpec(shape, lambda i: (0,) * len(shape))
    state = pl.BlockSpec((nb, HEADS, DH, DH), lambda i: (i, 0, 0, 0))
    return pl.pallas_call(
        _mixer_sample_kernel,
        grid=(b // nb,),
        in_specs=[
            pl.BlockSpec((nb, t, D), lambda i: (i, 0, 0)),
            full((1, D)), full((D, IN_COLS)), full(lbl.shape), full((1, DH)), full((1, DH)),
            full(cos.shape), full(sin.shape),
            full(dmat.shape), full(qin.shape), full(kout.shape), full(gall.shape),
            full(tri.shape), full(ones.shape), state, state,
        ],
        out_specs=[pl.BlockSpec((nb * t, D), lambda i: (i, 0)), state, state],
        out_shape=[jax.ShapeDtypeStruct((b * t, D), F32),
                   jax.ShapeDtypeStruct(sh.shape, F32),
                   jax.ShapeDtypeStruct(sr.shape, F32)],
        scratch_shapes=[pltpu.VMEM((nb * t, IN_COLS), F32)],
        compiler_params=_cparams("parallel"),
        name="mixer_sample",
    )(x, nm, win, lbl, gh, gr, cos, sin, dmat, qin, kout, gall, tri, ones, sh, sr)


def _memkv_kernel(mem_ref, g_ref, wk_ref, wv_ref, k_ref, v_ref):
    mn = _rms(mem_ref[0], g_ref[...]).astype(BF16)
    k = jnp.dot(mn, wk_ref[...], preferred_element_type=F32)
    v = jnp.dot(mn, wv_ref[...], preferred_element_type=F32)
    for h in range(X_HEADS):
        k_ref[0, h] = k[:, h * X_DH:(h + 1) * X_DH]
        v_ref[0, h] = v[:, h * X_DH:(h + 1) * X_DH]


def _memkv(mem, g, wk, wv):
    b = mem.shape[0]
    full = lambda shape: pl.BlockSpec(shape, lambda i: (0,) * len(shape))
    kv = pl.BlockSpec((1, X_HEADS, N_MEM, X_DH), lambda i: (i, 0, 0, 0))
    shape = jax.ShapeDtypeStruct((b, X_HEADS, N_MEM, X_DH), F32)
    return pl.pallas_call(
        _memkv_kernel,
        grid=(b,),
        in_specs=[pl.BlockSpec((1, N_MEM, D), lambda i: (i, 0, 0)), full((1, D)), full((D, D)), full((D, D))],
        out_specs=[kv, kv],
        out_shape=[shape, shape],
        compiler_params=_cparams("parallel"),
        name="mem_kv",
    )(mem, g, wk, wv)


def _pre_attn_kernel(x_ref, mixed_ref, wout_ref, nx_ref, wxq_ref, h1_ref, q_ref):
    h1 = x_ref[...] + _dot(mixed_ref[...], wout_ref[...])
    h1_ref[...] = h1
    hn = _rms(h1, nx_ref[...])
    q_ref[...] = _dot(hn, wxq_ref[...]).astype(q_ref.dtype)


def _pre_attn(x, mixed, wout, nx, wxq, q_dtype):
    n = x.shape[0]
    row = pl.BlockSpec((ROW_TILE, D), lambda i: (i, 0))
    full = lambda shape: pl.BlockSpec(shape, lambda i: (0,) * len(shape))
    return pl.pallas_call(
        _pre_attn_kernel,
        grid=(n // ROW_TILE,),
        in_specs=[row, row, full((D, D)), full((1, D)), full((D, D))],
        out_specs=[row, row],
        out_shape=[jax.ShapeDtypeStruct((n, D), F32), jax.ShapeDtypeStruct((n, D), q_dtype)],
        compiler_params=_cparams("parallel"),
        name="pre_attn",
    )(x, mixed, wout, nx, wxq)


def _post_attn_kernel(h1_ref, o_ref, wxo_ref, nf_ref, wr_ref, br_ref, h2_ref, hn_ref, lg_ref):
    h2 = h1_ref[...] + _dot(o_ref[...], wxo_ref[...])
    h2_ref[...] = h2
    hn = _rms(h2, nf_ref[...])
    hn_ref[...] = hn
    lg_ref[...] = _dot(hn, wr_ref[...]) + br_ref[...]


def _post_attn(h1, o, wxo, nf, wr, br):
    n = h1.shape[0]
    row = pl.BlockSpec((ROW_TILE, D), lambda i: (i, 0))
    full = lambda shape: pl.BlockSpec(shape, lambda i: (0,) * len(shape))
    return pl.pallas_call(
        _post_attn_kernel,
        grid=(n // ROW_TILE,),
        in_specs=[row, row, full((D, D)), full((1, D)), full((D, 128)), full((1, 128))],
        out_specs=[row, row, pl.BlockSpec((ROW_TILE, 128), lambda i: (i, 0))],
        out_shape=[jax.ShapeDtypeStruct((n, D), F32), jax.ShapeDtypeStruct((n, D), F32),
                   jax.ShapeDtypeStruct((n, 128), F32)],
        compiler_params=_cparams("parallel"),
        name="post_attn",
    )(h1, o, wxo, nf, wr, br)


def _softmax_rows(s):
    m = jnp.max(s, axis=-1, keepdims=True)
    e = jnp.exp(s - m)
    return e / jnp.sum(e, axis=-1, keepdims=True)


def _xattn_prompt_kernel(x_ref, mixed_ref, k_ref, v_ref, wout_ref, nx_ref, wxq_ref, wxo_ref,
                         nf_ref, wr_ref, br_ref, h2_ref, hn_ref, lg_ref, o_scr):
    h1 = x_ref[0] + _dot(mixed_ref[0], wout_ref[...])
    q = _dot(_rms(h1, nx_ref[...]), wxq_ref[...]).astype(BF16)
    for h in range(X_HEADS):
        sl = slice(h * X_DH, (h + 1) * X_DH)
        s = _dot_nt(q[:, sl], k_ref[0, h]) * (X_DH ** -0.5)
        o_scr[:, sl] = _dot(_softmax_rows(s), v_ref[0, h]).astype(BF16)
    h2 = h1 + _dot(o_scr[...], wxo_ref[...])
    h2_ref[...] = h2
    hn = _rms(h2, nf_ref[...])
    hn_ref[...] = hn
    lg_ref[...] = _dot(hn, wr_ref[...]) + br_ref[...]


def _xattn_prompt(x, mixed, mk, mv, wout, nx, wxq, wxo, nf, wr, br):
    b, seq, _ = x.shape
    per_seq = seq // Q_TILE
    full = lambda shape: pl.BlockSpec(shape, lambda i, j: (0,) * len(shape))
    tok = pl.BlockSpec((1, Q_TILE, D), lambda i, j: (i, j, 0))
    kv = pl.BlockSpec((1, X_HEADS, N_MEM, X_DH), lambda i, j: (i, 0, 0, 0))
    row = pl.BlockSpec((Q_TILE, D), lambda i, j: (i * per_seq + j, 0))
    n = b * seq
    return pl.pallas_call(
        _xattn_prompt_kernel,
        grid=(b, per_seq),
        in_specs=[tok, tok, kv, kv, full((D, D)), full((1, D)), full((D, D)), full((D, D)),
                  full((1, D)), full((D, 128)), full((1, 128))],
        out_specs=[row, row, pl.BlockSpec((Q_TILE, 128), lambda i, j: (i * per_seq + j, 0))],
        out_shape=[jax.ShapeDtypeStruct((n, D), F32), jax.ShapeDtypeStruct((n, D), F32),
                   jax.ShapeDtypeStruct((n, 128), F32)],
        scratch_shapes=[pltpu.VMEM((Q_TILE, D), BF16)],
        compiler_params=_cparams("parallel", "arbitrary"),
        name="xattn_prompt",
    )(x, mixed, mk, mv, wout, nx, wxq, wxo, nf, wr, br)


def _attn_sample_kernel(q_ref, k_ref, v_ref, o_ref):
    nb = k_ref.shape[0]
    ts = q_ref.shape[0] // nb
    units = [(b, h) for b in range(nb) for h in range(X_HEADS)]
    s = jnp.concatenate(
        [_dot_nt(q_ref[b * ts:(b + 1) * ts, h * X_DH:(h + 1) * X_DH], k_ref[b, h]) for b, h in units], axis=0)
    p = _softmax_rows(s * (X_DH ** -0.5))
    for i, (b, h) in enumerate(units):
        o_ref[b * ts:(b + 1) * ts, h * X_DH:(h + 1) * X_DH] = _dot(p[i * ts:(i + 1) * ts], v_ref[b, h])


def _attn_sample(q, ck, cv, ts):
    n = q.shape[0]
    nb = A_BATCH
    kv = pl.BlockSpec((nb, X_HEADS, N_MEM, X_DH), lambda i: (i, 0, 0, 0))
    qs = pl.BlockSpec((nb * ts, D), lambda i: (i, 0))
    return pl.pallas_call(
        _attn_sample_kernel,
        grid=(n // (nb * ts),),
        in_specs=[qs, kv, kv],
        out_specs=qs,
        out_shape=jax.ShapeDtypeStruct((n, D), F32),
        compiler_params=_cparams("parallel"),
        name="attn_sample",
    )(q, ck, cv)


def _router_kernel(lg_ref, stril_ref, idx_ref, p_ref, rank_ref, cnt_ref, carry_ref):
    @pl.when(pl.program_id(0) == 0)
    def _():
        carry_ref[...] = jnp.zeros_like(carry_ref)

    n = lg_ref.shape[0]
    lane = lax.broadcasted_iota(jnp.int32, (n, 128), 1)
    lane_f = lane.astype(F32)
    l = jnp.where(lane < N_EXPERTS, lg_ref[...], -jnp.inf)
    tops, idxs, hots = [], [], []
    for _ in range(TOP_K):
        m = jnp.max(l, axis=1, keepdims=True)
        idx = jnp.min(jnp.where(l == m, lane_f, 128.0), axis=1, keepdims=True)
        hot = lane_f == idx
        l = jnp.where(hot, -jnp.inf, l)
        tops.append(m)
        idxs.append(idx)
        hots.append(hot)
    sel = jnp.where(hots[0] | hots[1] | hots[2] | hots[3], 1.0, 0.0)
    before = jnp.dot(stril_ref[...], sel.astype(BF16), preferred_element_type=F32) + carry_ref[...]
    carry_ref[...] += jnp.sum(sel, axis=0, keepdims=True)
    cnt_ref[...] = carry_ref[...]
    es = [jnp.exp(t - tops[0]) for t in tops]
    den = es[0] + es[1] + es[2] + es[3]
    idx_out = jnp.zeros((n, 128), jnp.int32)
    p_out = jnp.zeros((n, 128), F32)
    rank_out = jnp.zeros((n, 128), jnp.int32)
    for k in range(TOP_K):
        rank = jnp.sum(jnp.where(hots[k], before, 0.0), axis=1, keepdims=True).astype(jnp.int32)
        idx_out = jnp.where(lane == k, idxs[k].astype(jnp.int32), idx_out)
        p_out = jnp.where(lane == k, es[k] / den, p_out)
        rank_out = jnp.where(lane == k, rank, rank_out)
    idx_ref[...] = idx_out
    p_ref[...] = p_out
    rank_ref[...] = rank_out


def _router(logits, stril):
    n = logits.shape[0]
    row = pl.BlockSpec((R_TILE, 128), lambda i: (i, 0))
    one = pl.BlockSpec((1, 128), lambda i: (0, 0))
    return pl.pallas_call(
        _router_kernel,
        grid=(n // R_TILE,),
        in_specs=[row, pl.BlockSpec((R_TILE, R_TILE), lambda i: (0, 0))],
        out_specs=[row, row, row, one],
        out_shape=[jax.ShapeDtypeStruct((n, 128), jnp.int32), jax.ShapeDtypeStruct((n, 128), F32),
                   jax.ShapeDtypeStruct((n, 128), jnp.int32), jax.ShapeDtypeStruct((1, 128), F32)],
        scratch_shapes=[pltpu.VMEM((1, 128), F32)],
        compiler_params=_cparams("arbitrary"),
        name="router",
    )(logits, stril)


def _dest_kernel(idx_ref, rank_ref, offs_ref, dest_ref):
    n = idx_ref.shape[0]
    lane = lax.broadcasted_iota(jnp.int32, (n, 128), 1)
    idx = idx_ref[...]
    out = jnp.zeros((n, 128), jnp.int32)
    for k in range(TOP_K):
        hot = lane == idx[:, k:k + 1]
        off = jnp.sum(jnp.where(hot, offs_ref[...], 0.0), axis=1, keepdims=True)
        out = jnp.where(lane == k, off.astype(jnp.int32), out)
    dest_ref[...] = out + rank_ref[...]


def _dest(idx, rank, offs):
    n = idx.shape[0]
    row = pl.BlockSpec((R_TILE, 128), lambda i: (i, 0))
    return pl.pallas_call(
        _dest_kernel,
        grid=(n // R_TILE,),
        in_specs=[row, row, pl.BlockSpec((1, 128), lambda i: (0, 0))],
        out_specs=row,
        out_shape=jax.ShapeDtypeStruct((n, 128), jnp.int32),
        compiler_params=_cparams("parallel"),
        name="route_dest",
    )(idx, rank, offs)


def _dispatch_kernel(p_tiles, dest_p_ref, dest_s_ref, hn_p_ref, hn_s_ref, xs_ref, sem):
    nt = DISPATCH_TILE
    i = pl.program_id(0)

    def scatter(dest_ref, hn_ref, tile):
        base = tile * (nt * TOP_K)

        def copy(t, k):
            return pltpu.make_async_copy(hn_ref.at[pl.ds(t, 1)],
                                         xs_ref.at[pl.ds(dest_ref[base + t * TOP_K + k], 1)], sem)

        def start(t, carry):
            for k in range(TOP_K):
                copy(t, k).start(priority=k % 2)
            return carry

        def wait(t, carry):
            for k in range(TOP_K):
                copy(t, k).wait()
            return carry

        for t in range(nt):
            start(t, 0)
        for t in range(nt):
            wait(t, 0)

    @pl.when(i < p_tiles)
    def _():
        scatter(dest_p_ref, hn_p_ref, i)

    @pl.when(i >= p_tiles)
    def _():
        scatter(dest_s_ref, hn_s_ref, i - p_tiles)


def _dispatch(dest_p, dest_s, hn_p, hn_s):
    p_tiles = hn_p.shape[0] // DISPATCH_TILE
    s_tiles = hn_s.shape[0] // DISPATCH_TILE
    n_rows = (hn_p.shape[0] + hn_s.shape[0]) * TOP_K
    return pl.pallas_call(
        functools.partial(_dispatch_kernel, p_tiles),
        grid_spec=pltpu.PrefetchScalarGridSpec(
            num_scalar_prefetch=2,
            grid=(p_tiles + s_tiles,),
            in_specs=[pl.BlockSpec((DISPATCH_TILE, D), lambda i, dp, ds: (jnp.minimum(i, p_tiles - 1), 0)),
                      pl.BlockSpec((DISPATCH_TILE, D), lambda i, dp, ds: (jnp.maximum(i - p_tiles, 0), 0))],
            out_specs=pl.BlockSpec(memory_space=pl.ANY),
            scratch_shapes=[pltpu.SemaphoreType.DMA],
        ),
        out_shape=jax.ShapeDtypeStruct((n_rows, D), F32),
        compiler_params=_cparams("arbitrary"),
        name="moe_dispatch",
    )(dest_p, dest_s, hn_p, hn_s)


def _experts_kernel(tile_ref, exp_ref, lo_ref, hi_ref, first_ref, fresh_ref,
                    xs_ref, wgu_ref, bg_ref, bl_ref, wd_ref, bd_ref, perm_ref, ys_ref,
                    wg_s, wl_s, wd_s):
    i = pl.program_id(0)
    lo = lo_ref[i]
    hi = hi_ref[i]

    @pl.when(fresh_ref[i] == 1)
    def _():
        pw = perm_ref.shape[0]
        for c in range(2 * D // pw):
            blk = wgu_ref[0, :, c * pw:(c + 1) * pw].astype(BF16)
            sep = jnp.dot(blk, perm_ref[...], preferred_element_type=F32)
            wg_s[:, c * pw // 2:(c + 1) * pw // 2] = sep[:, :pw // 2].astype(BF16)
            wl_s[:, c * pw // 2:(c + 1) * pw // 2] = sep[:, pw // 2:].astype(BF16)
        wd_s[...] = wd_ref[0].astype(BF16)

    @pl.when(hi > lo)
    def _():
        x = xs_ref[...].astype(BF16)
        glu = jnp.minimum(jnp.dot(x, wg_s[...], preferred_element_type=F32) + bg_ref[0], LIMIT)
        lin = jnp.clip(jnp.dot(x, wl_s[...], preferred_element_type=F32) + bl_ref[0], -LIMIT, LIMIT)
        hmid = glu * jax.nn.sigmoid(ALPHA * glu) * (lin + 1.0)
        y = _dot(hmid, wd_s[...]) + bd_ref[0]
        row = tile_ref[i] * M_TILE + lax.broadcasted_iota(jnp.int32, (M_TILE, 1), 0)
        mine = (row >= lo) & (row < hi)

        @pl.when(first_ref[i] == 1)
        def _():
            ys_ref[...] = jnp.where(mine, y, 0.0)

        @pl.when(first_ref[i] == 0)
        def _():
            ys_ref[...] = jnp.where(mine, y, ys_ref[...])


def _experts(meta, xs, wgu, bg, bl, wd, bd, perm):
    tile, expert, lo, hi, first, fresh = meta
    rows = pl.BlockSpec((M_TILE, D), lambda i, t, e, *_: (t[i], 0))
    bspec = pl.BlockSpec((1, 1, D), lambda i, t, e, *_: (e[i], 0, 0))
    return pl.pallas_call(
        _experts_kernel,
        grid_spec=pltpu.PrefetchScalarGridSpec(
            num_scalar_prefetch=6,
            grid=(tile.shape[0],),
            in_specs=[rows,
                      pl.BlockSpec((1, D, 2 * D), lambda i, t, e, *_: (e[i], 0, 0)), bspec, bspec,
                      pl.BlockSpec((1, D, D), lambda i, t, e, *_: (e[i], 0, 0)), bspec,
                      pl.BlockSpec(perm.shape, lambda i, *_: (0, 0))],
            out_specs=rows,
            scratch_shapes=[pltpu.VMEM((D, D), BF16)] * 3,
        ),
        out_shape=jax.ShapeDtypeStruct(xs.shape, F32),
        compiler_params=_cparams("arbitrary"),
        name="moe_experts",
    )(tile, expert, lo, hi, first, fresh, xs, wgu, bg, bl, wd, bd, perm)


def _combine_kernel(dest_ref, ys_ref, h2_ref, p_ref, nf_ref, y_ref, buf_ref, sem):
    nt = h2_ref.shape[0]
    base = pl.program_id(0) * (nt * TOP_K)

    def copy(t, k):
        return pltpu.make_async_copy(ys_ref.at[pl.ds(dest_ref[base + t * TOP_K + k], 1)],
                                     buf_ref.at[k, pl.ds(t, 1)], sem)

    def start(t, carry):
        for k in range(TOP_K):
            copy(t, k).start(priority=k % 2)
        return carry

    def wait(t, carry):
        for k in range(TOP_K):
            copy(t, k).wait()
        return carry

    for t in range(nt):
        start(t, 0)
    for t in range(nt):
        wait(t, 0)
    p = p_ref[...]
    h3 = h2_ref[...]
    for k in range(TOP_K):
        h3 = h3 + p[:, k:k + 1] * buf_ref[k]
    y_ref[...] = _rms(h3, nf_ref[...])


def _combine(dest_flat, ys, h2, p, nf):
    n = h2.shape[0]
    row = pl.BlockSpec((COMBINE_TILE, D), lambda i, d: (i, 0))
    return pl.pallas_call(
        _combine_kernel,
        grid_spec=pltpu.PrefetchScalarGridSpec(
            num_scalar_prefetch=1,
            grid=(n // COMBINE_TILE,),
            in_specs=[pl.BlockSpec(memory_space=pl.ANY), row,
                      pl.BlockSpec((COMBINE_TILE, 128), lambda i, d: (i, 0)),
                      pl.BlockSpec((1, D), lambda i, d: (0, 0))],
            out_specs=row,
            scratch_shapes=[pltpu.VMEM((TOP_K, COMBINE_TILE, D), F32), pltpu.SemaphoreType.DMA],
        ),
        out_shape=jax.ShapeDtypeStruct((n, D), F32),
        compiler_params=_cparams("arbitrary"),
        name="moe_combine",
    )(dest_flat, ys, h2, p, nf)


def _rope_tables(pos):
    half = DH // 2
    inv = np.power(ROPE_BASE, -np.arange(half, dtype=np.float64) / half)
    ang = pos.astype(np.float64)[:, None] * inv[None, :]
    cos = np.concatenate([np.cos(ang), np.cos(ang)], axis=1)
    sin = np.concatenate([-np.sin(ang), np.sin(ang)], axis=1)
    return jnp.asarray(cos, F32), jnp.asarray(sin, F32)


def _retention_tables(c, reps):
    log_g = np.log1p(-np.exp2(-5.0 - np.arange(HEADS, dtype=np.float64)))
    idx = np.arange(c, dtype=np.float64)
    rel = idx[:, None] - idx[None, :]
    dmat = np.where(rel >= 0, np.exp(log_g[:, None, None] * np.maximum(rel, 0.0)), 0.0)
    big = np.zeros((HEADS, c * reps, c * reps))
    for r in range(reps):
        big[:, r * c:(r + 1) * c, r * c:(r + 1) * c] = dmat
    q_in = np.tile(np.exp(log_g[:, None] * (idx + 1.0)), (1, reps))
    k_out = np.tile(np.exp(log_g[:, None] * (c - 1.0 - idx)), (1, reps))
    g_all = np.exp(log_g * c)
    bc = lambda a: jnp.asarray(np.broadcast_to(a[..., None], a.shape + (DH,)), F32)
    return jnp.asarray(big, F32), bc(q_in), bc(k_out), bc(g_all[:, None])


def _block_tri(c, reps, strict=False):
    idx = np.arange(c * reps)
    same = (idx[:, None] // c) == (idx[None, :] // c)
    low = idx[:, None] > idx[None, :] if strict else idx[:, None] >= idx[None, :]
    return jnp.asarray(same & low, BF16), jnp.asarray(same, BF16)


def _expert_schedule(counts, n_rows):
    n_tiles = n_rows // M_TILE
    n_items = n_tiles + N_EXPERTS - 1
    ends = jnp.cumsum(counts)
    starts = ends - counts
    first_tile = starts // M_TILE
    tiles_e = jnp.where(counts > 0, (ends - 1) // M_TILE - first_tile + 1, 0)
    item_end = jnp.cumsum(tiles_e)
    item_start = item_end - tiles_e
    total = item_end[-1]
    i = jnp.arange(n_items, dtype=jnp.int32)
    live = i < total
    ic = jnp.minimum(i, total - 1)
    e = jnp.sum((item_end[None, :] <= ic[:, None]).astype(jnp.int32), axis=1)
    e = jnp.minimum(e, N_EXPERTS - 1)
    hot = e[:, None] == jnp.arange(N_EXPERTS, dtype=jnp.int32)[None, :]
    at_e = lambda table: jnp.sum(jnp.where(hot, table[None, :], 0), axis=1)
    tile = (at_e(first_tile) + ic - at_e(item_start)).astype(jnp.int32)
    lo = jnp.where(live, jnp.maximum(at_e(starts), tile * M_TILE), 0).astype(jnp.int32)
    hi = jnp.where(live, jnp.minimum(at_e(ends), (tile + 1) * M_TILE), 0).astype(jnp.int32)
    shifted = lambda a: jnp.concatenate([jnp.full((1,), -1, jnp.int32), a[:-1]])
    first = (live & (tile != shifted(tile))).astype(jnp.int32)
    fresh = (live & (e != shifted(e))).astype(jnp.int32)
    return tile, e, lo, hi, first, fresh


def kernel(x_prompt, x_sample, state_hgrn, state_ret, cache_mem_k, cache_mem_v, mem_prompt, norm_mix, w_in, hgrn_lb_logits, hgrn_out_norm, ret_out_norm, w_out, norm_x, norm_mem, w_xq, w_mk, w_mv, w_xo, norm_ffn, w_router, b_router, w_gate_up, b_gate_up, w_down, b_down, norm_final):
    bp, tp, _ = x_prompt.shape
    bs, ts, _ = x_sample.shape
    n_p, n_s = bp * tp, bs * ts
    past_len = 16384

    row = lambda a: a.reshape(1, -1).astype(F32)
    win = w_in[0].astype(BF16)
    wout = w_out[0].astype(BF16)
    wxq = w_xq[0].astype(BF16)
    wxo = w_xo[0].astype(BF16)
    wmk = w_mk[0].astype(BF16)
    wmv = w_mv[0].astype(BF16)
    wr = jnp.pad(w_router[0], ((0, 0), (0, 128 - N_EXPERTS))).astype(BF16)
    br = jnp.pad(b_router[0], (0, 128 - N_EXPERTS)).reshape(1, 128).astype(F32)
    bg = b_gate_up[0][:, 0::2].reshape(N_EXPERTS, 1, D)
    bl = b_gate_up[0][:, 1::2].reshape(N_EXPERTS, 1, D)
    bd = b_down[0].reshape(N_EXPERTS, 1, D)
    pw = 256
    perm_np = np.zeros((pw, pw), np.float32)
    perm_np[np.arange(pw), (np.arange(pw) % 2) * (pw // 2) + np.arange(pw) // 2] = 1.0
    perm = jnp.asarray(perm_np, BF16)
    nm, nx, nmem, nffn, nfin = row(norm_mix[0]), row(norm_x[0]), row(norm_mem[0]), row(norm_ffn[0]), row(norm_final)
    gh, gr = row(hgrn_out_norm[0]), row(ret_out_norm[0])
    lbl = hgrn_lb_logits.astype(F32)

    cos_p, sin_p = _rope_tables(np.arange(tp))
    dmat_p, qin_p, kout_p, gall_p = _retention_tables(RET_CHUNK, 1)
    tri_p, _ = _block_tri(HGRN_CHUNK, 1)
    mixed_p, sh_p, sr_p = _mixer_prompt(x_prompt, nm, win, lbl, gh, gr, cos_p, sin_p,
                                        dmat_p, qin_p, kout_p, gall_p, tri_p)
    cos_s, sin_s = _rope_tables(np.tile(past_len + np.arange(ts), S_BATCH))
    dmat_s, qin_s, kout_s, gall_s = _retention_tables(ts, S_BATCH)
    tri_s, ones_s = _block_tri(ts, S_BATCH)
    mixed_s, sh_s, sr_s = _mixer_sample(x_sample, nm, win, lbl, gh, gr, cos_s, sin_s,
                                        dmat_s, qin_s, kout_s, gall_s, tri_s, ones_s,
                                        state_hgrn[0], state_ret[0])

    mk_p, mv_p = _memkv(mem_prompt, nmem, wmk, wmv)
    h2_p, hn_p, lg_p = _xattn_prompt(x_prompt, mixed_p, mk_p, mv_p, wout, nx, wxq, wxo, nffn, wr, br)
    h1_s, q_s = _pre_attn(x_sample.reshape(n_s, D), mixed_s, wout, nx, wxq, F32)
    o_s = _attn_sample(q_s, cache_mem_k[0], cache_mem_v[0], ts)
    h2_s, hn_s, lg_s = _post_attn(h1_s, o_s, wxo, nffn, wr, br)

    stril, _ = _block_tri(R_TILE, 1, strict=True)
    idx, p, rank, counts = _router(jnp.concatenate([lg_p, lg_s], axis=0), stril)
    counts = counts[0, :N_EXPERTS].astype(jnp.int32)
    offs = jnp.pad(jnp.cumsum(counts) - counts, (0, 128 - N_EXPERTS)).reshape(1, 128).astype(F32)
    dest = _dest(idx, rank, offs)[:, :TOP_K].reshape(-1)
    dest_p, dest_s = dest[:n_p * TOP_K], dest[n_p * TOP_K:]

    n_rows = (n_p + n_s) * TOP_K
    xs = _dispatch(dest_p, dest_s, hn_p, hn_s)
    ys = _experts(_expert_schedule(counts, n_rows), xs, w_gate_up[0], bg, bl, w_down[0], bd, perm)
    y_p = _combine(dest_p, ys, h2_p, p[:n_p], nfin)
    y_s = _combine(dest_s, ys, h2_s, p[n_p:], nfin)

    return (y_p.reshape(bp, tp, D), y_s.reshape(bs, ts, D),
            sh_p[None], sr_p[None], mk_p[None], mv_p[None], sh_s[None], sr_s[None])
```

```python
import functools

import numpy as np
import jax
import jax.numpy as jnp
from jax import lax
from jax.experimental import pallas as pl
from jax.experimental.pallas import tpu as pltpu

F32 = jnp.float32
BF16 = jnp.bfloat16

D = 1024
HEADS = 4
DH = 128
GW = HEADS * DH
IN_COLS = 8 * GW
ROPE_BASE = 10000.0
N_MEM = 256
X_HEADS = 4
X_DH = D // X_HEADS
N_EXPERTS = 32
TOP_K = 4
LIMIT = 7.0
ALPHA = 1.702
EPS = 1e-6
EXP_CLAMP = 80.0

HGRN_CHUNK = 64
RET_CHUNK = 128
T_TILE = 256
S_BATCH = 8
ROW_TILE = 512
Q_TILE = 512
A_BATCH = 8
R_TILE = 512
M_TILE = 256
DISPATCH_TILE = 256
COMBINE_TILE = 128
VMEM_LIMIT = 56 * 1024 * 1024


def _cparams(*sem):
    return pltpu.CompilerParams(dimension_semantics=sem, vmem_limit_bytes=VMEM_LIMIT)


def _dot(a, b):
    return jnp.dot(a.astype(BF16), b.astype(BF16), preferred_element_type=F32)


def _dot_nt(a, b):
    return lax.dot_general(a.astype(BF16), b.astype(BF16), (((1,), (1,)), ((), ())),
                           preferred_element_type=F32)


def _dot_tn(a, b):
    return lax.dot_general(a, b, (((0,), (0,)), ((), ())), preferred_element_type=F32)


def _rms(x, g):
    return x * lax.rsqrt(jnp.mean(x * x, axis=-1, keepdims=True) + EPS) * g


ROW_CHUNKS = D // 128


def _load_row_tiles(ref, n):
    return jnp.concatenate([ref[pl.ds(c, n, stride=ROW_CHUNKS), :] for c in range(ROW_CHUNKS)], axis=1)


def _store_row_tiles(ref, x):
    n = x.shape[0]
    for c in range(ROW_CHUNKS):
        ref[pl.ds(c, n, stride=ROW_CHUNKS), :] = x[:, c * 128:(c + 1) * 128]


def _cumsum_rows(tri, v):
    hi = v.astype(BF16)
    lo = (v - hi.astype(F32)).astype(BF16)
    return (jnp.dot(tri, hi, preferred_element_type=F32)
            + jnp.dot(tri, lo, preferred_element_type=F32))


def _lower_bound(lbl_ref):
    l = lbl_ref[...]
    m = jnp.max(l, axis=0, keepdims=True)
    e = jnp.exp(l - m)
    return e[0:1, :] / jnp.sum(e, axis=0, keepdims=True)


def _row_to_col(row):
    n = row.shape[1]
    eye = lax.broadcasted_iota(jnp.int32, (n, n), 0) == lax.broadcasted_iota(jnp.int32, (n, n), 1)
    return jnp.sum(jnp.where(eye, jnp.broadcast_to(row, (n, n)), 0.0), axis=1, keepdims=True)


def _rotary(x, cs, sn):
    return x * cs + pltpu.roll(x, DH // 2, 1) * sn


def _hgrn_out(o, gh, gate):
    on = o * lax.rsqrt(jnp.mean(o * o, axis=-1, keepdims=True) + EPS) * gh
    return on * jax.nn.sigmoid(gate)


def _ret_out(o, gr, gate):
    mu = jnp.mean(o, axis=-1, keepdims=True)
    xc = o - mu
    var = jnp.mean(xc * xc, axis=-1, keepdims=True)
    return xc * lax.rsqrt(var + EPS) * gr * (gate * jax.nn.sigmoid(gate))


def _mixer_prompt_kernel(x_ref, nm_ref, win_ref, lbl_ref, gh_ref, gr_ref, cos_ref, sin_ref,
                         dmat_ref, qin_ref, kout_ref, gall_ref, tri_ref,
                         mixed_ref, sh_ref, sr_ref, proj_ref):
    @pl.when(pl.program_id(1) == 0)
    def _():
        sh_ref[...] = jnp.zeros_like(sh_ref)
        sr_ref[...] = jnp.zeros_like(sr_ref)

    xn = _rms(x_ref[0], nm_ref[...]).astype(BF16)
    proj_ref[...] = jnp.dot(xn, win_ref[...], preferred_element_type=F32)
    lb = _lower_bound(lbl_ref)
    gh = gh_ref[...]
    gr = gr_ref[...]
    c = HGRN_CHUNK
    causal = (lax.broadcasted_iota(jnp.int32, (c, c), 0) >= lax.broadcasted_iota(jnp.int32, (c, c), 1))

    def hgrn_step(ci, carry):
        rows = pl.ds(pl.multiple_of(ci * c, c), c)
        hq = proj_ref[rows, 0:GW] * (DH ** -0.5)
        f = lb + (1.0 - lb) * jax.nn.sigmoid(proj_ref[rows, GW:2 * GW])
        kk = 1.0 - f
        lc = _cumsum_rows(tri_ref[...], jnp.log(f))
        mid = lc[c // 2 - 1:c // 2, :]
        last = lc[c - 1:c, :]
        qd = hq * jnp.exp(jnp.minimum(lc - mid, EXP_CLAMP))
        kd = kk * jnp.exp(jnp.minimum(mid - lc, EXP_CLAMP))
        qs = hq * jnp.exp(lc)
        ks = kk * jnp.exp(last - lc)
        dl = jnp.exp(last)
        for h in range(HEADS):
            sl = slice(h * DH, (h + 1) * DH)
            v = proj_ref[rows, 2 * GW + h * DH:2 * GW + (h + 1) * DH].astype(BF16)
            a = jnp.where(causal, _dot_nt(qd[:, sl], kd[:, sl]), 0.0)
            s_old = sh_ref[0, h]
            o = _dot(a, v) + _dot(qs[:, sl], s_old)
            sh_ref[0, h] = _row_to_col(dl[:, sl]) * s_old + _dot_tn(ks[:, sl].astype(BF16), v)
            gate = proj_ref[rows, 3 * GW + h * DH:3 * GW + (h + 1) * DH]
            mixed_ref[0, rows, sl] = _hgrn_out(o, gh, gate).astype(BF16)
        return carry

    lax.fori_loop(0, T_TILE // c, hgrn_step, 0, unroll=True)

    cr = RET_CHUNK

    def ret_step(ci, carry):
        rows = pl.ds(pl.multiple_of(ci * cr, cr), cr)
        cs = cos_ref[rows, :]
        sn = sin_ref[rows, :]
        for h in range(HEADS):
            base = 4 * GW + h * DH
            q = _rotary(proj_ref[rows, base:base + DH], cs, sn)
            k = _rotary(proj_ref[rows, base + GW:base + GW + DH], cs, sn) * (DH ** -0.5)
            v = proj_ref[rows, base + 2 * GW:base + 2 * GW + DH].astype(BF16)
            sc = _dot_nt(q, k) * dmat_ref[h]
            s_old = sr_ref[0, h]
            o = _dot(sc, v) + _dot(q * qin_ref[h], s_old)
            sr_ref[0, h] = gall_ref[h] * s_old + _dot_tn((k * kout_ref[h]).astype(BF16), v)
            gate = proj_ref[rows, base + 3 * GW:base + 3 * GW + DH]
            mixed_ref[0, rows, GW + h * DH:GW + (h + 1) * DH] = _ret_out(o, gr, gate).astype(BF16)
        return carry

    lax.fori_loop(0, T_TILE // cr, ret_step, 0, unroll=True)


def _mixer_prompt(x, nm, win, lbl, gh, gr, cos, sin, dmat, qin, kout, gall, tri):
    b, t, _ = x.shape
    full = lambda shape: pl.BlockSpec(shape, lambda i, j: (0,) * len(shape))
    state = pl.BlockSpec((1, HEADS, DH, DH), lambda i, j: (i, 0, 0, 0))
    return pl.pallas_call(
        _mixer_prompt_kernel,
        grid=(b, t // T_TILE),
        in_specs=[
            pl.BlockSpec((1, T_TILE, D), lambda i, j: (i, j, 0)),
            full((1, D)), full((D, IN_COLS)), full(lbl.shape), full((1, DH)), full((1, DH)),
            pl.BlockSpec((T_TILE, DH), lambda i, j: (j, 0)),
            pl.BlockSpec((T_TILE, DH), lambda i, j: (j, 0)),
            full(dmat.shape), full(qin.shape), full(kout.shape), full(gall.shape), full(tri.shape),
        ],
        out_specs=[pl.BlockSpec((1, T_TILE, D), lambda i, j: (i, j, 0)), state, state],
        out_shape=[jax.ShapeDtypeStruct((b, t, D), BF16),
                   jax.ShapeDtypeStruct((b, HEADS, DH, DH), F32),
                   jax.ShapeDtypeStruct((b, HEADS, DH, DH), F32)],
        scratch_shapes=[pltpu.VMEM((T_TILE, IN_COLS), F32)],
        compiler_params=_cparams("parallel", "arbitrary"),
        name="mixer_prompt",
    )(x, nm, win, lbl, gh, gr, cos, sin, dmat, qin, kout, gall, tri)


def _mixer_sample_kernel(x_ref, nm_ref, win_ref, lbl_ref, gh_ref, gr_ref, cos_ref, sin_ref,
                         dmat_ref, qin_ref, kout_ref, gall_ref, tri_ref, ones_ref,
                         shin_ref, srin_ref, mixed_ref, sh_ref, sr_ref, proj_ref):
    nb, ts, _ = x_ref.shape
    rows_n = nb * ts
    xn = _rms(x_ref[...].reshape(rows_n, D), nm_ref[...]).astype(BF16)
    proj_ref[...] = jnp.dot(xn, win_ref[...], preferred_element_type=F32)
    lb = _lower_bound(lbl_ref)
    gh = gh_ref[...]
    gr = gr_ref[...]
    causal = tri_ref[...] > 0

    hq = proj_ref[:, 0:GW] * (DH ** -0.5)
    f = lb + (1.0 - lb) * jax.nn.sigmoid(proj_ref[:, GW:2 * GW])
    kk = 1.0 - f
    lf = jnp.log(f)
    lc = _cumsum_rows(tri_ref[...], lf)
    last = _cumsum_rows(ones_ref[...], lf)
    qd = hq * jnp.exp(lc)
    kd = kk * jnp.exp(jnp.minimum(-lc, EXP_CLAMP))
    ks = kk * jnp.exp(last - lc)
    dl = jnp.exp(last)
    for h in range(HEADS):
        sl = slice(h * DH, (h + 1) * DH)
        v = proj_ref[:, 2 * GW + h * DH:2 * GW + (h + 1) * DH]
        a = jnp.where(causal, _dot_nt(qd[:, sl], kd[:, sl]), 0.0)
        o_intra = _dot(a, v)
        gate = proj_ref[:, 3 * GW + h * DH:3 * GW + (h + 1) * DH]
        for b in range(nb):
            r = slice(b * ts, (b + 1) * ts)
            s_old = shin_ref[b, h]
            o = o_intra[r] + _dot(qd[r, sl], s_old)
            sh_ref[b, h] = (_row_to_col(dl[b * ts:b * ts + 1, sl]) * s_old
                            + _dot_tn(ks[r, sl], v[r]))
            mixed_ref[r, sl] = _hgrn_out(o, gh, gate[r])

    cs = cos_ref[...]
    sn = sin_ref[...]
    for h in range(HEADS):
        base = 4 * GW + h * DH
        q = _rotary(proj_ref[:, base:base + DH], cs, sn)
        k = _rotary(proj_ref[:, base + GW:base + GW + DH], cs, sn) * (DH ** -0.5)
        v = proj_ref[:, base + 2 * GW:base + 2 * GW + DH]
        sc = _dot_nt(q, k) * dmat_ref[h]
        o_intra = _dot(sc, v)
        qi = q * qin_ref[h]
        ko = k * kout_ref[h]
        gate = proj_ref[:, base + 3 * GW:base + 3 * GW + DH]
        for b in range(nb):
            r = slice(b * ts, (b + 1) * ts)
            s_old = srin_ref[b, h]
            o = o_intra[r] + _dot(qi[r], s_old)
            sr_ref[b, h] = gall_ref[h] * s_old + _dot_tn(ko[r], v[r])
            mixed_ref[r, GW + h * DH:GW + (h + 1) * DH] = _ret_out(o, gr, gate[r])


def _mixer_sample(x, nm, win, lbl, gh, gr, cos, sin, dmat, qin, kout, gall, tri, ones, sh, sr):
    b, t, _ = x.shape
    nb = S_BATCH
    full = lambda shape: pl.BlockSpec(shape, lambda i: (0,) * len(shape))
    state = pl.BlockSpec((nb, HEADS, DH, DH), lambda i: (i, 0, 0, 0))
    return pl.pallas_call(
        _mixer_sample_kernel,
        grid=(b // nb,),
        in_specs=[
            pl.BlockSpec((nb, t, D), lambda i: (i, 0, 0)),
            full((1, D)), full((D, IN_COLS)), full(lbl.shape), full((1, DH)), full((1, DH)),
            full(cos.shape), full(sin.shape),
            full(dmat.shape), full(qin.shape), full(kout.shape), full(gall.shape),
            full(tri.shape), full(ones.shape), state, state,
        ],
        out_specs=[pl.BlockSpec((nb * t, D), lambda i: (i, 0)), state, state],
        out_shape=[jax.ShapeDtypeStruct((b * t, D), F32),
                   jax.ShapeDtypeStruct(sh.shape, F32),
                   jax.ShapeDtypeStruct(sr.shape, F32)],
        scratch_shapes=[pltpu.VMEM((nb * t, IN_COLS), F32)],
        compiler_params=_cparams("parallel"),
        name="mixer_sample",
    )(x, nm, win, lbl, gh, gr, cos, sin, dmat, qin, kout, gall, tri, ones, sh, sr)


def _memkv_kernel(mem_ref, g_ref, wk_ref, wv_ref, k_ref, v_ref):
    mn = _rms(mem_ref[0], g_ref[...]).astype(BF16)
    k = jnp.dot(mn, wk_ref[...], preferred_element_type=F32)
    v = jnp.dot(mn, wv_ref[...], preferred_element_type=F32)
    for h in range(X_HEADS):
        k_ref[0, h] = k[:, h * X_DH:(h + 1) * X_DH]
        v_ref[0, h] = v[:, h * X_DH:(h + 1) * X_DH]


def _memkv(mem, g, wk, wv):
    b = mem.shape[0]
    full = lambda shape: pl.BlockSpec(shape, lambda i: (0,) * len(shape))
    kv = pl.BlockSpec((1, X_HEADS, N_MEM, X_DH), lambda i: (i, 0, 0, 0))
    shape = jax.ShapeDtypeStruct((b, X_HEADS, N_MEM, X_DH), F32)
    return pl.pallas_call(
        _memkv_kernel,
        grid=(b,),
        in_specs=[pl.BlockSpec((1, N_MEM, D), lambda i: (i, 0, 0)), full((1, D)), full((D, D)), full((D, D))],
        out_specs=[kv, kv],
        out_shape=[shape, shape],
        compiler_params=_cparams("parallel"),
        name="mem_kv",
    )(mem, g, wk, wv)


def _pre_attn_kernel(x_ref, mixed_ref, wout_ref, nx_ref, wxq_ref, h1_ref, q_ref):
    h1 = x_ref[...] + _dot(mixed_ref[...], wout_ref[...])
    h1_ref[...] = h1
    hn = _rms(h1, nx_ref[...])
    q_ref[...] = _dot(hn, wxq_ref[...]).astype(q_ref.dtype)


def _pre_attn(x, mixed, wout, nx, wxq, q_dtype):
    n = x.shape[0]
    row = pl.BlockSpec((ROW_TILE, D), lambda i: (i, 0))
    full = lambda shape: pl.BlockSpec(shape, lambda i: (0,) * len(shape))
    return pl.pallas_call(
        _pre_attn_kernel,
        grid=(n // ROW_TILE,),
        in_specs=[row, row, full((D, D)), full((1, D)), full((D, D))],
        out_specs=[row, row],
        out_shape=[jax.ShapeDtypeStruct((n, D), F32), jax.ShapeDtypeStruct((n, D), q_dtype)],
        compiler_params=_cparams("parallel"),
        name="pre_attn",
    )(x, mixed, wout, nx, wxq)


def _post_attn_kernel(h1_ref, o_ref, wxo_ref, nf_ref, wr_ref, br_ref, h2_ref, hn_ref, lg_ref):
    h2 = h1_ref[...] + _dot(o_ref[...], wxo_ref[...])
    h2_ref[...] = h2
    hn = _rms(h2, nf_ref[...])
    _store_row_tiles(hn_ref, hn)
    lg_ref[...] = _dot(hn, wr_ref[...]) + br_ref[...]


def _post_attn(h1, o, wxo, nf, wr, br):
    n = h1.shape[0]
    row = pl.BlockSpec((ROW_TILE, D), lambda i: (i, 0))
    full = lambda shape: pl.BlockSpec(shape, lambda i: (0,) * len(shape))
    return pl.pallas_call(
        _post_attn_kernel,
        grid=(n // ROW_TILE,),
        in_specs=[row, row, full((D, D)), full((1, D)), full((D, 128)), full((1, 128))],
        out_specs=[row, pl.BlockSpec((ROW_TILE * ROW_CHUNKS, 128), lambda i: (i, 0)),
                   pl.BlockSpec((ROW_TILE, 128), lambda i: (i, 0))],
        out_shape=[jax.ShapeDtypeStruct((n, D), F32), jax.ShapeDtypeStruct((n * ROW_CHUNKS, 128), F32),
                   jax.ShapeDtypeStruct((n, 128), F32)],
        compiler_params=_cparams("parallel"),
        name="post_attn",
    )(h1, o, wxo, nf, wr, br)


def _softmax_rows(s):
    m = jnp.max(s, axis=-1, keepdims=True)
    e = jnp.exp(s - m)
    return e / jnp.sum(e, axis=-1, keepdims=True)


def _xattn_prompt_kernel(x_ref, mixed_ref, k_ref, v_ref, wout_ref, nx_ref, wxq_ref, wxo_ref,
                         nf_ref, wr_ref, br_ref, h2_ref, hn_ref, lg_ref, o_scr):
    h1 = x_ref[0] + _dot(mixed_ref[0], wout_ref[...])
    q = _dot(_rms(h1, nx_ref[...]), wxq_ref[...]).astype(BF16)
    for h in range(X_HEADS):
        sl = slice(h * X_DH, (h + 1) * X_DH)
        s = _dot_nt(q[:, sl], k_ref[0, h]) * (X_DH ** -0.5)
        o_scr[:, sl] = _dot(_softmax_rows(s), v_ref[0, h]).astype(BF16)
    h2 = h1 + _dot(o_scr[...], wxo_ref[...])
    h2_ref[...] = h2
    hn = _rms(h2, nf_ref[...])
    _store_row_tiles(hn_ref, hn)
    lg_ref[...] = _dot(hn, wr_ref[...]) + br_ref[...]


def _xattn_prompt(x, mixed, mk, mv, wout, nx, wxq, wxo, nf, wr, br):
    b, seq, _ = x.shape
    per_seq = seq // Q_TILE
    full = lambda shape: pl.BlockSpec(shape, lambda i, j: (0,) * len(shape))
    tok = pl.BlockSpec((1, Q_TILE, D), lambda i, j: (i, j, 0))
    kv = pl.BlockSpec((1, X_HEADS, N_MEM, X_DH), lambda i, j: (i, 0, 0, 0))
    row = pl.BlockSpec((Q_TILE, D), lambda i, j: (i * per_seq + j, 0))
    n = b * seq
    return pl.pallas_call(
        _xattn_prompt_kernel,
        grid=(b, per_seq),
        in_specs=[tok, tok, kv, kv, full((D, D)), full((1, D)), full((D, D)), full((D, D)),
                  full((1, D)), full((D, 128)), full((1, 128))],
        out_specs=[row, pl.BlockSpec((Q_TILE * ROW_CHUNKS, 128), lambda i, j: (i * per_seq + j, 0)),
                   pl.BlockSpec((Q_TILE, 128), lambda i, j: (i * per_seq + j, 0))],
        out_shape=[jax.ShapeDtypeStruct((n, D), F32), jax.ShapeDtypeStruct((n * ROW_CHUNKS, 128), F32),
                   jax.ShapeDtypeStruct((n, 128), F32)],
        scratch_shapes=[pltpu.VMEM((Q_TILE, D), BF16)],
        compiler_params=_cparams("parallel", "arbitrary"),
        name="xattn_prompt",
    )(x, mixed, mk, mv, wout, nx, wxq, wxo, nf, wr, br)


def _attn_sample_kernel(q_ref, k_ref, v_ref, o_ref):
    nb = k_ref.shape[0]
    ts = q_ref.shape[0] // nb
    units = [(b, h) for b in range(nb) for h in range(X_HEADS)]
    s = jnp.concatenate(
        [_dot_nt(q_ref[b * ts:(b + 1) * ts, h * X_DH:(h + 1) * X_DH], k_ref[b, h]) for b, h in units], axis=0)
    p = _softmax_rows(s * (X_DH ** -0.5))
    for i, (b, h) in enumerate(units):
        o_ref[b * ts:(b + 1) * ts, h * X_DH:(h + 1) * X_DH] = _dot(p[i * ts:(i + 1) * ts], v_ref[b, h])


def _attn_sample(q, ck, cv, ts):
    n = q.shape[0]
    nb = A_BATCH
    kv = pl.BlockSpec((nb, X_HEADS, N_MEM, X_DH), lambda i: (i, 0, 0, 0))
    qs = pl.BlockSpec((nb * ts, D), lambda i: (i, 0))
    return pl.pallas_call(
        _attn_sample_kernel,
        grid=(n // (nb * ts),),
        in_specs=[qs, kv, kv],
        out_specs=qs,
        out_shape=jax.ShapeDtypeStruct((n, D), F32),
        compiler_params=_cparams("parallel"),
        name="attn_sample",
    )(q, ck, cv)


def _router_kernel(lg_ref, stril_ref, idx_ref, p_ref, rank_ref, cnt_ref, carry_ref):
    @pl.when(pl.program_id(0) == 0)
    def _():
        carry_ref[...] = jnp.zeros_like(carry_ref)

    n = lg_ref.shape[0]
    lane = lax.broadcasted_iota(jnp.int32, (n, 128), 1)
    lane_f = lane.astype(F32)
    l = jnp.where(lane < N_EXPERTS, lg_ref[...], -jnp.inf)
    tops, idxs, hots = [], [], []
    for _ in range(TOP_K):
        m = jnp.max(l, axis=1, keepdims=True)
        idx = jnp.min(jnp.where(l == m, lane_f, 128.0), axis=1, keepdims=True)
        hot = lane_f == idx
        l = jnp.where(hot, -jnp.inf, l)
        tops.append(m)
        idxs.append(idx)
        hots.append(hot)
    sel = jnp.where(hots[0] | hots[1] | hots[2] | hots[3], 1.0, 0.0)
    before = jnp.dot(stril_ref[...], sel.astype(BF16), preferred_element_type=F32) + carry_ref[...]
    carry_ref[...] += jnp.sum(sel, axis=0, keepdims=True)
    cnt_ref[...] = carry_ref[...]
    es = [jnp.exp(t - tops[0]) for t in tops]
    den = es[0] + es[1] + es[2] + es[3]
    idx_out = jnp.zeros((n, 128), jnp.int32)
    p_out = jnp.zeros((n, 128), F32)
    rank_out = jnp.zeros((n, 128), jnp.int32)
    for k in range(TOP_K):
        rank = jnp.sum(jnp.where(hots[k], before, 0.0), axis=1, keepdims=True).astype(jnp.int32)
        idx_out = jnp.where(lane == k, idxs[k].astype(jnp.int32), idx_out)
        p_out = jnp.where(lane == k, es[k] / den, p_out)
        rank_out = jnp.where(lane == k, rank, rank_out)
    idx_ref[...] = idx_out
    p_ref[...] = p_out
    rank_ref[...] = rank_out


def _router(logits, stril):
    n = logits.shape[0]
    row = pl.BlockSpec((R_TILE, 128), lambda i: (i, 0))
    one = pl.BlockSpec((1, 128), lambda i: (0, 0))
    return pl.pallas_call(
        _router_kernel,
        grid=(n // R_TILE,),
        in_specs=[row, pl.BlockSpec((R_TILE, R_TILE), lambda i: (0, 0))],
        out_specs=[row, row, row, one],
        out_shape=[jax.ShapeDtypeStruct((n, 128), jnp.int32), jax.ShapeDtypeStruct((n, 128), F32),
                   jax.ShapeDtypeStruct((n, 128), jnp.int32), jax.ShapeDtypeStruct((1, 128), F32)],
        scratch_shapes=[pltpu.VMEM((1, 128), F32)],
        compiler_params=_cparams("arbitrary"),
        name="router",
    )(logits, stril)


def _dest_kernel(idx_ref, rank_ref, offs_ref, dest_ref):
    n = idx_ref.shape[0]
    lane = lax.broadcasted_iota(jnp.int32, (n, 128), 1)
    idx = idx_ref[...]
    out = jnp.zeros((n, 128), jnp.int32)
    for k in range(TOP_K):
        hot = lane == idx[:, k:k + 1]
        off = jnp.sum(jnp.where(hot, offs_ref[...], 0.0), axis=1, keepdims=True)
        out = jnp.where(lane == k, off.astype(jnp.int32), out)
    dest_ref[...] = (out + rank_ref[...]) * ROW_CHUNKS


def _dest(idx, rank, offs):
    n = idx.shape[0]
    row = pl.BlockSpec((R_TILE, 128), lambda i: (i, 0))
    return pl.pallas_call(
        _dest_kernel,
        grid=(n // R_TILE,),
        in_specs=[row, row, pl.BlockSpec((1, 128), lambda i: (0, 0))],
        out_specs=row,
        out_shape=jax.ShapeDtypeStruct((n, 128), jnp.int32),
        compiler_params=_cparams("parallel"),
        name="route_dest",
    )(idx, rank, offs)


def _dispatch_kernel(p_tiles, dest_p_ref, dest_s_ref, hn_p_ref, hn_s_ref, xs_ref, sem):
    nt = DISPATCH_TILE
    i = pl.program_id(0)

    def scatter(dest_ref, hn_ref, tile):
        base = tile * (nt * TOP_K)

        def copy(t, k):
            dst = pl.multiple_of(dest_ref[base + t * TOP_K + k], ROW_CHUNKS)
            return pltpu.make_async_copy(hn_ref.at[pl.ds(t * ROW_CHUNKS, ROW_CHUNKS)],
                                         xs_ref.at[pl.ds(dst, ROW_CHUNKS)], sem)

        def start(t, carry):
            for k in range(TOP_K):
                copy(t, k).start(priority=k % 2)
            return carry

        def wait(t, carry):
            for k in range(TOP_K):
                copy(t, k).wait()
            return carry

        for t in range(nt):
            start(t, 0)
        for t in range(nt):
            wait(t, 0)

    @pl.when(i < p_tiles)
    def _():
        scatter(dest_p_ref, hn_p_ref, i)

    @pl.when(i >= p_tiles)
    def _():
        scatter(dest_s_ref, hn_s_ref, i - p_tiles)


def _dispatch(dest_p, dest_s, hn_p, hn_s):
    p_tiles = hn_p.shape[0] // (DISPATCH_TILE * ROW_CHUNKS)
    s_tiles = hn_s.shape[0] // (DISPATCH_TILE * ROW_CHUNKS)
    n_rows = (hn_p.shape[0] + hn_s.shape[0]) * TOP_K
    return pl.pallas_call(
        functools.partial(_dispatch_kernel, p_tiles),
        grid_spec=pltpu.PrefetchScalarGridSpec(
            num_scalar_prefetch=2,
            grid=(p_tiles + s_tiles,),
            in_specs=[pl.BlockSpec((DISPATCH_TILE * ROW_CHUNKS, 128),
                                   lambda i, dp, ds: (jnp.minimum(i, p_tiles - 1), 0)),
                      pl.BlockSpec((DISPATCH_TILE * ROW_CHUNKS, 128),
                                   lambda i, dp, ds: (jnp.maximum(i - p_tiles, 0), 0))],
            out_specs=pl.BlockSpec(memory_space=pl.ANY),
            scratch_shapes=[pltpu.SemaphoreType.DMA],
        ),
        out_shape=jax.ShapeDtypeStruct((n_rows, 128), F32),
        compiler_params=_cparams("arbitrary"),
        name="moe_dispatch",
    )(dest_p, dest_s, hn_p, hn_s)


def _experts_kernel(tile_ref, exp_ref, lo_ref, hi_ref, first_ref, fresh_ref,
                    xs_ref, wgu_ref, bg_ref, bl_ref, wd_ref, bd_ref, perm_ref, ys_ref,
                    wg_s, wl_s, wd_s):
    i = pl.program_id(0)
    lo = lo_ref[i]
    hi = hi_ref[i]

    @pl.when(fresh_ref[i] == 1)
    def _():
        pw = perm_ref.shape[0]
        for c in range(2 * D // pw):
            blk = wgu_ref[0, :, c * pw:(c + 1) * pw].astype(BF16)
            sep = jnp.dot(blk, perm_ref[...], preferred_element_type=F32)
            wg_s[:, c * pw // 2:(c + 1) * pw // 2] = sep[:, :pw // 2].astype(BF16)
            wl_s[:, c * pw // 2:(c + 1) * pw // 2] = sep[:, pw // 2:].astype(BF16)
        wd_s[...] = wd_ref[0].astype(BF16)

    @pl.when(hi > lo)
    def _():
        x = _load_row_tiles(xs_ref, M_TILE).astype(BF16)
        glu = jnp.minimum(jnp.dot(x, wg_s[...], preferred_element_type=F32) + bg_ref[0], LIMIT)
        lin = jnp.clip(jnp.dot(x, wl_s[...], preferred_element_type=F32) + bl_ref[0], -LIMIT, LIMIT)
        hmid = glu * jax.nn.sigmoid(ALPHA * glu) * (lin + 1.0)
        y = _dot(hmid, wd_s[...]) + bd_ref[0]
        row = tile_ref[i] * M_TILE + lax.broadcasted_iota(jnp.int32, (M_TILE, 1), 0)
        mine = (row >= lo) & (row < hi)

        @pl.when(first_ref[i] == 1)
        def _():
            _store_row_tiles(ys_ref, jnp.where(mine, y, 0.0))

        @pl.when(first_ref[i] == 0)
        def _():
            _store_row_tiles(ys_ref, jnp.where(mine, y, _load_row_tiles(ys_ref, M_TILE)))


def _experts(meta, xs, wgu, bg, bl, wd, bd, perm):
    tile, expert, lo, hi, first, fresh = meta
    rows = pl.BlockSpec((M_TILE * ROW_CHUNKS, 128), lambda i, t, e, *_: (t[i], 0))
    bspec = pl.BlockSpec((1, 1, D), lambda i, t, e, *_: (e[i], 0, 0))
    return pl.pallas_call(
        _experts_kernel,
        grid_spec=pltpu.PrefetchScalarGridSpec(
            num_scalar_prefetch=6,
            grid=(tile.shape[0],),
            in_specs=[rows,
                      pl.BlockSpec((1, D, 2 * D), lambda i, t, e, *_: (e[i], 0, 0)), bspec, bspec,
                      pl.BlockSpec((1, D, D), lambda i, t, e, *_: (e[i], 0, 0)), bspec,
                      pl.BlockSpec(perm.shape, lambda i, *_: (0, 0))],
            out_specs=rows,
            scratch_shapes=[pltpu.VMEM((D, D), BF16)] * 3,
        ),
        out_shape=jax.ShapeDtypeStruct(xs.shape, F32),
        compiler_params=_cparams("arbitrary"),
        name="moe_experts",
    )(tile, expert, lo, hi, first, fresh, xs, wgu, bg, bl, wd, bd, perm)


def _combine_kernel(dest_ref, ys_ref, h2_ref, p_ref, nf_ref, y_ref, buf_ref, sem):
    nt = h2_ref.shape[0]
    base = pl.program_id(0) * (nt * TOP_K)

    def copy(t, k):
        src = pl.multiple_of(dest_ref[base + t * TOP_K + k], ROW_CHUNKS)
        return pltpu.make_async_copy(ys_ref.at[pl.ds(src, ROW_CHUNKS)],
                                     buf_ref.at[k, pl.ds(t * ROW_CHUNKS, ROW_CHUNKS)], sem)

    def start(t, carry):
        for k in range(TOP_K):
            copy(t, k).start(priority=k % 2)
        return carry

    def wait(t, carry):
        for k in range(TOP_K):
            copy(t, k).wait()
        return carry

    for t in range(nt):
        start(t, 0)
    for t in range(nt):
        wait(t, 0)
    p = p_ref[...]
    h3 = h2_ref[...]
    for k in range(TOP_K):
        h3 = h3 + p[:, k:k + 1] * _load_row_tiles(buf_ref.at[k], nt)
    y_ref[...] = _rms(h3, nf_ref[...])


def _combine(dest_flat, ys, h2, p, nf):
    n = h2.shape[0]
    row = pl.BlockSpec((COMBINE_TILE, D), lambda i, d: (i, 0))
    return pl.pallas_call(
        _combine_kernel,
        grid_spec=pltpu.PrefetchScalarGridSpec(
            num_scalar_prefetch=1,
            grid=(n // COMBINE_TILE,),
            in_specs=[pl.BlockSpec(memory_space=pl.ANY), row,
                      pl.BlockSpec((COMBINE_TILE, 128), lambda i, d: (i, 0)),
                      pl.BlockSpec((1, D), lambda i, d: (0, 0))],
            out_specs=row,
            scratch_shapes=[pltpu.VMEM((TOP_K, COMBINE_TILE * ROW_CHUNKS, 128), F32), pltpu.SemaphoreType.DMA],
        ),
        out_shape=jax.ShapeDtypeStruct((n, D), F32),
        compiler_params=_cparams("arbitrary"),
        name="moe_combine",
    )(dest_flat, ys, h2, p, nf)


def _rope_tables(pos):
    half = DH // 2
    inv = np.power(ROPE_BASE, -np.arange(half, dtype=np.float64) / half)
    ang = pos.astype(np.float64)[:, None] * inv[None, :]
    cos = np.concatenate([np.cos(ang), np.cos(ang)], axis=1)
    sin = np.concatenate([-np.sin(ang), np.sin(ang)], axis=1)
    return jnp.asarray(cos, F32), jnp.asarray(sin, F32)


def _retention_tables(c, reps):
    log_g = np.log1p(-np.exp2(-5.0 - np.arange(HEADS, dtype=np.float64)))
    idx = np.arange(c, dtype=np.float64)
    rel = idx[:, None] - idx[None, :]
    dmat = np.where(rel >= 0, np.exp(log_g[:, None, None] * np.maximum(rel, 0.0)), 0.0)
    big = np.zeros((HEADS, c * reps, c * reps))
    for r in range(reps):
        big[:, r * c:(r + 1) * c, r * c:(r + 1) * c] = dmat
    q_in = np.tile(np.exp(log_g[:, None] * (idx + 1.0)), (1, reps))
    k_out = np.tile(np.exp(log_g[:, None] * (c - 1.0 - idx)), (1, reps))
    g_all = np.exp(log_g * c)
    bc = lambda a: jnp.asarray(np.broadcast_to(a[..., None], a.shape + (DH,)), F32)
    return jnp.asarray(big, F32), bc(q_in), bc(k_out), bc(g_all[:, None])


def _block_tri(c, reps, strict=False):
    idx = np.arange(c * reps)
    same = (idx[:, None] // c) == (idx[None, :] // c)
    low = idx[:, None] > idx[None, :] if strict else idx[:, None] >= idx[None, :]
    return jnp.asarray(same & low, BF16), jnp.asarray(same, BF16)


def _expert_schedule(counts, n_rows):
    n_tiles = n_rows // M_TILE
    n_items = n_tiles + N_EXPERTS - 1
    ends = jnp.cumsum(counts)
    starts = ends - counts
    first_tile = starts // M_TILE
    tiles_e = jnp.where(counts > 0, (ends - 1) // M_TILE - first_tile + 1, 0)
    item_end = jnp.cumsum(tiles_e)
    item_start = item_end - tiles_e
    total = item_end[-1]
    i = jnp.arange(n_items, dtype=jnp.int32)
    live = i < total
    ic = jnp.minimum(i, total - 1)
    e = jnp.sum((item_end[None, :] <= ic[:, None]).astype(jnp.int32), axis=1)
    e = jnp.minimum(e, N_EXPERTS - 1)
    hot = e[:, None] == jnp.arange(N_EXPERTS, dtype=jnp.int32)[None, :]
    at_e = lambda table: jnp.sum(jnp.where(hot, table[None, :], 0), axis=1)
    tile = (at_e(first_tile) + ic - at_e(item_start)).astype(jnp.int32)
    lo = jnp.where(live, jnp.maximum(at_e(starts), tile * M_TILE), 0).astype(jnp.int32)
    hi = jnp.where(live, jnp.minimum(at_e(ends), (tile + 1) * M_TILE), 0).astype(jnp.int32)
    shifted = lambda a: jnp.concatenate([jnp.full((1,), -1, jnp.int32), a[:-1]])
    first = (live & (tile != shifted(tile))).astype(jnp.int32)
    fresh = (live & (e != shifted(e))).astype(jnp.int32)
    return tile, e, lo, hi, first, fresh


def kernel(x_prompt, x_sample, state_hgrn, state_ret, cache_mem_k, cache_mem_v, mem_prompt, norm_mix, w_in, hgrn_lb_logits, hgrn_out_norm, ret_out_norm, w_out, norm_x, norm_mem, w_xq, w_mk, w_mv, w_xo, norm_ffn, w_router, b_router, w_gate_up, b_gate_up, w_down, b_down, norm_final):
    bp, tp, _ = x_prompt.shape
    bs, ts, _ = x_sample.shape
    n_p, n_s = bp * tp, bs * ts
    past_len = 16384

    row = lambda a: a.reshape(1, -1).astype(F32)
    win = w_in[0].astype(BF16)
    wout = w_out[0].astype(BF16)
    wxq = w_xq[0].astype(BF16)
    wxo = w_xo[0].astype(BF16)
    wmk = w_mk[0].astype(BF16)
    wmv = w_mv[0].astype(BF16)
    wr = jnp.pad(w_router[0], ((0, 0), (0, 128 - N_EXPERTS))).astype(BF16)
    br = jnp.pad(b_router[0], (0, 128 - N_EXPERTS)).reshape(1, 128).astype(F32)
    bg = b_gate_up[0][:, 0::2].reshape(N_EXPERTS, 1, D)
    bl = b_gate_up[0][:, 1::2].reshape(N_EXPERTS, 1, D)
    bd = b_down[0].reshape(N_EXPERTS, 1, D)
    pw = 256
    perm_np = np.zeros((pw, pw), np.float32)
    perm_np[np.arange(pw), (np.arange(pw) % 2) * (pw // 2) + np.arange(pw) // 2] = 1.0
    perm = jnp.asarray(perm_np, BF16)
    nm, nx, nmem, nffn, nfin = row(norm_mix[0]), row(norm_x[0]), row(norm_mem[0]), row(norm_ffn[0]), row(norm_final)
    gh, gr = row(hgrn_out_norm[0]), row(ret_out_norm[0])
    lbl = hgrn_lb_logits.astype(F32)

    cos_p, sin_p = _rope_tables(np.arange(tp))
    dmat_p, qin_p, kout_p, gall_p = _retention_tables(RET_CHUNK, 1)
    tri_p, _ = _block_tri(HGRN_CHUNK, 1)
    mixed_p, sh_p, sr_p = _mixer_prompt(x_prompt, nm, win, lbl, gh, gr, cos_p, sin_p,
                                        dmat_p, qin_p, kout_p, gall_p, tri_p)
    cos_s, sin_s = _rope_tables(np.tile(past_len + np.arange(ts), S_BATCH))
    dmat_s, qin_s, kout_s, gall_s = _retention_tables(ts, S_BATCH)
    tri_s, ones_s = _block_tri(ts, S_BATCH)
    mixed_s, sh_s, sr_s = _mixer_sample(x_sample, nm, win, lbl, gh, gr, cos_s, sin_s,
                                        dmat_s, qin_s, kout_s, gall_s, tri_s, ones_s,
                                        state_hgrn[0], state_ret[0])

    mk_p, mv_p = _memkv(mem_prompt, nmem, wmk, wmv)
    h2_p, hn_p, lg_p = _xattn_prompt(x_prompt, mixed_p, mk_p, mv_p, wout, nx, wxq, wxo, nffn, wr, br)
    h1_s, q_s = _pre_attn(x_sample.reshape(n_s, D), mixed_s, wout, nx, wxq, F32)
    o_s = _attn_sample(q_s, cache_mem_k[0], cache_mem_v[0], ts)
    h2_s, hn_s, lg_s = _post_attn(h1_s, o_s, wxo, nffn, wr, br)

    stril, _ = _block_tri(R_TILE, 1, strict=True)
    idx, p, rank, counts = _router(jnp.concatenate([lg_p, lg_s], axis=0), stril)
    counts = counts[0, :N_EXPERTS].astype(jnp.int32)
    offs = jnp.pad(jnp.cumsum(counts) - counts, (0, 128 - N_EXPERTS)).reshape(1, 128).astype(F32)
    dest = _dest(idx, rank, offs)[:, :TOP_K].reshape(-1)
    dest_p, dest_s = dest[:n_p * TOP_K], dest[n_p * TOP_K:]

    n_rows = (n_p + n_s) * TOP_K
    xs = _dispatch(dest_p, dest_s, hn_p, hn_s)
    ys = _experts(_expert_schedule(counts, n_rows), xs, w_gate_up[0], bg, bl, w_down[0], bd, perm)
    y_p = _combine(dest_p, ys, h2_p, p[:n_p], nfin)
    y_s = _combine(dest_s, ys, h2_s, p[n_p:], nfin)

    return (y_p.reshape(bp, tp, D), y_s.reshape(bs, ts, D),
            sh_p[None], sr_p[None], mk_p[None], mv_p[None], sh_s[None], sr_s[None])
```

```python
import functools

import numpy as np
import jax
import jax.numpy as jnp
from jax import lax
from jax.experimental import pallas as pl
from jax.experimental.pallas import tpu as pltpu

F32 = jnp.float32
BF16 = jnp.bfloat16

D = 1024
HEADS = 4
DH = 128
GW = HEADS * DH
IN_COLS = 8 * GW
ROPE_BASE = 10000.0
N_MEM = 256
X_HEADS = 4
X_DH = D // X_HEADS
N_EXPERTS = 32
TOP_K = 4
LIMIT = 7.0
ALPHA = 1.702
EPS = 1e-6
EXP_CLAMP = 80.0

HGRN_CHUNK = 64
RET_CHUNK = 128
T_TILE = 256
S_BATCH = 8
ROW_TILE = 512
Q_TILE = 512
A_BATCH = 8
R_TILE = 512
M_TILE = 512
DISPATCH_TILE = 256
COMBINE_TILE = 128
VMEM_LIMIT = 56 * 1024 * 1024


def _cparams(*sem):
    return pltpu.CompilerParams(dimension_semantics=sem, vmem_limit_bytes=VMEM_LIMIT)


def _dot(a, b):
    return jnp.dot(a.astype(BF16), b.astype(BF16), preferred_element_type=F32)


def _dot_nt(a, b):
    return lax.dot_general(a.astype(BF16), b.astype(BF16), (((1,), (1,)), ((), ())),
                           preferred_element_type=F32)


def _dot_tn(a, b):
    return lax.dot_general(a, b, (((0,), (0,)), ((), ())), preferred_element_type=F32)


def _rms(x, g):
    return x * lax.rsqrt(jnp.mean(x * x, axis=-1, keepdims=True) + EPS) * g


ROW_CHUNKS = D // 128


def _load_row_tiles(ref, n):
    return jnp.concatenate([ref[pl.ds(c, n, stride=ROW_CHUNKS), :] for c in range(ROW_CHUNKS)], axis=1)


def _store_row_tiles(ref, x):
    n = x.shape[0]
    for c in range(ROW_CHUNKS):
        ref[pl.ds(c, n, stride=ROW_CHUNKS), :] = x[:, c * 128:(c + 1) * 128]


def _cumsum_rows(tri, v):
    hi = v.astype(BF16)
    lo = (v - hi.astype(F32)).astype(BF16)
    return (jnp.dot(tri, hi, preferred_element_type=F32)
            + jnp.dot(tri, lo, preferred_element_type=F32))


def _lower_bound(lbl_ref):
    l = lbl_ref[...]
    m = jnp.max(l, axis=0, keepdims=True)
    e = jnp.exp(l - m)
    return e[0:1, :] / jnp.sum(e, axis=0, keepdims=True)


def _row_to_col(row):
    n = row.shape[1]
    eye = lax.broadcasted_iota(jnp.int32, (n, n), 0) == lax.broadcasted_iota(jnp.int32, (n, n), 1)
    return jnp.sum(jnp.where(eye, jnp.broadcast_to(row, (n, n)), 0.0), axis=1, keepdims=True)


def _rotary(x, cs, sn):
    return x * cs + pltpu.roll(x, DH // 2, 1) * sn


def _hgrn_out(o, gh, gate):
    on = o * lax.rsqrt(jnp.mean(o * o, axis=-1, keepdims=True) + EPS) * gh
    return on * jax.nn.sigmoid(gate)


def _ret_out(o, gr, gate):
    mu = jnp.mean(o, axis=-1, keepdims=True)
    xc = o - mu
    var = jnp.mean(xc * xc, axis=-1, keepdims=True)
    return xc * lax.rsqrt(var + EPS) * gr * (gate * jax.nn.sigmoid(gate))


def _mixer_prompt_kernel(x_ref, nm_ref, win_ref, lbl_ref, gh_ref, gr_ref, cos_ref, sin_ref,
                         dmat_ref, qin_ref, kout_ref, gall_ref, tri_ref,
                         mixed_ref, sh_ref, sr_ref, proj_ref, f_ref, lc_ref):
    @pl.when(pl.program_id(1) == 0)
    def _():
        sh_ref[...] = jnp.zeros_like(sh_ref)
        sr_ref[...] = jnp.zeros_like(sr_ref)

    xn = _rms(x_ref[0], nm_ref[...]).astype(BF16)
    proj_ref[...] = jnp.dot(xn, win_ref[...], preferred_element_type=F32)
    lb = _lower_bound(lbl_ref)
    gh = gh_ref[...]
    gr = gr_ref[...]
    c = HGRN_CHUNK
    causal = (lax.broadcasted_iota(jnp.int32, (c, c), 0) >= lax.broadcasted_iota(jnp.int32, (c, c), 1))

    def gate_step(ci, worst):
        rows = pl.ds(pl.multiple_of(ci * c, c), c)
        f = lb + (1.0 - lb) * jax.nn.sigmoid(proj_ref[rows, GW:2 * GW])
        lc = _cumsum_rows(tri_ref[...], jnp.log(f))
        f_ref[rows, :] = f
        lc_ref[rows, :] = lc
        return jnp.maximum(worst, jnp.max(jnp.abs(lc - lc[c // 2 - 1:c // 2, :])))

    worst = lax.fori_loop(0, T_TILE // c, gate_step, jnp.float32(0.0), unroll=True)

    def scores_factored(hq, kk, lc, h, r0):
        sl = slice(h * DH, (h + 1) * DH)
        mid = lc[c // 2 - 1:c // 2, sl]
        qd = hq[:, sl] * jnp.exp(lc[:, sl] - mid)
        kd = kk[:, sl] * jnp.exp(mid - lc[:, sl])
        return jnp.where(causal, _dot_nt(qd, kd), 0.0)

    def scores_termwise(hq, kk, lc, h, r0):
        sl = slice(h * DH, (h + 1) * DH)
        col_id = lax.broadcasted_iota(jnp.int32, (c, c), 1)

        sub_id = lax.broadcasted_iota(jnp.int32, (8, DH), 0)

        def row_of(ref, s):
            grp = ref[pl.ds(pl.multiple_of(r0 + lax.shift_left(lax.shift_right_logical(s, 3), 3), 8), 8), sl]
            return jnp.sum(jnp.where(sub_id == (s & 7), grp, 0.0), axis=0, keepdims=True)

        def body(s, a):
            lcs = row_of(lc_ref, s)
            ks_row = 1.0 - row_of(f_ref, s)
            w = jnp.exp(jnp.minimum(lc[:, sl] - lcs, 0.0))
            col = jnp.sum(hq[:, sl] * ks_row * w, axis=1, keepdims=True)
            return a + jnp.where(col_id == s, col, 0.0)

        return jnp.where(causal, lax.fori_loop(0, c, body, jnp.zeros((c, c), F32)), 0.0)

    def hgrn_step(scores, ci, carry):
        r0 = pl.multiple_of(ci * c, c)
        rows = pl.ds(r0, c)
        hq = proj_ref[rows, 0:GW] * (DH ** -0.5)
        kk = 1.0 - f_ref[rows, :]
        lc = lc_ref[rows, :]
        last = lc[c - 1:c, :]
        qs = hq * jnp.exp(lc)
        ks = kk * jnp.exp(last - lc)
        dl = jnp.exp(last)
        for h in range(HEADS):
            sl = slice(h * DH, (h + 1) * DH)
            v = proj_ref[rows, 2 * GW + h * DH:2 * GW + (h + 1) * DH].astype(BF16)
            a = scores(hq, kk, lc, h, r0)
            s_old = sh_ref[0, h]
            o = _dot(a, v) + _dot(qs[:, sl], s_old)
            sh_ref[0, h] = _row_to_col(dl[:, sl]) * s_old + _dot_tn(ks[:, sl].astype(BF16), v)
            gate = proj_ref[rows, 3 * GW + h * DH:3 * GW + (h + 1) * DH]
            mixed_ref[0, rows, sl] = _hgrn_out(o, gh, gate).astype(BF16)
        return carry

    @pl.when(worst <= EXP_CLAMP)
    def _():
        lax.fori_loop(0, T_TILE // c, functools.partial(hgrn_step, scores_factored), 0, unroll=True)

    @pl.when(jnp.logical_not(worst <= EXP_CLAMP))
    def _():
        lax.fori_loop(0, T_TILE // c, functools.partial(hgrn_step, scores_termwise), 0)

    cr = RET_CHUNK

    def ret_step(ci, carry):
        rows = pl.ds(pl.multiple_of(ci * cr, cr), cr)
        cs = cos_ref[rows, :]
        sn = sin_ref[rows, :]
        for h in range(HEADS):
            base = 4 * GW + h * DH
            q = _rotary(proj_ref[rows, base:base + DH], cs, sn)
            k = _rotary(proj_ref[rows, base + GW:base + GW + DH], cs, sn) * (DH ** -0.5)
            v = proj_ref[rows, base + 2 * GW:base + 2 * GW + DH].astype(BF16)
            sc = _dot_nt(q, k) * dmat_ref[h]
            s_old = sr_ref[0, h]
            o = _dot(sc, v) + _dot(q * qin_ref[h], s_old)
            sr_ref[0, h] = gall_ref[h] * s_old + _dot_tn((k * kout_ref[h]).astype(BF16), v)
            gate = proj_ref[rows, base + 3 * GW:base + 3 * GW + DH]
            mixed_ref[0, rows, GW + h * DH:GW + (h + 1) * DH] = _ret_out(o, gr, gate).astype(BF16)
        return carry

    lax.fori_loop(0, T_TILE // cr, ret_step, 0, unroll=True)


def _mixer_prompt(x, nm, win, lbl, gh, gr, cos, sin, dmat, qin, kout, gall, tri):
    b, t, _ = x.shape
    full = lambda shape: pl.BlockSpec(shape, lambda i, j: (0,) * len(shape))
    state = pl.BlockSpec((1, HEADS, DH, DH), lambda i, j: (i, 0, 0, 0))
    return pl.pallas_call(
        _mixer_prompt_kernel,
        grid=(b, t // T_TILE),
        in_specs=[
            pl.BlockSpec((1, T_TILE, D), lambda i, j: (i, j, 0)),
            full((1, D)), full((D, IN_COLS)), full(lbl.shape), full((1, DH)), full((1, DH)),
            pl.BlockSpec((T_TILE, DH), lambda i, j: (j, 0)),
            pl.BlockSpec((T_TILE, DH), lambda i, j: (j, 0)),
            full(dmat.shape), full(qin.shape), full(kout.shape), full(gall.shape), full(tri.shape),
        ],
        out_specs=[pl.BlockSpec((1, T_TILE, D), lambda i, j: (i, j, 0)), state, state],
        out_shape=[jax.ShapeDtypeStruct((b, t, D), BF16),
                   jax.ShapeDtypeStruct((b, HEADS, DH, DH), F32),
                   jax.ShapeDtypeStruct((b, HEADS, DH, DH), F32)],
        scratch_shapes=[pltpu.VMEM((T_TILE, IN_COLS), F32), pltpu.VMEM((T_TILE, GW), F32),
                        pltpu.VMEM((T_TILE, GW), F32)],
        compiler_params=_cparams("parallel", "arbitrary"),
        name="mixer_prompt",
    )(x, nm, win, lbl, gh, gr, cos, sin, dmat, qin, kout, gall, tri)


def _mixer_sample_kernel(x_ref, nm_ref, win_ref, lbl_ref, gh_ref, gr_ref, cos_ref, sin_ref,
                         dmat_ref, qin_ref, kout_ref, gall_ref, tri_ref, ones_ref,
                         shin_ref, srin_ref, mixed_ref, sh_ref, sr_ref, proj_ref):
    nb, ts, _ = x_ref.shape
    rows_n = nb * ts
    xn = _rms(x_ref[...].reshape(rows_n, D), nm_ref[...]).astype(BF16)
    proj_ref[...] = jnp.dot(xn, win_ref[...], preferred_element_type=F32)
    lb = _lower_bound(lbl_ref)
    gh = gh_ref[...]
    gr = gr_ref[...]
    hq = proj_ref[:, 0:GW] * (DH ** -0.5)
    f = lb + (1.0 - lb) * jax.nn.sigmoid(proj_ref[:, GW:2 * GW])
    kk = 1.0 - f
    lf = jnp.log(f)
    lc = _cumsum_rows(tri_ref[...], lf)
    last = _cumsum_rows(ones_ref[...], lf)
    qd = hq * jnp.exp(lc)
    ks = kk * jnp.exp(last - lc)
    dl = jnp.exp(last)
    t_pos = lax.broadcasted_iota(jnp.int32, (nb, ts, 1), 1)

    def intra_termwise(h):
        sl = slice(h * DH, (h + 1) * DH)
        q3 = hq[:, sl].reshape(nb, ts, DH)
        k3 = kk[:, sl].reshape(nb, ts, DH)
        lc3 = lc[:, sl].reshape(nb, ts, DH)
        v3 = proj_ref[:, 2 * GW + h * DH:2 * GW + (h + 1) * DH].reshape(nb, ts, DH)
        o3 = jnp.zeros((nb, ts, DH), F32)
        for s in range(ts):
            w = jnp.exp(jnp.minimum(lc3 - lc3[:, s:s + 1, :], 0.0))
            score = jnp.sum(q3 * k3[:, s:s + 1, :] * w, axis=-1, keepdims=True)
            o3 = o3 + jnp.where(t_pos >= s, score, 0.0) * v3[:, s:s + 1, :]
        return o3.reshape(rows_n, DH)

    for h in range(HEADS):
        sl = slice(h * DH, (h + 1) * DH)
        v = proj_ref[:, 2 * GW + h * DH:2 * GW + (h + 1) * DH]
        o_intra = intra_termwise(h)
        gate = proj_ref[:, 3 * GW + h * DH:3 * GW + (h + 1) * DH]
        for b in range(nb):
            r = slice(b * ts, (b + 1) * ts)
            s_old = shin_ref[b, h]
            o = o_intra[r] + _dot(qd[r, sl], s_old)
            sh_ref[b, h] = (_row_to_col(dl[b * ts:b * ts + 1, sl]) * s_old
                            + _dot_tn(ks[r, sl], v[r]))
            mixed_ref[r, sl] = _hgrn_out(o, gh, gate[r])

    cs = cos_ref[...]
    sn = sin_ref[...]
    for h in range(HEADS):
        base = 4 * GW + h * DH
        q = _rotary(proj_ref[:, base:base + DH], cs, sn)
        k = _rotary(proj_ref[:, base + GW:base + GW + DH], cs, sn) * (DH ** -0.5)
        v = proj_ref[:, base + 2 * GW:base + 2 * GW + DH]
        sc = _dot_nt(q, k) * dmat_ref[h]
        o_intra = _dot(sc, v)
        qi = q * qin_ref[h]
        ko = k * kout_ref[h]
        gate = proj_ref[:, base + 3 * GW:base + 3 * GW + DH]
        for b in range(nb):
            r = slice(b * ts, (b + 1) * ts)
            s_old = srin_ref[b, h]
            o = o_intra[r] + _dot(qi[r], s_old)
            sr_ref[b, h] = gall_ref[h] * s_old + _dot_tn(ko[r], v[r])
            mixed_ref[r, GW + h * DH:GW + (h + 1) * DH] = _ret_out(o, gr, gate[r])


def _mixer_sample(x, nm, win, lbl, gh, gr, cos, sin, dmat, qin, kout, gall, tri, ones, sh, sr):
    b, t, _ = x.shape
    nb = S_BATCH
    full = lambda shape: pl.BlockSpec(shape, lambda i: (0,) * len(shape))
    state = pl.BlockSpec((nb, HEADS, DH, DH), lambda i: (i, 0, 0, 0))
    return pl.pallas_call(
        _mixer_sample_kernel,
        grid=(b // nb,),
        in_specs=[
            pl.BlockSpec((nb, t, D), lambda i: (i, 0, 0)),
            full((1, D)), full((D, IN_COLS)), full(lbl.shape), full((1, DH)), full((1, DH)),
            full(cos.shape), full(sin.shape),
            full(dmat.shape), full(qin.shape), full(kout.shape), full(gall.shape),
            full(tri.shape), full(ones.shape), state, state,
        ],
        out_specs=[pl.BlockSpec((nb * t, D), lambda i: (i, 0)), state, state],
        out_shape=[jax.ShapeDtypeStruct((b * t, D), F32),
                   jax.ShapeDtypeStruct(sh.shape, F32),
                   jax.ShapeDtypeStruct(sr.shape, F32)],
        scratch_shapes=[pltpu.VMEM((nb * t, IN_COLS), F32)],
        compiler_params=_cparams("parallel"),
        name="mixer_sample",
    )(x, nm, win, lbl, gh, gr, cos, sin, dmat, qin, kout, gall, tri, ones, sh, sr)


def _memkv_kernel(mem_ref, g_ref, wk_ref, wv_ref, k_ref, v_ref):
    mn = _rms(mem_ref[0], g_ref[...]).astype(BF16)
    k = jnp.dot(mn, wk_ref[...], preferred_element_type=F32)
    v = jnp.dot(mn, wv_ref[...], preferred_element_type=F32)
    for h in range(X_HEADS):
        k_ref[0, h] = k[:, h * X_DH:(h + 1) * X_DH]
        v_ref[0, h] = v[:, h * X_DH:(h + 1) * X_DH]


def _memkv(mem, g, wk, wv):
    b = mem.shape[0]
    full = lambda shape: pl.BlockSpec(shape, lambda i: (0,) * len(shape))
    kv = pl.BlockSpec((1, X_HEADS, N_MEM, X_DH), lambda i: (i, 0, 0, 0))
    shape = jax.ShapeDtypeStruct((b, X_HEADS, N_MEM, X_DH), F32)
    return pl.pallas_call(
        _memkv_kernel,
        grid=(b,),
        in_specs=[pl.BlockSpec((1, N_MEM, D), lambda i: (i, 0, 0)), full((1, D)), full((D, D)), full((D, D))],
        out_specs=[kv, kv],
        out_shape=[shape, shape],
        compiler_params=_cparams("parallel"),
        name="mem_kv",
    )(mem, g, wk, wv)


def _pre_attn_kernel(x_ref, mixed_ref, wout_ref, nx_ref, wxq_ref, h1_ref, q_ref):
    h1 = x_ref[...] + _dot(mixed_ref[...], wout_ref[...])
    h1_ref[...] = h1
    hn = _rms(h1, nx_ref[...])
    q_ref[...] = _dot(hn, wxq_ref[...]).astype(q_ref.dtype)


def _pre_attn(x, mixed, wout, nx, wxq, q_dtype):
    n = x.shape[0]
    row = pl.BlockSpec((ROW_TILE, D), lambda i: (i, 0))
    full = lambda shape: pl.BlockSpec(shape, lambda i: (0,) * len(shape))
    return pl.pallas_call(
        _pre_attn_kernel,
        grid=(n // ROW_TILE,),
        in_specs=[row, row, full((D, D)), full((1, D)), full((D, D))],
        out_specs=[row, row],
        out_shape=[jax.ShapeDtypeStruct((n, D), F32), jax.ShapeDtypeStruct((n, D), q_dtype)],
        compiler_params=_cparams("parallel"),
        name="pre_attn",
    )(x, mixed, wout, nx, wxq)


def _post_attn_kernel(h1_ref, o_ref, wxo_ref, nf_ref, wr_ref, br_ref, h2_ref, hn_ref, lg_ref):
    h2 = h1_ref[...] + _dot(o_ref[...], wxo_ref[...])
    h2_ref[...] = h2
    hn = _rms(h2, nf_ref[...])
    _store_row_tiles(hn_ref, hn)
    lg_ref[...] = _dot(hn, wr_ref[...]) + br_ref[...]


def _post_attn(h1, o, wxo, nf, wr, br):
    n = h1.shape[0]
    row = pl.BlockSpec((ROW_TILE, D), lambda i: (i, 0))
    full = lambda shape: pl.BlockSpec(shape, lambda i: (0,) * len(shape))
    return pl.pallas_call(
        _post_attn_kernel,
        grid=(n // ROW_TILE,),
        in_specs=[row, row, full((D, D)), full((1, D)), full((D, 128)), full((1, 128))],
        out_specs=[row, pl.BlockSpec((ROW_TILE * ROW_CHUNKS, 128), lambda i: (i, 0)),
                   pl.BlockSpec((ROW_TILE, 128), lambda i: (i, 0))],
        out_shape=[jax.ShapeDtypeStruct((n, D), F32), jax.ShapeDtypeStruct((n * ROW_CHUNKS, 128), F32),
                   jax.ShapeDtypeStruct((n, 128), F32)],
        compiler_params=_cparams("parallel"),
        name="post_attn",
    )(h1, o, wxo, nf, wr, br)


def _softmax_rows(s):
    m = jnp.max(s, axis=-1, keepdims=True)
    e = jnp.exp(s - m)
    return e / jnp.sum(e, axis=-1, keepdims=True)


def _xattn_prompt_kernel(x_ref, mixed_ref, k_ref, v_ref, wout_ref, nx_ref, wxq_ref, wxo_ref,
                         nf_ref, wr_ref, br_ref, h2_ref, hn_ref, lg_ref, o_scr):
    h1 = x_ref[0] + _dot(mixed_ref[0], wout_ref[...])
    q = _dot(_rms(h1, nx_ref[...]), wxq_ref[...]).astype(BF16)
    for h in range(X_HEADS):
        sl = slice(h * X_DH, (h + 1) * X_DH)
        s = _dot_nt(q[:, sl], k_ref[0, h]) * (X_DH ** -0.5)
        o_scr[:, sl] = _dot(_softmax_rows(s), v_ref[0, h]).astype(BF16)
    h2 = h1 + _dot(o_scr[...], wxo_ref[...])
    h2_ref[...] = h2
    hn = _rms(h2, nf_ref[...])
    _store_row_tiles(hn_ref, hn)
    lg_ref[...] = _dot(hn, wr_ref[...]) + br_ref[...]


def _xattn_prompt(x, mixed, mk, mv, wout, nx, wxq, wxo, nf, wr, br):
    b, seq, _ = x.shape
    per_seq = seq // Q_TILE
    full = lambda shape: pl.BlockSpec(shape, lambda i, j: (0,) * len(shape))
    tok = pl.BlockSpec((1, Q_TILE, D), lambda i, j: (i, j, 0))
    kv = pl.BlockSpec((1, X_HEADS, N_MEM, X_DH), lambda i, j: (i, 0, 0, 0))
    row = pl.BlockSpec((Q_TILE, D), lambda i, j: (i * per_seq + j, 0))
    n = b * seq
    return pl.pallas_call(
        _xattn_prompt_kernel,
        grid=(b, per_seq),
        in_specs=[tok, tok, kv, kv, full((D, D)), full((1, D)), full((D, D)), full((D, D)),
                  full((1, D)), full((D, 128)), full((1, 128))],
        out_specs=[row, pl.BlockSpec((Q_TILE * ROW_CHUNKS, 128), lambda i, j: (i * per_seq + j, 0)),
                   pl.BlockSpec((Q_TILE, 128), lambda i, j: (i * per_seq + j, 0))],
        out_shape=[jax.ShapeDtypeStruct((n, D), F32), jax.ShapeDtypeStruct((n * ROW_CHUNKS, 128), F32),
                   jax.ShapeDtypeStruct((n, 128), F32)],
        scratch_shapes=[pltpu.VMEM((Q_TILE, D), BF16)],
        compiler_params=_cparams("parallel", "arbitrary"),
        name="xattn_prompt",
    )(x, mixed, mk, mv, wout, nx, wxq, wxo, nf, wr, br)


def _attn_sample_kernel(q_ref, k_ref, v_ref, o_ref):
    nb = k_ref.shape[0]
    ts = q_ref.shape[0] // nb
    units = [(b, h) for b in range(nb) for h in range(X_HEADS)]
    s = jnp.concatenate(
        [_dot_nt(q_ref[b * ts:(b + 1) * ts, h * X_DH:(h + 1) * X_DH], k_ref[b, h]) for b, h in units], axis=0)
    p = _softmax_rows(s * (X_DH ** -0.5))
    for i, (b, h) in enumerate(units):
        o_ref[b * ts:(b + 1) * ts, h * X_DH:(h + 1) * X_DH] = _dot(p[i * ts:(i + 1) * ts], v_ref[b, h])


def _attn_sample(q, ck, cv, ts):
    n = q.shape[0]
    nb = A_BATCH
    kv = pl.BlockSpec((nb, X_HEADS, N_MEM, X_DH), lambda i: (i, 0, 0, 0))
    qs = pl.BlockSpec((nb * ts, D), lambda i: (i, 0))
    return pl.pallas_call(
        _attn_sample_kernel,
        grid=(n // (nb * ts),),
        in_specs=[qs, kv, kv],
        out_specs=qs,
        out_shape=jax.ShapeDtypeStruct((n, D), F32),
        compiler_params=_cparams("parallel"),
        name="attn_sample",
    )(q, ck, cv)


def _router_kernel(lg_ref, stril_ref, idx_ref, p_ref, rank_ref, cnt_ref, carry_ref):
    @pl.when(pl.program_id(0) == 0)
    def _():
        carry_ref[...] = jnp.zeros_like(carry_ref)

    n = lg_ref.shape[0]
    lane = lax.broadcasted_iota(jnp.int32, (n, 128), 1)
    lane_f = lane.astype(F32)
    l = jnp.where(lane < N_EXPERTS, lg_ref[...], -jnp.inf)
    tops, idxs, hots = [], [], []
    for _ in range(TOP_K):
        m = jnp.max(l, axis=1, keepdims=True)
        idx = jnp.min(jnp.where(l == m, lane_f, 128.0), axis=1, keepdims=True)
        hot = lane_f == idx
        l = jnp.where(hot, -jnp.inf, l)
        tops.append(m)
        idxs.append(idx)
        hots.append(hot)
    sel = jnp.where(hots[0] | hots[1] | hots[2] | hots[3], 1.0, 0.0)
    before = jnp.dot(stril_ref[...], sel.astype(BF16), preferred_element_type=F32) + carry_ref[...]
    carry_ref[...] += jnp.sum(sel, axis=0, keepdims=True)
    cnt_ref[...] = carry_ref[...]
    es = [jnp.exp(t - tops[0]) for t in tops]
    den = es[0] + es[1] + es[2] + es[3]
    idx_out = jnp.zeros((n, 128), jnp.int32)
    p_out = jnp.zeros((n, 128), F32)
    rank_out = jnp.zeros((n, 128), jnp.int32)
    for k in range(TOP_K):
        rank = jnp.sum(jnp.where(hots[k], before, 0.0), axis=1, keepdims=True).astype(jnp.int32)
        idx_out = jnp.where(lane == k, idxs[k].astype(jnp.int32), idx_out)
        p_out = jnp.where(lane == k, es[k] / den, p_out)
        rank_out = jnp.where(lane == k, rank, rank_out)
    idx_ref[...] = idx_out
    p_ref[...] = p_out
    rank_ref[...] = rank_out


def _router(logits, stril):
    n = logits.shape[0]
    row = pl.BlockSpec((R_TILE, 128), lambda i: (i, 0))
    one = pl.BlockSpec((1, 128), lambda i: (0, 0))
    return pl.pallas_call(
        _router_kernel,
        grid=(n // R_TILE,),
        in_specs=[row, pl.BlockSpec((R_TILE, R_TILE), lambda i: (0, 0))],
        out_specs=[row, row, row, one],
        out_shape=[jax.ShapeDtypeStruct((n, 128), jnp.int32), jax.ShapeDtypeStruct((n, 128), F32),
                   jax.ShapeDtypeStruct((n, 128), jnp.int32), jax.ShapeDtypeStruct((1, 128), F32)],
        scratch_shapes=[pltpu.VMEM((1, 128), F32)],
        compiler_params=_cparams("arbitrary"),
        name="router",
    )(logits, stril)


def _dest_kernel(idx_ref, rank_ref, offs_ref, dest_ref):
    n = idx_ref.shape[0]
    lane = lax.broadcasted_iota(jnp.int32, (n, 128), 1)
    idx = idx_ref[...]
    out = jnp.zeros((n, 128), jnp.int32)
    for k in range(TOP_K):
        hot = lane == idx[:, k:k + 1]
        off = jnp.sum(jnp.where(hot, offs_ref[...], 0.0), axis=1, keepdims=True)
        out = jnp.where(lane == k, off.astype(jnp.int32), out)
    dest_ref[...] = (out + rank_ref[...]) * ROW_CHUNKS


def _dest(idx, rank, offs):
    n = idx.shape[0]
    row = pl.BlockSpec((R_TILE, 128), lambda i: (i, 0))
    return pl.pallas_call(
        _dest_kernel,
        grid=(n // R_TILE,),
        in_specs=[row, row, pl.BlockSpec((1, 128), lambda i: (0, 0))],
        out_specs=row,
        out_shape=jax.ShapeDtypeStruct((n, 128), jnp.int32),
        compiler_params=_cparams("parallel"),
        name="route_dest",
    )(idx, rank, offs)


def _dispatch_kernel(p_tiles, dest_p_ref, dest_s_ref, hn_p_ref, hn_s_ref, xs_ref, sem):
    nt = DISPATCH_TILE
    i = pl.program_id(0)

    def scatter(dest_ref, hn_ref, tile):
        base = tile * (nt * TOP_K)

        def copy(t, k):
            dst = pl.multiple_of(dest_ref[base + t * TOP_K + k], ROW_CHUNKS)
            return pltpu.make_async_copy(hn_ref.at[pl.ds(t * ROW_CHUNKS, ROW_CHUNKS)],
                                         xs_ref.at[pl.ds(dst, ROW_CHUNKS)], sem)

        def start(t, carry):
            for k in range(TOP_K):
                copy(t, k).start(priority=k % 2)
            return carry

        def wait(t, carry):
            for k in range(TOP_K):
                copy(t, k).wait()
            return carry

        for t in range(nt):
            start(t, 0)
        for t in range(nt):
            wait(t, 0)

    @pl.when(i < p_tiles)
    def _():
        scatter(dest_p_ref, hn_p_ref, i)

    @pl.when(i >= p_tiles)
    def _():
        scatter(dest_s_ref, hn_s_ref, i - p_tiles)


def _dispatch(dest_p, dest_s, hn_p, hn_s):
    p_tiles = hn_p.shape[0] // (DISPATCH_TILE * ROW_CHUNKS)
    s_tiles = hn_s.shape[0] // (DISPATCH_TILE * ROW_CHUNKS)
    n_rows = (hn_p.shape[0] + hn_s.shape[0]) * TOP_K
    return pl.pallas_call(
        functools.partial(_dispatch_kernel, p_tiles),
        grid_spec=pltpu.PrefetchScalarGridSpec(
            num_scalar_prefetch=2,
            grid=(p_tiles + s_tiles,),
            in_specs=[pl.BlockSpec((DISPATCH_TILE * ROW_CHUNKS, 128),
                                   lambda i, dp, ds: (jnp.minimum(i, p_tiles - 1), 0)),
                      pl.BlockSpec((DISPATCH_TILE * ROW_CHUNKS, 128),
                                   lambda i, dp, ds: (jnp.maximum(i - p_tiles, 0), 0))],
            out_specs=pl.BlockSpec(memory_space=pl.ANY),
            scratch_shapes=[pltpu.SemaphoreType.DMA],
        ),
        out_shape=jax.ShapeDtypeStruct((n_rows, 128), F32),
        compiler_params=_cparams("arbitrary"),
        name="moe_dispatch",
    )(dest_p, dest_s, hn_p, hn_s)


def _experts_kernel(tile_ref, exp_ref, lo_ref, hi_ref, first_ref, fresh_ref,
                    xs_ref, wgu_ref, bg_ref, bl_ref, wd_ref, bd_ref, perm_ref, ys_ref,
                    wg_s, wl_s, wd_s):
    i = pl.program_id(0)
    lo = lo_ref[i]
    hi = hi_ref[i]

    @pl.when(fresh_ref[i] == 1)
    def _():
        pw = perm_ref.shape[0]
        for c in range(2 * D // pw):
            blk = wgu_ref[0, :, c * pw:(c + 1) * pw].astype(BF16)
            sep = jnp.dot(blk, perm_ref[...], preferred_element_type=F32)
            wg_s[:, c * pw // 2:(c + 1) * pw // 2] = sep[:, :pw // 2].astype(BF16)
            wl_s[:, c * pw // 2:(c + 1) * pw // 2] = sep[:, pw // 2:].astype(BF16)
        wd_s[...] = wd_ref[0].astype(BF16)

    @pl.when(hi > lo)
    def _():
        x = _load_row_tiles(xs_ref, M_TILE).astype(BF16)
        glu = jnp.minimum(jnp.dot(x, wg_s[...], preferred_element_type=F32) + bg_ref[0], LIMIT)
        lin = jnp.clip(jnp.dot(x, wl_s[...], preferred_element_type=F32) + bl_ref[0], -LIMIT, LIMIT)
        hmid = glu * jax.nn.sigmoid(ALPHA * glu) * (lin + 1.0)
        y = _dot(hmid, wd_s[...]) + bd_ref[0]
        row = tile_ref[i] * M_TILE + lax.broadcasted_iota(jnp.int32, (M_TILE, 1), 0)
        mine = (row >= lo) & (row < hi)

        @pl.when(first_ref[i] == 1)
        def _():
            _store_row_tiles(ys_ref, jnp.where(mine, y, 0.0))

        @pl.when(first_ref[i] == 0)
        def _():
            _store_row_tiles(ys_ref, jnp.where(mine, y, _load_row_tiles(ys_ref, M_TILE)))


def _experts(meta, xs, wgu, bg, bl, wd, bd, perm):
    tile, expert, lo, hi, first, fresh = meta
    rows = pl.BlockSpec((M_TILE * ROW_CHUNKS, 128), lambda i, t, e, *_: (t[i], 0))
    bspec = pl.BlockSpec((1, 1, D), lambda i, t, e, *_: (e[i], 0, 0))
    return pl.pallas_call(
        _experts_kernel,
        grid_spec=pltpu.PrefetchScalarGridSpec(
            num_scalar_prefetch=6,
            grid=(tile.shape[0],),
            in_specs=[rows,
                      pl.BlockSpec((1, D, 2 * D), lambda i, t, e, *_: (e[i], 0, 0)), bspec, bspec,
                      pl.BlockSpec((1, D, D), lambda i, t, e, *_: (e[i], 0, 0)), bspec,
                      pl.BlockSpec(perm.shape, lambda i, *_: (0, 0))],
            out_specs=rows,
            scratch_shapes=[pltpu.VMEM((D, D), BF16)] * 3,
        ),
        out_shape=jax.ShapeDtypeStruct(xs.shape, F32),
        compiler_params=_cparams("arbitrary"),
        name="moe_experts",
    )(tile, expert, lo, hi, first, fresh, xs, wgu, bg, bl, wd, bd, perm)


def _combine_kernel(dest_ref, ys_ref, h2_ref, p_ref, nf_ref, y_ref, buf_ref, sem):
    nt = h2_ref.shape[0]
    base = pl.program_id(0) * (nt * TOP_K)

    def copy(t, k):
        src = pl.multiple_of(dest_ref[base + t * TOP_K + k], ROW_CHUNKS)
        return pltpu.make_async_copy(ys_ref.at[pl.ds(src, ROW_CHUNKS)],
                                     buf_ref.at[k, pl.ds(t * ROW_CHUNKS, ROW_CHUNKS)], sem)

    def start(t, carry):
        for k in range(TOP_K):
            copy(t, k).start(priority=k % 2)
        return carry

    def wait(t, carry):
        for k in range(TOP_K):
            copy(t, k).wait()
        return carry

    for t in range(nt):
        start(t, 0)
    for t in range(nt):
        wait(t, 0)
    p = p_ref[...]
    h3 = h2_ref[...]
    for k in range(TOP_K):
        h3 = h3 + p[:, k:k + 1] * _load_row_tiles(buf_ref.at[k], nt)
    y_ref[...] = _rms(h3, nf_ref[...])


def _combine(dest_flat, ys, h2, p, nf):
    n = h2.shape[0]
    row = pl.BlockSpec((COMBINE_TILE, D), lambda i, d: (i, 0))
    return pl.pallas_call(
        _combine_kernel,
        grid_spec=pltpu.PrefetchScalarGridSpec(
            num_scalar_prefetch=1,
            grid=(n // COMBINE_TILE,),
            in_specs=[pl.BlockSpec(memory_space=pl.ANY), row,
                      pl.BlockSpec((COMBINE_TILE, 128), lambda i, d: (i, 0)),
                      pl.BlockSpec((1, D), lambda i, d: (0, 0))],
            out_specs=row,
            scratch_shapes=[pltpu.VMEM((TOP_K, COMBINE_TILE * ROW_CHUNKS, 128), F32), pltpu.SemaphoreType.DMA],
        ),
        out_shape=jax.ShapeDtypeStruct((n, D), F32),
        compiler_params=_cparams("arbitrary"),
        name="moe_combine",
    )(dest_flat, ys, h2, p, nf)


def _rope_tables(pos):
    half = DH // 2
    inv = np.power(ROPE_BASE, -np.arange(half, dtype=np.float64) / half)
    ang = pos.astype(np.float64)[:, None] * inv[None, :]
    cos = np.concatenate([np.cos(ang), np.cos(ang)], axis=1)
    sin = np.concatenate([-np.sin(ang), np.sin(ang)], axis=1)
    return jnp.asarray(cos, F32), jnp.asarray(sin, F32)


def _retention_tables(c, reps):
    log_g = np.log1p(-np.exp2(-5.0 - np.arange(HEADS, dtype=np.float64)))
    idx = np.arange(c, dtype=np.float64)
    rel = idx[:, None] - idx[None, :]
    dmat = np.where(rel >= 0, np.exp(log_g[:, None, None] * np.maximum(rel, 0.0)), 0.0)
    big = np.zeros((HEADS, c * reps, c * reps))
    for r in range(reps):
        big[:, r * c:(r + 1) * c, r * c:(r + 1) * c] = dmat
    q_in = np.tile(np.exp(log_g[:, None] * (idx + 1.0)), (1, reps))
    k_out = np.tile(np.exp(log_g[:, None] * (c - 1.0 - idx)), (1, reps))
    g_all = np.exp(log_g * c)
    bc = lambda a: jnp.asarray(np.broadcast_to(a[..., None], a.shape + (DH,)), F32)
    return jnp.asarray(big, F32), bc(q_in), bc(k_out), bc(g_all[:, None])


def _block_tri(c, reps, strict=False):
    idx = np.arange(c * reps)
    same = (idx[:, None] // c) == (idx[None, :] // c)
    low = idx[:, None] > idx[None, :] if strict else idx[:, None] >= idx[None, :]
    return jnp.asarray(same & low, BF16), jnp.asarray(same, BF16)


def _expert_schedule(counts, n_rows):
    n_tiles = n_rows // M_TILE
    n_items = n_tiles + N_EXPERTS - 1
    ends = jnp.cumsum(counts)
    starts = ends - counts
    first_tile = starts // M_TILE
    tiles_e = jnp.where(counts > 0, (ends - 1) // M_TILE - first_tile + 1, 0)
    item_end = jnp.cumsum(tiles_e)
    item_start = item_end - tiles_e
    total = item_end[-1]
    i = jnp.arange(n_items, dtype=jnp.int32)
    live = i < total
    ic = jnp.minimum(i, total - 1)
    e = jnp.sum((item_end[None, :] <= ic[:, None]).astype(jnp.int32), axis=1)
    e = jnp.minimum(e, N_EXPERTS - 1)
    hot = e[:, None] == jnp.arange(N_EXPERTS, dtype=jnp.int32)[None, :]
    at_e = lambda table: jnp.sum(jnp.where(hot, table[None, :], 0), axis=1)
    tile = (at_e(first_tile) + ic - at_e(item_start)).astype(jnp.int32)
    lo = jnp.where(live, jnp.maximum(at_e(starts), tile * M_TILE), 0).astype(jnp.int32)
    hi = jnp.where(live, jnp.minimum(at_e(ends), (tile + 1) * M_TILE), 0).astype(jnp.int32)
    shifted = lambda a: jnp.concatenate([jnp.full((1,), -1, jnp.int32), a[:-1]])
    first = (live & (tile != shifted(tile))).astype(jnp.int32)
    fresh = (live & (e != shifted(e))).astype(jnp.int32)
    return tile, e, lo, hi, first, fresh


def kernel(x_prompt, x_sample, state_hgrn, state_ret, cache_mem_k, cache_mem_v, mem_prompt, norm_mix, w_in, hgrn_lb_logits, hgrn_out_norm, ret_out_norm, w_out, norm_x, norm_mem, w_xq, w_mk, w_mv, w_xo, norm_ffn, w_router, b_router, w_gate_up, b_gate_up, w_down, b_down, norm_final):
    bp, tp, _ = x_prompt.shape
    bs, ts, _ = x_sample.shape
    n_p, n_s = bp * tp, bs * ts
    past_len = 16384

    row = lambda a: a.reshape(1, -1).astype(F32)
    win = w_in[0].astype(BF16)
    wout = w_out[0].astype(BF16)
    wxq = w_xq[0].astype(BF16)
    wxo = w_xo[0].astype(BF16)
    wmk = w_mk[0].astype(BF16)
    wmv = w_mv[0].astype(BF16)
    wr = jnp.pad(w_router[0], ((0, 0), (0, 128 - N_EXPERTS))).astype(BF16)
    br = jnp.pad(b_router[0], (0, 128 - N_EXPERTS)).reshape(1, 128).astype(F32)
    bg = b_gate_up[0][:, 0::2].reshape(N_EXPERTS, 1, D)
    bl = b_gate_up[0][:, 1::2].reshape(N_EXPERTS, 1, D)
    bd = b_down[0].reshape(N_EXPERTS, 1, D)
    pw = 256
    perm_np = np.zeros((pw, pw), np.float32)
    perm_np[np.arange(pw), (np.arange(pw) % 2) * (pw // 2) + np.arange(pw) // 2] = 1.0
    perm = jnp.asarray(perm_np, BF16)
    nm, nx, nmem, nffn, nfin = row(norm_mix[0]), row(norm_x[0]), row(norm_mem[0]), row(norm_ffn[0]), row(norm_final)
    gh, gr = row(hgrn_out_norm[0]), row(ret_out_norm[0])
    lbl = hgrn_lb_logits.astype(F32)

    cos_p, sin_p = _rope_tables(np.arange(tp))
    dmat_p, qin_p, kout_p, gall_p = _retention_tables(RET_CHUNK, 1)
    tri_p, _ = _block_tri(HGRN_CHUNK, 1)
    mixed_p, sh_p, sr_p = _mixer_prompt(x_prompt, nm, win, lbl, gh, gr, cos_p, sin_p,
                                        dmat_p, qin_p, kout_p, gall_p, tri_p)
    cos_s, sin_s = _rope_tables(np.tile(past_len + np.arange(ts), S_BATCH))
    dmat_s, qin_s, kout_s, gall_s = _retention_tables(ts, S_BATCH)
    tri_s, ones_s = _block_tri(ts, S_BATCH)
    mixed_s, sh_s, sr_s = _mixer_sample(x_sample, nm, win, lbl, gh, gr, cos_s, sin_s,
                                        dmat_s, qin_s, kout_s, gall_s, tri_s, ones_s,
                                        state_hgrn[0], state_ret[0])

    mk_p, mv_p = _memkv(mem_prompt, nmem, wmk, wmv)
    h2_p, hn_p, lg_p = _xattn_prompt(x_prompt, mixed_p, mk_p, mv_p, wout, nx, wxq, wxo, nffn, wr, br)
    h1_s, q_s = _pre_attn(x_sample.reshape(n_s, D), mixed_s, wout, nx, wxq, F32)
    o_s = _attn_sample(q_s, cache_mem_k[0], cache_mem_v[0], ts)
    h2_s, hn_s, lg_s = _post_attn(h1_s, o_s, wxo, nffn, wr, br)

    stril, _ = _block_tri(R_TILE, 1, strict=True)
    idx, p, rank, counts = _router(jnp.concatenate([lg_p, lg_s], axis=0), stril)
    counts = counts[0, :N_EXPERTS].astype(jnp.int32)
    offs = jnp.pad(jnp.cumsum(counts) - counts, (0, 128 - N_EXPERTS)).reshape(1, 128).astype(F32)
    dest = _dest(idx, rank, offs)[:, :TOP_K].reshape(-1)
    dest_p, dest_s = dest[:n_p * TOP_K], dest[n_p * TOP_K:]

    n_rows = (n_p + n_s) * TOP_K
    xs = _dispatch(dest_p, dest_s, hn_p, hn_s)
    ys = _experts(_expert_schedule(counts, n_rows), xs, w_gate_up[0], bg, bl, w_down[0], bd, perm)
    y_p = _combine(dest_p, ys, h2_p, p[:n_p], nfin)
    y_s = _combine(dest_s, ys, h2_s, p[n_p:], nfin)

    return (y_p.reshape(bp, tp, D), y_s.reshape(bs, ts, D),
            sh_p[None], sr_p[None], mk_p[None], mv_p[None], sh_s[None], sr_s[None])
```

```python
import functools

import numpy as np
import jax
import jax.numpy as jnp
from jax import lax
from jax.experimental import pallas as pl
from jax.experimental.pallas import tpu as pltpu

F32 = jnp.float32
BF16 = jnp.bfloat16

D = 1024
HEADS = 4
DH = 128
GW = HEADS * DH
IN_COLS = 8 * GW
ROPE_BASE = 10000.0
N_MEM = 256
X_HEADS = 4
X_DH = D // X_HEADS
N_EXPERTS = 32
TOP_K = 4
LIMIT = 7.0
ALPHA = 1.702
EPS = 1e-6
EXP_CLAMP = 80.0

HGRN_CHUNK = 64
RET_CHUNK = 128
T_TILE = 256
S_BATCH = 8
ROW_TILE = 512
Q_TILE = 512
A_BATCH = 8
R_TILE = 1024
M_TILE = 512
DISPATCH_TILE = 256
COMBINE_TILE = 128
RUN_PIECE = 16
RUN_ROWS = COMBINE_TILE + RUN_PIECE
VMEM_LIMIT = 56 * 1024 * 1024


def _cparams(*sem):
    return pltpu.CompilerParams(dimension_semantics=sem, vmem_limit_bytes=VMEM_LIMIT)


def _dot(a, b):
    return jnp.dot(a.astype(BF16), b.astype(BF16), preferred_element_type=F32)


def _dot_nt(a, b):
    return lax.dot_general(a.astype(BF16), b.astype(BF16), (((1,), (1,)), ((), ())),
                           preferred_element_type=F32)


def _dot_tn(a, b):
    return lax.dot_general(a, b, (((0,), (0,)), ((), ())), preferred_element_type=F32)


def _rms(x, g):
    return x * lax.rsqrt(jnp.mean(x * x, axis=-1, keepdims=True) + EPS) * g


ROW_CHUNKS = D // 128


def _load_row_tiles(ref, n):
    return jnp.concatenate([ref[pl.ds(c, n, stride=ROW_CHUNKS), :] for c in range(ROW_CHUNKS)], axis=1)


def _store_row_tiles(ref, x):
    n = x.shape[0]
    for c in range(ROW_CHUNKS):
        ref[pl.ds(c, n, stride=ROW_CHUNKS), :] = x[:, c * 128:(c + 1) * 128]


def _cumsum_rows(tri, v):
    hi = v.astype(BF16)
    lo = (v - hi.astype(F32)).astype(BF16)
    return (jnp.dot(tri, hi, preferred_element_type=F32)
            + jnp.dot(tri, lo, preferred_element_type=F32))


def _lower_bound(lbl_ref):
    l = lbl_ref[...]
    m = jnp.max(l, axis=0, keepdims=True)
    e = jnp.exp(l - m)
    return e[0:1, :] / jnp.sum(e, axis=0, keepdims=True)


def _row_to_col(row):
    n = row.shape[1]
    eye = lax.broadcasted_iota(jnp.int32, (n, n), 0) == lax.broadcasted_iota(jnp.int32, (n, n), 1)
    return jnp.sum(jnp.where(eye, jnp.broadcast_to(row, (n, n)), 0.0), axis=1, keepdims=True)


def _rotary(x, cs, sn):
    return x * cs + pltpu.roll(x, DH // 2, 1) * sn


def _hgrn_out(o, gh, gate):
    on = o * lax.rsqrt(jnp.mean(o * o, axis=-1, keepdims=True) + EPS) * gh
    return on * jax.nn.sigmoid(gate)


def _ret_out(o, gr, gate):
    mu = jnp.mean(o, axis=-1, keepdims=True)
    xc = o - mu
    var = jnp.mean(xc * xc, axis=-1, keepdims=True)
    return xc * lax.rsqrt(var + EPS) * gr * (gate * jax.nn.sigmoid(gate))


def _mixer_prompt_kernel(x_ref, nm_ref, win_ref, lbl_ref, gh_ref, gr_ref, cos_ref, sin_ref,
                         dmat_ref, qin_ref, kout_ref, gall_ref, tri_ref,
                         mixed_ref, sh_ref, sr_ref, proj_ref, f_ref, lc_ref):
    @pl.when(pl.program_id(1) == 0)
    def _():
        sh_ref[...] = jnp.zeros_like(sh_ref)
        sr_ref[...] = jnp.zeros_like(sr_ref)

    xn = _rms(x_ref[0], nm_ref[...]).astype(BF16)
    proj_ref[...] = jnp.dot(xn, win_ref[...], preferred_element_type=F32)
    lb = _lower_bound(lbl_ref)
    gh = gh_ref[...]
    gr = gr_ref[...]
    c = HGRN_CHUNK
    causal = (lax.broadcasted_iota(jnp.int32, (c, c), 0) >= lax.broadcasted_iota(jnp.int32, (c, c), 1))

    def gate_step(ci, worst):
        rows = pl.ds(pl.multiple_of(ci * c, c), c)
        f = lb + (1.0 - lb) * jax.nn.sigmoid(proj_ref[rows, GW:2 * GW])
        lc = _cumsum_rows(tri_ref[...], jnp.log(f))
        f_ref[rows, :] = f
        lc_ref[rows, :] = lc
        return jnp.maximum(worst, jnp.max(jnp.abs(lc - lc[c // 2 - 1:c // 2, :])))

    worst = lax.fori_loop(0, T_TILE // c, gate_step, jnp.float32(0.0), unroll=True)

    def scores_factored(hq, kk, lc, h, r0):
        sl = slice(h * DH, (h + 1) * DH)
        mid = lc[c // 2 - 1:c // 2, sl]
        qd = hq[:, sl] * jnp.exp(lc[:, sl] - mid)
        kd = kk[:, sl] * jnp.exp(mid - lc[:, sl])
        return jnp.where(causal, _dot_nt(qd, kd), 0.0)

    def scores_termwise(hq, kk, lc, h, r0):
        sl = slice(h * DH, (h + 1) * DH)
        col_id = lax.broadcasted_iota(jnp.int32, (c, c), 1)

        sub_id = lax.broadcasted_iota(jnp.int32, (8, DH), 0)

        def row_of(ref, s):
            grp = ref[pl.ds(pl.multiple_of(r0 + lax.shift_left(lax.shift_right_logical(s, 3), 3), 8), 8), sl]
            return jnp.sum(jnp.where(sub_id == (s & 7), grp, 0.0), axis=0, keepdims=True)

        def body(s, a):
            lcs = row_of(lc_ref, s)
            ks_row = 1.0 - row_of(f_ref, s)
            w = jnp.exp(jnp.minimum(lc[:, sl] - lcs, 0.0))
            col = jnp.sum(hq[:, sl] * ks_row * w, axis=1, keepdims=True)
            return a + jnp.where(col_id == s, col, 0.0)

        return jnp.where(causal, lax.fori_loop(0, c, body, jnp.zeros((c, c), F32)), 0.0)

    def hgrn_step(scores, ci, carry):
        r0 = pl.multiple_of(ci * c, c)
        rows = pl.ds(r0, c)
        hq = proj_ref[rows, 0:GW] * (DH ** -0.5)
        kk = 1.0 - f_ref[rows, :]
        lc = lc_ref[rows, :]
        last = lc[c - 1:c, :]
        qs = hq * jnp.exp(lc)
        ks = kk * jnp.exp(last - lc)
        dl = jnp.exp(last)
        for h in range(HEADS):
            sl = slice(h * DH, (h + 1) * DH)
            v = proj_ref[rows, 2 * GW + h * DH:2 * GW + (h + 1) * DH].astype(BF16)
            a = scores(hq, kk, lc, h, r0)
            s_old = sh_ref[0, h]
            o = _dot(a, v) + _dot(qs[:, sl], s_old)
            sh_ref[0, h] = _row_to_col(dl[:, sl]) * s_old + _dot_tn(ks[:, sl].astype(BF16), v)
            gate = proj_ref[rows, 3 * GW + h * DH:3 * GW + (h + 1) * DH]
            mixed_ref[0, rows, sl] = _hgrn_out(o, gh, gate).astype(BF16)
        return carry

    @pl.when(worst <= EXP_CLAMP)
    def _():
        lax.fori_loop(0, T_TILE // c, functools.partial(hgrn_step, scores_factored), 0, unroll=True)

    @pl.when(jnp.logical_not(worst <= EXP_CLAMP))
    def _():
        lax.fori_loop(0, T_TILE // c, functools.partial(hgrn_step, scores_termwise), 0)

    cr = RET_CHUNK

    def ret_step(ci, carry):
        rows = pl.ds(pl.multiple_of(ci * cr, cr), cr)
        cs = cos_ref[rows, :]
        sn = sin_ref[rows, :]
        for h in range(HEADS):
            base = 4 * GW + h * DH
            q = _rotary(proj_ref[rows, base:base + DH], cs, sn)
            k = _rotary(proj_ref[rows, base + GW:base + GW + DH], cs, sn) * (DH ** -0.5)
            v = proj_ref[rows, base + 2 * GW:base + 2 * GW + DH].astype(BF16)
            sc = _dot_nt(q, k) * dmat_ref[h]
            s_old = sr_ref[0, h]
            o = _dot(sc, v) + _dot(q * qin_ref[h], s_old)
            sr_ref[0, h] = gall_ref[h] * s_old + _dot_tn((k * kout_ref[h]).astype(BF16), v)
            gate = proj_ref[rows, base + 3 * GW:base + 3 * GW + DH]
            mixed_ref[0, rows, GW + h * DH:GW + (h + 1) * DH] = _ret_out(o, gr, gate).astype(BF16)
        return carry

    lax.fori_loop(0, T_TILE // cr, ret_step, 0, unroll=True)


def _mixer_prompt(x, nm, win, lbl, gh, gr, cos, sin, dmat, qin, kout, gall, tri):
    b, t, _ = x.shape
    full = lambda shape: pl.BlockSpec(shape, lambda i, j: (0,) * len(shape))
    state = pl.BlockSpec((1, HEADS, DH, DH), lambda i, j: (i, 0, 0, 0))
    return pl.pallas_call(
        _mixer_prompt_kernel,
        grid=(b, t // T_TILE),
        in_specs=[
            pl.BlockSpec((1, T_TILE, D), lambda i, j: (i, j, 0)),
            full((1, D)), full((D, IN_COLS)), full(lbl.shape), full((1, DH)), full((1, DH)),
            pl.BlockSpec((T_TILE, DH), lambda i, j: (j, 0)),
            pl.BlockSpec((T_TILE, DH), lambda i, j: (j, 0)),
            full(dmat.shape), full(qin.shape), full(kout.shape), full(gall.shape), full(tri.shape),
        ],
        out_specs=[pl.BlockSpec((1, T_TILE, D), lambda i, j: (i, j, 0)), state, state],
        out_shape=[jax.ShapeDtypeStruct((b, t, D), BF16),
                   jax.ShapeDtypeStruct((b, HEADS, DH, DH), F32),
                   jax.ShapeDtypeStruct((b, HEADS, DH, DH), F32)],
        scratch_shapes=[pltpu.VMEM((T_TILE, IN_COLS), F32), pltpu.VMEM((T_TILE, GW), F32),
                        pltpu.VMEM((T_TILE, GW), F32)],
        compiler_params=_cparams("parallel", "arbitrary"),
        name="mixer_prompt",
    )(x, nm, win, lbl, gh, gr, cos, sin, dmat, qin, kout, gall, tri)


def _mixer_sample_kernel(x_ref, nm_ref, win_ref, lbl_ref, gh_ref, gr_ref, cos_ref, sin_ref,
                         dmat_ref, qin_ref, kout_ref, gall_ref, tri_ref, ones_ref,
                         shin_ref, srin_ref, mixed_ref, sh_ref, sr_ref, proj_ref):
    nb, ts, _ = x_ref.shape
    rows_n = nb * ts
    xn = _rms(x_ref[...].reshape(rows_n, D), nm_ref[...]).astype(BF16)
    proj_ref[...] = jnp.dot(xn, win_ref[...], preferred_element_type=F32)
    lb = _lower_bound(lbl_ref)
    gh = gh_ref[...]
    gr = gr_ref[...]
    hq = proj_ref[:, 0:GW] * (DH ** -0.5)
    f = lb + (1.0 - lb) * jax.nn.sigmoid(proj_ref[:, GW:2 * GW])
    kk = 1.0 - f
    lf = jnp.log(f)
    lc = _cumsum_rows(tri_ref[...], lf)
    last = _cumsum_rows(ones_ref[...], lf)
    qd = hq * jnp.exp(lc)
    ks = kk * jnp.exp(last - lc)
    dl = jnp.exp(last)
    t_pos = lax.broadcasted_iota(jnp.int32, (nb, ts, 1), 1)

    def intra_termwise(h):
        sl = slice(h * DH, (h + 1) * DH)
        q3 = hq[:, sl].reshape(nb, ts, DH)
        k3 = kk[:, sl].reshape(nb, ts, DH)
        lc3 = lc[:, sl].reshape(nb, ts, DH)
        v3 = proj_ref[:, 2 * GW + h * DH:2 * GW + (h + 1) * DH].reshape(nb, ts, DH)
        o3 = jnp.zeros((nb, ts, DH), F32)
        for s in range(ts):
            w = jnp.exp(jnp.minimum(lc3 - lc3[:, s:s + 1, :], 0.0))
            score = jnp.sum(q3 * k3[:, s:s + 1, :] * w, axis=-1, keepdims=True)
            o3 = o3 + jnp.where(t_pos >= s, score, 0.0) * v3[:, s:s + 1, :]
        return o3.reshape(rows_n, DH)

    for h in range(HEADS):
        sl = slice(h * DH, (h + 1) * DH)
        v = proj_ref[:, 2 * GW + h * DH:2 * GW + (h + 1) * DH]
        o_intra = intra_termwise(h)
        gate = proj_ref[:, 3 * GW + h * DH:3 * GW + (h + 1) * DH]
        for b in range(nb):
            r = slice(b * ts, (b + 1) * ts)
            s_old = shin_ref[b, h]
            o = o_intra[r] + _dot(qd[r, sl], s_old)
            sh_ref[b, h] = (_row_to_col(dl[b * ts:b * ts + 1, sl]) * s_old
                            + _dot_tn(ks[r, sl], v[r]))
            mixed_ref[r, sl] = _hgrn_out(o, gh, gate[r])

    cs = cos_ref[...]
    sn = sin_ref[...]
    for h in range(HEADS):
        base = 4 * GW + h * DH
        q = _rotary(proj_ref[:, base:base + DH], cs, sn)
        k = _rotary(proj_ref[:, base + GW:base + GW + DH], cs, sn) * (DH ** -0.5)
        v = proj_ref[:, base + 2 * GW:base + 2 * GW + DH]
        sc = _dot_nt(q, k) * dmat_ref[h]
        o_intra = _dot(sc, v)
        qi = q * qin_ref[h]
        ko = k * kout_ref[h]
        gate = proj_ref[:, base + 3 * GW:base + 3 * GW + DH]
        for b in range(nb):
            r = slice(b * ts, (b + 1) * ts)
            s_old = srin_ref[b, h]
            o = o_intra[r] + _dot(qi[r], s_old)
            sr_ref[b, h] = gall_ref[h] * s_old + _dot_tn(ko[r], v[r])
            mixed_ref[r, GW + h * DH:GW + (h + 1) * DH] = _ret_out(o, gr, gate[r])


def _mixer_sample(x, nm, win, lbl, gh, gr, cos, sin, dmat, qin, kout, gall, tri, ones, sh, sr):
    b, t, _ = x.shape
    nb = S_BATCH
    full = lambda shape: pl.BlockSpec(shape, lambda i: (0,) * len(shape))
    state = pl.BlockSpec((nb, HEADS, DH, DH), lambda i: (i, 0, 0, 0))
    return pl.pallas_call(
        _mixer_sample_kernel,
        grid=(b // nb,),
        in_specs=[
            pl.BlockSpec((nb, t, D), lambda i: (i, 0, 0)),
            full((1, D)), full((D, IN_COLS)), full(lbl.shape), full((1, DH)), full((1, DH)),
            full(cos.shape), full(sin.shape),
            full(dmat.shape), full(qin.shape), full(kout.shape), full(gall.shape),
            full(tri.shape), full(ones.shape), state, state,
        ],
        out_specs=[pl.BlockSpec((nb * t, D), lambda i: (i, 0)), state, state],
        out_shape=[jax.ShapeDtypeStruct((b * t, D), F32),
                   jax.ShapeDtypeStruct(sh.shape, F32),
                   jax.ShapeDtypeStruct(sr.shape, F32)],
        scratch_shapes=[pltpu.VMEM((nb * t, IN_COLS), F32)],
        compiler_params=_cparams("parallel"),
        name="mixer_sample",
    )(x, nm, win, lbl, gh, gr, cos, sin, dmat, qin, kout, gall, tri, ones, sh, sr)


def _memkv_kernel(mem_ref, g_ref, wk_ref, wv_ref, k_ref, v_ref):
    mn = _rms(mem_ref[0], g_ref[...]).astype(BF16)
    k = jnp.dot(mn, wk_ref[...], preferred_element_type=F32)
    v = jnp.dot(mn, wv_ref[...], preferred_element_type=F32)
    for h in range(X_HEADS):
        k_ref[0, h] = k[:, h * X_DH:(h + 1) * X_DH]
        v_ref[0, h] = v[:, h * X_DH:(h + 1) * X_DH]


def _memkv(mem, g, wk, wv):
    b = mem.shape[0]
    full = lambda shape: pl.BlockSpec(shape, lambda i: (0,) * len(shape))
    kv = pl.BlockSpec((1, X_HEADS, N_MEM, X_DH), lambda i: (i, 0, 0, 0))
    shape = jax.ShapeDtypeStruct((b, X_HEADS, N_MEM, X_DH), F32)
    return pl.pallas_call(
        _memkv_kernel,
        grid=(b,),
        in_specs=[pl.BlockSpec((1, N_MEM, D), lambda i: (i, 0, 0)), full((1, D)), full((D, D)), full((D, D))],
        out_specs=[kv, kv],
        out_shape=[shape, shape],
        compiler_params=_cparams("parallel"),
        name="mem_kv",
    )(mem, g, wk, wv)


def _pre_attn_kernel(x_ref, mixed_ref, wout_ref, nx_ref, wxq_ref, h1_ref, q_ref):
    h1 = x_ref[...] + _dot(mixed_ref[...], wout_ref[...])
    h1_ref[...] = h1
    hn = _rms(h1, nx_ref[...])
    q_ref[...] = _dot(hn, wxq_ref[...]).astype(q_ref.dtype)


def _pre_attn(x, mixed, wout, nx, wxq, q_dtype):
    n = x.shape[0]
    row = pl.BlockSpec((ROW_TILE, D), lambda i: (i, 0))
    full = lambda shape: pl.BlockSpec(shape, lambda i: (0,) * len(shape))
    return pl.pallas_call(
        _pre_attn_kernel,
        grid=(n // ROW_TILE,),
        in_specs=[row, row, full((D, D)), full((1, D)), full((D, D))],
        out_specs=[row, row],
        out_shape=[jax.ShapeDtypeStruct((n, D), F32), jax.ShapeDtypeStruct((n, D), q_dtype)],
        compiler_params=_cparams("parallel"),
        name="pre_attn",
    )(x, mixed, wout, nx, wxq)


def _post_attn_kernel(h1_ref, o_ref, wxo_ref, nf_ref, wr_ref, br_ref, h2_ref, hn_ref, lg_ref):
    h2 = h1_ref[...] + _dot(o_ref[...], wxo_ref[...])
    h2_ref[...] = h2
    hn = _rms(h2, nf_ref[...])
    _store_row_tiles(hn_ref, hn)
    lg_ref[...] = _dot(hn, wr_ref[...]) + br_ref[...]


def _post_attn(h1, o, wxo, nf, wr, br):
    n = h1.shape[0]
    row = pl.BlockSpec((ROW_TILE, D), lambda i: (i, 0))
    full = lambda shape: pl.BlockSpec(shape, lambda i: (0,) * len(shape))
    return pl.pallas_call(
        _post_attn_kernel,
        grid=(n // ROW_TILE,),
        in_specs=[row, row, full((D, D)), full((1, D)), full((D, 128)), full((1, 128))],
        out_specs=[row, pl.BlockSpec((ROW_TILE * ROW_CHUNKS, 128), lambda i: (i, 0)),
                   pl.BlockSpec((ROW_TILE, 128), lambda i: (i, 0))],
        out_shape=[jax.ShapeDtypeStruct((n, D), F32), jax.ShapeDtypeStruct((n * ROW_CHUNKS, 128), F32),
                   jax.ShapeDtypeStruct((n, 128), F32)],
        compiler_params=_cparams("parallel"),
        name="post_attn",
    )(h1, o, wxo, nf, wr, br)


def _softmax_rows(s):
    m = jnp.max(s, axis=-1, keepdims=True)
    e = jnp.exp(s - m)
    return e / jnp.sum(e, axis=-1, keepdims=True)


def _xattn_prompt_kernel(x_ref, mixed_ref, k_ref, v_ref, wout_ref, nx_ref, wxq_ref, wxo_ref,
                         nf_ref, wr_ref, br_ref, h2_ref, hn_ref, lg_ref, o_scr):
    h1 = x_ref[0] + _dot(mixed_ref[0], wout_ref[...])
    q = _dot(_rms(h1, nx_ref[...]), wxq_ref[...]).astype(BF16)
    for h in range(X_HEADS):
        sl = slice(h * X_DH, (h + 1) * X_DH)
        s = _dot_nt(q[:, sl], k_ref[0, h]) * (X_DH ** -0.5)
        o_scr[:, sl] = _dot(_softmax_rows(s), v_ref[0, h]).astype(BF16)
    h2 = h1 + _dot(o_scr[...], wxo_ref[...])
    h2_ref[...] = h2
    hn = _rms(h2, nf_ref[...])
    _store_row_tiles(hn_ref, hn)
    lg_ref[...] = _dot(hn, wr_ref[...]) + br_ref[...]


def _xattn_prompt(x, mixed, mk, mv, wout, nx, wxq, wxo, nf, wr, br):
    b, seq, _ = x.shape
    per_seq = seq // Q_TILE
    full = lambda shape: pl.BlockSpec(shape, lambda i, j: (0,) * len(shape))
    tok = pl.BlockSpec((1, Q_TILE, D), lambda i, j: (i, j, 0))
    kv = pl.BlockSpec((1, X_HEADS, N_MEM, X_DH), lambda i, j: (i, 0, 0, 0))
    row = pl.BlockSpec((Q_TILE, D), lambda i, j: (i * per_seq + j, 0))
    n = b * seq
    return pl.pallas_call(
        _xattn_prompt_kernel,
        grid=(b, per_seq),
        in_specs=[tok, tok, kv, kv, full((D, D)), full((1, D)), full((D, D)), full((D, D)),
                  full((1, D)), full((D, 128)), full((1, 128))],
        out_specs=[row, pl.BlockSpec((Q_TILE * ROW_CHUNKS, 128), lambda i, j: (i * per_seq + j, 0)),
                   pl.BlockSpec((Q_TILE, 128), lambda i, j: (i * per_seq + j, 0))],
        out_shape=[jax.ShapeDtypeStruct((n, D), F32), jax.ShapeDtypeStruct((n * ROW_CHUNKS, 128), F32),
                   jax.ShapeDtypeStruct((n, 128), F32)],
        scratch_shapes=[pltpu.VMEM((Q_TILE, D), BF16)],
        compiler_params=_cparams("parallel", "arbitrary"),
        name="xattn_prompt",
    )(x, mixed, mk, mv, wout, nx, wxq, wxo, nf, wr, br)


def _attn_sample_kernel(q_ref, k_ref, v_ref, o_ref):
    nb = k_ref.shape[0]
    ts = q_ref.shape[0] // nb
    units = [(b, h) for b in range(nb) for h in range(X_HEADS)]
    s = jnp.concatenate(
        [_dot_nt(q_ref[b * ts:(b + 1) * ts, h * X_DH:(h + 1) * X_DH], k_ref[b, h]) for b, h in units], axis=0)
    p = _softmax_rows(s * (X_DH ** -0.5))
    for i, (b, h) in enumerate(units):
        o_ref[b * ts:(b + 1) * ts, h * X_DH:(h + 1) * X_DH] = _dot(p[i * ts:(i + 1) * ts], v_ref[b, h])


def _attn_sample(q, ck, cv, ts):
    n = q.shape[0]
    nb = A_BATCH
    kv = pl.BlockSpec((nb, X_HEADS, N_MEM, X_DH), lambda i: (i, 0, 0, 0))
    qs = pl.BlockSpec((nb * ts, D), lambda i: (i, 0))
    return pl.pallas_call(
        _attn_sample_kernel,
        grid=(n // (nb * ts),),
        in_specs=[qs, kv, kv],
        out_specs=qs,
        out_shape=jax.ShapeDtypeStruct((n, D), F32),
        compiler_params=_cparams("parallel"),
        name="attn_sample",
    )(q, ck, cv)


def _router_kernel(lg_ref, stril_ref, idx_ref, prep_ref, rank_ref, cst_ref, sub_ref, cnt_ref, carry_ref):
    @pl.when(pl.program_id(0) == 0)
    def _():
        carry_ref[...] = jnp.zeros_like(carry_ref)

    n = lg_ref.shape[0]
    lane = lax.broadcasted_iota(jnp.int32, (n, 128), 1)
    lane_f = lane.astype(F32)
    l = jnp.where(lane < N_EXPERTS, lg_ref[...], -jnp.inf)
    tops, idxs, hots = [], [], []
    for _ in range(TOP_K):
        m = jnp.max(l, axis=1, keepdims=True)
        idx = jnp.min(jnp.where(l == m, lane_f, 128.0), axis=1, keepdims=True)
        hot = lane_f == idx
        l = jnp.where(hot, -jnp.inf, l)
        tops.append(m)
        idxs.append(idx)
        hots.append(hot)
    sel = jnp.where(hots[0] | hots[1] | hots[2] | hots[3], 1.0, 0.0)
    before = jnp.dot(stril_ref[...], sel.astype(BF16), preferred_element_type=F32) + carry_ref[...]
    carry_ref[...] += jnp.sum(sel, axis=0, keepdims=True)
    cnt_ref[...] = carry_ref[...]
    subs = n // COMBINE_TILE
    cst_ref[...] = jnp.concatenate([before[j * COMBINE_TILE:j * COMBINE_TILE + 1, :] for j in range(subs)], axis=0)
    sub_ref[...] = jnp.concatenate(
        [jnp.sum(sel[j * COMBINE_TILE:(j + 1) * COMBINE_TILE, :], axis=0, keepdims=True) for j in range(subs)], axis=0)
    es = [jnp.exp(t - tops[0]) for t in tops]
    den = es[0] + es[1] + es[2] + es[3]
    idx_out = jnp.zeros((n, 128), jnp.int32)
    rank_out = jnp.zeros((n, 128), jnp.int32)
    for k in range(TOP_K):
        rank = jnp.sum(jnp.where(hots[k], before, 0.0), axis=1, keepdims=True).astype(jnp.int32)
        idx_out = jnp.where(lane == k, idxs[k].astype(jnp.int32), idx_out)
        rank_out = jnp.where(lane == k, rank, rank_out)
        prep_ref[pl.ds(k, n, stride=TOP_K), :] = jnp.broadcast_to(es[k] / den, (n, 128))
    idx_ref[...] = idx_out
    rank_ref[...] = rank_out


def _router(logits, stril):
    n = logits.shape[0]
    subs = R_TILE // COMBINE_TILE
    row = pl.BlockSpec((R_TILE, 128), lambda i: (i, 0))
    one = pl.BlockSpec((1, 128), lambda i: (0, 0))
    sub = pl.BlockSpec((subs, 128), lambda i: (i, 0))
    n_sub = n // COMBINE_TILE
    return pl.pallas_call(
        _router_kernel,
        grid=(n // R_TILE,),
        in_specs=[row, pl.BlockSpec((R_TILE, R_TILE), lambda i: (0, 0))],
        out_specs=[row, pl.BlockSpec((R_TILE * TOP_K, 128), lambda i: (i, 0)), row, sub, sub, one],
        out_shape=[jax.ShapeDtypeStruct((n, 128), jnp.int32), jax.ShapeDtypeStruct((n * TOP_K, 128), F32),
                   jax.ShapeDtypeStruct((n, 128), jnp.int32), jax.ShapeDtypeStruct((n_sub, 128), F32),
                   jax.ShapeDtypeStruct((n_sub, 128), F32), jax.ShapeDtypeStruct((1, 128), F32)],
        scratch_shapes=[pltpu.VMEM((1, 128), F32)],
        compiler_params=_cparams("arbitrary"),
        name="router",
    )(logits, stril)


def _dest_kernel(n_rows, idx_ref, rank_ref, offs_ref, cst_ref, sub_ref, dest_ref, pos_ref, src_ref, npc_ref):
    n = idx_ref.shape[0]
    subs = n // COMBINE_TILE
    lane = lax.broadcasted_iota(jnp.int32, (n, 128), 1)
    lane_row = lax.broadcasted_iota(jnp.int32, (subs, 128), 1).astype(F32)
    offs = offs_ref[...]
    before = cst_ref[...]
    start = offs + before
    pieces = jnp.floor((sub_ref[...] + (RUN_PIECE - 1)) * (1.0 / RUN_PIECE))
    shift = jnp.maximum(start + pieces * RUN_PIECE - n_rows, 0.0)
    src_ref[...] = ((start - shift) * ROW_CHUNKS).astype(jnp.int32)
    npc_ref[...] = pieces.astype(jnp.int32)
    slot = lane_row * RUN_ROWS - before + shift
    slot = jnp.concatenate([jnp.broadcast_to(slot[j:j + 1, :], (COMBINE_TILE, 128)) for j in range(subs)], axis=0)
    idx = idx_ref[...]
    d_out = jnp.zeros((n, 128), F32)
    p_out = jnp.zeros((n, 128), F32)
    for k in range(TOP_K):
        hot = lane == idx[:, k:k + 1]
        d_out = jnp.where(lane == k, jnp.sum(jnp.where(hot, offs, 0.0), axis=1, keepdims=True), d_out)
        p_out = jnp.where(lane == k, jnp.sum(jnp.where(hot, slot, 0.0), axis=1, keepdims=True), p_out)
    rank = rank_ref[...]
    dest_ref[...] = (d_out.astype(jnp.int32) + rank) * ROW_CHUNKS
    pos_ref[...] = (p_out.astype(jnp.int32) + rank) * ROW_CHUNKS


def _dest(idx, rank, offs, cst, sub, n_rows):
    n = idx.shape[0]
    subs = R_TILE // COMBINE_TILE
    row = pl.BlockSpec((R_TILE, 128), lambda i: (i, 0))
    tab = pl.BlockSpec((subs, 128), lambda i: (i, 0))
    n_sub = n // COMBINE_TILE
    return pl.pallas_call(
        functools.partial(_dest_kernel, n_rows),
        grid=(n // R_TILE,),
        in_specs=[row, row, pl.BlockSpec((1, 128), lambda i: (0, 0)), tab, tab],
        out_specs=[row, row, tab, tab],
        out_shape=[jax.ShapeDtypeStruct((n, 128), jnp.int32), jax.ShapeDtypeStruct((n, 128), jnp.int32),
                   jax.ShapeDtypeStruct((n_sub, 128), jnp.int32), jax.ShapeDtypeStruct((n_sub, 128), jnp.int32)],
        compiler_params=_cparams("parallel"),
        name="route_dest",
    )(idx, rank, offs, cst, sub)


def _dispatch_kernel(p_tiles, dest_p_ref, dest_s_ref, hn_p_ref, hn_s_ref, xs_ref, sem):
    nt = DISPATCH_TILE
    i = pl.program_id(0)

    def scatter(dest_ref, hn_ref, tile):
        base = tile * (nt * TOP_K)

        def copy(t, k):
            dst = pl.multiple_of(dest_ref[base + t * TOP_K + k], ROW_CHUNKS)
            return pltpu.make_async_copy(hn_ref.at[pl.ds(t * ROW_CHUNKS, ROW_CHUNKS)],
                                         xs_ref.at[pl.ds(dst, ROW_CHUNKS)], sem)

        def start(t, carry):
            for k in range(TOP_K):
                copy(t, k).start(priority=k % 2)
            return carry

        def wait(t, carry):
            for k in range(TOP_K):
                copy(t, k).wait()
            return carry

        for t in range(nt):
            start(t, 0)
        for t in range(nt):
            wait(t, 0)

    @pl.when(i < p_tiles)
    def _():
        scatter(dest_p_ref, hn_p_ref, i)

    @pl.when(i >= p_tiles)
    def _():
        scatter(dest_s_ref, hn_s_ref, i - p_tiles)


def _dispatch(dest_p, dest_s, hn_p, hn_s):
    p_tiles = hn_p.shape[0] // (DISPATCH_TILE * ROW_CHUNKS)
    s_tiles = hn_s.shape[0] // (DISPATCH_TILE * ROW_CHUNKS)
    n_rows = (hn_p.shape[0] + hn_s.shape[0]) * TOP_K
    return pl.pallas_call(
        functools.partial(_dispatch_kernel, p_tiles),
        grid_spec=pltpu.PrefetchScalarGridSpec(
            num_scalar_prefetch=2,
            grid=(p_tiles + s_tiles,),
            in_specs=[pl.BlockSpec((DISPATCH_TILE * ROW_CHUNKS, 128),
                                   lambda i, dp, ds: (jnp.minimum(i, p_tiles - 1), 0)),
                      pl.BlockSpec((DISPATCH_TILE * ROW_CHUNKS, 128),
                                   lambda i, dp, ds: (jnp.maximum(i - p_tiles, 0), 0))],
            out_specs=pl.BlockSpec(memory_space=pl.ANY),
            scratch_shapes=[pltpu.SemaphoreType.DMA],
        ),
        out_shape=jax.ShapeDtypeStruct((n_rows, 128), F32),
        compiler_params=_cparams("arbitrary"),
        name="moe_dispatch",
    )(dest_p, dest_s, hn_p, hn_s)


def _experts_kernel(tile_ref, exp_ref, lo_ref, hi_ref, first_ref, fresh_ref,
                    xs_ref, wgu_ref, bg_ref, bl_ref, wd_ref, bd_ref, perm_ref, ys_ref,
                    wg_s, wl_s, wd_s):
    i = pl.program_id(0)
    lo = lo_ref[i]
    hi = hi_ref[i]

    @pl.when(fresh_ref[i] == 1)
    def _():
        pw = perm_ref.shape[0]
        for c in range(2 * D // pw):
            blk = wgu_ref[0, :, c * pw:(c + 1) * pw].astype(BF16)
            sep = jnp.dot(blk, perm_ref[...], preferred_element_type=F32)
            wg_s[:, c * pw // 2:(c + 1) * pw // 2] = sep[:, :pw // 2].astype(BF16)
            wl_s[:, c * pw // 2:(c + 1) * pw // 2] = sep[:, pw // 2:].astype(BF16)
        wd_s[...] = wd_ref[0].astype(BF16)

    @pl.when(hi > lo)
    def _():
        x = _load_row_tiles(xs_ref, M_TILE).astype(BF16)
        glu = jnp.minimum(jnp.dot(x, wg_s[...], preferred_element_type=F32) + bg_ref[0], LIMIT)
        lin = jnp.clip(jnp.dot(x, wl_s[...], preferred_element_type=F32) + bl_ref[0], -LIMIT, LIMIT)
        hmid = glu * jax.nn.sigmoid(ALPHA * glu) * (lin + 1.0)
        y = _dot(hmid, wd_s[...]) + bd_ref[0]
        row = tile_ref[i] * M_TILE + lax.broadcasted_iota(jnp.int32, (M_TILE, 1), 0)
        mine = (row >= lo) & (row < hi)

        @pl.when(first_ref[i] == 1)
        def _():
            _store_row_tiles(ys_ref, jnp.where(mine, y, 0.0))

        @pl.when(first_ref[i] == 0)
        def _():
            _store_row_tiles(ys_ref, jnp.where(mine, y, _load_row_tiles(ys_ref, M_TILE)))


def _experts(meta, xs, wgu, bg, bl, wd, bd, perm):
    tile, expert, lo, hi, first, fresh = meta
    rows = pl.BlockSpec((M_TILE * ROW_CHUNKS, 128), lambda i, t, e, *_: (t[i], 0))
    bspec = pl.BlockSpec((1, 1, D), lambda i, t, e, *_: (e[i], 0, 0))
    return pl.pallas_call(
        _experts_kernel,
        grid_spec=pltpu.PrefetchScalarGridSpec(
            num_scalar_prefetch=6,
            grid=(tile.shape[0],),
            in_specs=[rows,
                      pl.BlockSpec((1, D, 2 * D), lambda i, t, e, *_: (e[i], 0, 0)), bspec, bspec,
                      pl.BlockSpec((1, D, D), lambda i, t, e, *_: (e[i], 0, 0)), bspec,
                      pl.BlockSpec(perm.shape, lambda i, *_: (0, 0))],
            out_specs=rows,
            scratch_shapes=[pltpu.VMEM((D, D), BF16)] * 3,
        ),
        out_shape=jax.ShapeDtypeStruct(xs.shape, F32),
        compiler_params=_cparams("arbitrary"),
        name="moe_experts",
    )(tile, expert, lo, hi, first, fresh, xs, wgu, bg, bl, wd, bd, perm)


def _combine_kernel(tile0, pos_ref, src_ref, npc_ref, ys_ref, h2_ref, prep_ref, nf_ref, y_ref,
                    buf_ref, moe_ref, sem):
    nt = COMBINE_TILE
    i = pl.program_id(0)
    piece_rows = RUN_PIECE * ROW_CHUNKS

    def piece(tile, e, j):
        tab = (tile + tile0) * 128
        half = tile & 1
        src = pl.multiple_of(src_ref[tab + e] + j * piece_rows, ROW_CHUNKS)
        dst = pl.multiple_of(e * (RUN_ROWS * ROW_CHUNKS) + j * piece_rows, ROW_CHUNKS)
        return pltpu.make_async_copy(ys_ref.at[pl.ds(src, piece_rows)],
                                     buf_ref.at[half, pl.ds(dst, piece_rows)], sem.at[half])

    def fetch(tile):
        for e in range(N_EXPERTS):
            def start(j, carry, e=e):
                piece(tile, e, j).start(priority=e % 2)
                return carry
            lax.fori_loop(0, npc_ref[(tile + tile0) * 128 + e], start, 0)

    @pl.when(i == 0)
    def _():
        fetch(i)

    @pl.when(i + 1 < pl.num_programs(0))
    def _():
        fetch(i + 1)

    for e in range(N_EXPERTS):
        def wait(j, carry, e=e):
            piece(i, e, j).wait()
            return carry
        lax.fori_loop(0, npc_ref[(i + tile0) * 128 + e], wait, 0)

    base = i * (nt * TOP_K)
    runs = buf_ref.at[i & 1]
    for t in range(nt):
        acc = None
        for k in range(TOP_K):
            row = pl.multiple_of(pos_ref[base + t * TOP_K + k], ROW_CHUNKS)
            term = runs[pl.ds(row, ROW_CHUNKS), :] * prep_ref[t * TOP_K + k:t * TOP_K + k + 1, :]
            acc = term if acc is None else acc + term
        moe_ref[t * ROW_CHUNKS:(t + 1) * ROW_CHUNKS, :] = acc
    h3 = h2_ref[...] + _load_row_tiles(moe_ref, nt)
    y_ref[...] = _rms(h3, nf_ref[...])


def _combine(pos_flat, src_tab, npc_tab, tile0, ys, h2, prep, nf):
    n = h2.shape[0]
    row = pl.BlockSpec((COMBINE_TILE, D), lambda i, *_: (i, 0))
    return pl.pallas_call(
        functools.partial(_combine_kernel, tile0),
        grid_spec=pltpu.PrefetchScalarGridSpec(
            num_scalar_prefetch=3,
            grid=(n // COMBINE_TILE,),
            in_specs=[pl.BlockSpec(memory_space=pl.ANY), row,
                      pl.BlockSpec((COMBINE_TILE * TOP_K, 128), lambda i, *_: (i, 0)),
                      pl.BlockSpec((1, D), lambda i, *_: (0, 0))],
            out_specs=row,
            scratch_shapes=[pltpu.VMEM((2, N_EXPERTS * RUN_ROWS * ROW_CHUNKS, 128), F32),
                            pltpu.VMEM((COMBINE_TILE * ROW_CHUNKS, 128), F32),
                            pltpu.SemaphoreType.DMA((2,))],
        ),
        out_shape=jax.ShapeDtypeStruct((n, D), F32),
        compiler_params=_cparams("arbitrary"),
        name="moe_combine",
    )(pos_flat, src_tab, npc_tab, ys, h2, prep, nf)


def _rope_tables(pos):
    half = DH // 2
    inv = np.power(ROPE_BASE, -np.arange(half, dtype=np.float64) / half)
    ang = pos.astype(np.float64)[:, None] * inv[None, :]
    cos = np.concatenate([np.cos(ang), np.cos(ang)], axis=1)
    sin = np.concatenate([-np.sin(ang), np.sin(ang)], axis=1)
    return jnp.asarray(cos, F32), jnp.asarray(sin, F32)


def _retention_tables(c, reps):
    log_g = np.log1p(-np.exp2(-5.0 - np.arange(HEADS, dtype=np.float64)))
    idx = np.arange(c, dtype=np.float64)
    rel = idx[:, None] - idx[None, :]
    dmat = np.where(rel >= 0, np.exp(log_g[:, None, None] * np.maximum(rel, 0.0)), 0.0)
    big = np.zeros((HEADS, c * reps, c * reps))
    for r in range(reps):
        big[:, r * c:(r + 1) * c, r * c:(r + 1) * c] = dmat
    q_in = np.tile(np.exp(log_g[:, None] * (idx + 1.0)), (1, reps))
    k_out = np.tile(np.exp(log_g[:, None] * (c - 1.0 - idx)), (1, reps))
    g_all = np.exp(log_g * c)
    bc = lambda a: jnp.asarray(np.broadcast_to(a[..., None], a.shape + (DH,)), F32)
    return jnp.asarray(big, F32), bc(q_in), bc(k_out), bc(g_all[:, None])


def _block_tri(c, reps, strict=False):
    idx = np.arange(c * reps)
    same = (idx[:, None] // c) == (idx[None, :] // c)
    low = idx[:, None] > idx[None, :] if strict else idx[:, None] >= idx[None, :]
    return jnp.asarray(same & low, BF16), jnp.asarray(same, BF16)


def _expert_schedule(counts, n_rows):
    n_tiles = n_rows // M_TILE
    n_items = n_tiles + N_EXPERTS - 1
    ends = jnp.cumsum(counts)
    starts = ends - counts
    first_tile = starts // M_TILE
    tiles_e = jnp.where(counts > 0, (ends - 1) // M_TILE - first_tile + 1, 0)
    item_end = jnp.cumsum(tiles_e)
    item_start = item_end - tiles_e
    total = item_end[-1]
    i = jnp.arange(n_items, dtype=jnp.int32)
    live = i < total
    ic = jnp.minimum(i, total - 1)
    e = jnp.sum((item_end[None, :] <= ic[:, None]).astype(jnp.int32), axis=1)
    e = jnp.minimum(e, N_EXPERTS - 1)
    hot = e[:, None] == jnp.arange(N_EXPERTS, dtype=jnp.int32)[None, :]
    at_e = lambda table: jnp.sum(jnp.where(hot, table[None, :], 0), axis=1)
    tile = (at_e(first_tile) + ic - at_e(item_start)).astype(jnp.int32)
    lo = jnp.where(live, jnp.maximum(at_e(starts), tile * M_TILE), 0).astype(jnp.int32)
    hi = jnp.where(live, jnp.minimum(at_e(ends), (tile + 1) * M_TILE), 0).astype(jnp.int32)
    shifted = lambda a: jnp.concatenate([jnp.full((1,), -1, jnp.int32), a[:-1]])
    first = (live & (tile != shifted(tile))).astype(jnp.int32)
    fresh = (live & (e != shifted(e))).astype(jnp.int32)
    return tile, e, lo, hi, first, fresh


def kernel(x_prompt, x_sample, state_hgrn, state_ret, cache_mem_k, cache_mem_v, mem_prompt, norm_mix, w_in, hgrn_lb_logits, hgrn_out_norm, ret_out_norm, w_out, norm_x, norm_mem, w_xq, w_mk, w_mv, w_xo, norm_ffn, w_router, b_router, w_gate_up, b_gate_up, w_down, b_down, norm_final):
    bp, tp, _ = x_prompt.shape
    bs, ts, _ = x_sample.shape
    n_p, n_s = bp * tp, bs * ts
    past_len = 16384

    row = lambda a: a.reshape(1, -1).astype(F32)
    win = w_in[0].astype(BF16)
    wout = w_out[0].astype(BF16)
    wxq = w_xq[0].astype(BF16)
    wxo = w_xo[0].astype(BF16)
    wmk = w_mk[0].astype(BF16)
    wmv = w_mv[0].astype(BF16)
    wr = jnp.pad(w_router[0], ((0, 0), (0, 128 - N_EXPERTS))).astype(BF16)
    br = jnp.pad(b_router[0], (0, 128 - N_EXPERTS)).reshape(1, 128).astype(F32)
    bg = b_gate_up[0][:, 0::2].reshape(N_EXPERTS, 1, D)
    bl = b_gate_up[0][:, 1::2].reshape(N_EXPERTS, 1, D)
    bd = b_down[0].reshape(N_EXPERTS, 1, D)
    pw = 256
    perm_np = np.zeros((pw, pw), np.float32)
    perm_np[np.arange(pw), (np.arange(pw) % 2) * (pw // 2) + np.arange(pw) // 2] = 1.0
    perm = jnp.asarray(perm_np, BF16)
    nm, nx, nmem, nffn, nfin = row(norm_mix[0]), row(norm_x[0]), row(norm_mem[0]), row(norm_ffn[0]), row(norm_final)
    gh, gr = row(hgrn_out_norm[0]), row(ret_out_norm[0])
    lbl = hgrn_lb_logits.astype(F32)

    cos_p, sin_p = _rope_tables(np.arange(tp))
    dmat_p, qin_p, kout_p, gall_p = _retention_tables(RET_CHUNK, 1)
    tri_p, _ = _block_tri(HGRN_CHUNK, 1)
    mixed_p, sh_p, sr_p = _mixer_prompt(x_prompt, nm, win, lbl, gh, gr, cos_p, sin_p,
                                        dmat_p, qin_p, kout_p, gall_p, tri_p)
    cos_s, sin_s = _rope_tables(np.tile(past_len + np.arange(ts), S_BATCH))
    dmat_s, qin_s, kout_s, gall_s = _retention_tables(ts, S_BATCH)
    tri_s, ones_s = _block_tri(ts, S_BATCH)
    mixed_s, sh_s, sr_s = _mixer_sample(x_sample, nm, win, lbl, gh, gr, cos_s, sin_s,
                                        dmat_s, qin_s, kout_s, gall_s, tri_s, ones_s,
                                        state_hgrn[0], state_ret[0])

    mk_p, mv_p = _memkv(mem_prompt, nmem, wmk, wmv)
    h2_p, hn_p, lg_p = _xattn_prompt(x_prompt, mixed_p, mk_p, mv_p, wout, nx, wxq, wxo, nffn, wr, br)
    h1_s, q_s = _pre_attn(x_sample.reshape(n_s, D), mixed_s, wout, nx, wxq, F32)
    o_s = _attn_sample(q_s, cache_mem_k[0], cache_mem_v[0], ts)
    h2_s, hn_s, lg_s = _post_attn(h1_s, o_s, wxo, nffn, wr, br)

    stril, _ = _block_tri(R_TILE, 1, strict=True)
    idx, prep, rank, cst, sub, counts = _router(jnp.concatenate([lg_p, lg_s], axis=0), stril)
    counts = counts[0, :N_EXPERTS].astype(jnp.int32)
    offs = jnp.pad(jnp.cumsum(counts) - counts, (0, 128 - N_EXPERTS)).reshape(1, 128).astype(F32)
    n_rows = (n_p + n_s) * TOP_K
    dest, pos, src_tab, npc_tab = _dest(idx, rank, offs, cst, sub, n_rows)
    dest = dest[:, :TOP_K].reshape(-1)
    pos = pos[:, :TOP_K].reshape(-1)
    src_tab, npc_tab = src_tab.reshape(-1), npc_tab.reshape(-1)
    cut = n_p * TOP_K

    xs = _dispatch(dest[:cut], dest[cut:], hn_p, hn_s)
    ys = _experts(_expert_schedule(counts, n_rows), xs, w_gate_up[0], bg, bl, w_down[0], bd, perm)
    y_p = _combine(pos[:cut], src_tab, npc_tab, 0, ys, h2_p, prep[:cut], nfin)
    y_s = _combine(pos[cut:], src_tab, npc_tab, n_p // COMBINE_TILE, ys, h2_s, prep[cut:], nfin)

    return (y_p.reshape(bp, tp, D), y_s.reshape(bs, ts, D),
            sh_p[None], sr_p[None], mk_p[None], mv_p[None], sh_s[None], sr_s[None])
```

```python
import functools

import numpy as np
import jax
import jax.numpy as jnp
from jax import lax
from jax.experimental import pallas as pl
from jax.experimental.pallas import tpu as pltpu

F32 = jnp.float32
BF16 = jnp.bfloat16

D = 1024
HEADS = 4
DH = 128
GW = HEADS * DH
IN_COLS = 8 * GW
ROPE_BASE = 10000.0
N_MEM = 256
X_HEADS = 4
X_DH = D // X_HEADS
N_EXPERTS = 32
TOP_K = 4
LIMIT = 7.0
ALPHA = 1.702
EPS = 1e-6
EXP_CLAMP = 80.0

HGRN_CHUNK = 64
RET_CHUNK = 128
T_TILE = 512
S_BATCH = 8
ROW_TILE = 512
Q_TILE = 512
A_BATCH = 8
R_TILE = 1024
M_TILE = 512
DISPATCH_TILE = 256
COMBINE_TILE = 128
RUN_PIECE = 16
RUN_ROWS = COMBINE_TILE + RUN_PIECE
VMEM_LIMIT = 56 * 1024 * 1024


def _cparams(*sem):
    return pltpu.CompilerParams(dimension_semantics=sem, vmem_limit_bytes=VMEM_LIMIT)


def _dot(a, b):
    return jnp.dot(a.astype(BF16), b.astype(BF16), preferred_element_type=F32)


def _dot_nt(a, b):
    return lax.dot_general(a.astype(BF16), b.astype(BF16), (((1,), (1,)), ((), ())),
                           preferred_element_type=F32)


def _dot_tn(a, b):
    return lax.dot_general(a, b, (((0,), (0,)), ((), ())), preferred_element_type=F32)


def _rms(x, g):
    return x * lax.rsqrt(jnp.mean(x * x, axis=-1, keepdims=True) + EPS) * g


ROW_CHUNKS = D // 128


def _load_row_tiles(ref, n):
    return jnp.concatenate([ref[pl.ds(c, n, stride=ROW_CHUNKS), :] for c in range(ROW_CHUNKS)], axis=1)


def _store_row_tiles(ref, x):
    n = x.shape[0]
    for c in range(ROW_CHUNKS):
        ref[pl.ds(c, n, stride=ROW_CHUNKS), :] = x[:, c * 128:(c + 1) * 128]


def _cumsum_rows(tri, v):
    hi = v.astype(BF16)
    lo = (v - hi.astype(F32)).astype(BF16)
    return (jnp.dot(tri, hi, preferred_element_type=F32)
            + jnp.dot(tri, lo, preferred_element_type=F32))


def _lower_bound(lbl_ref):
    l = lbl_ref[...]
    m = jnp.max(l, axis=0, keepdims=True)
    e = jnp.exp(l - m)
    return e[0:1, :] / jnp.sum(e, axis=0, keepdims=True)


def _row_to_col(row):
    n = row.shape[1]
    eye = lax.broadcasted_iota(jnp.int32, (n, n), 0) == lax.broadcasted_iota(jnp.int32, (n, n), 1)
    return jnp.sum(jnp.where(eye, jnp.broadcast_to(row, (n, n)), 0.0), axis=1, keepdims=True)


def _rotary(x, cs, sn):
    return x * cs + pltpu.roll(x, DH // 2, 1) * sn


def _hgrn_out(o, gh, gate):
    on = o * lax.rsqrt(jnp.mean(o * o, axis=-1, keepdims=True) + EPS) * gh
    return on * jax.nn.sigmoid(gate)


def _ret_out(o, gr, gate):
    mu = jnp.mean(o, axis=-1, keepdims=True)
    xc = o - mu
    var = jnp.mean(xc * xc, axis=-1, keepdims=True)
    return xc * lax.rsqrt(var + EPS) * gr * (gate * jax.nn.sigmoid(gate))


def _mixer_prompt_kernel(x_ref, nm_ref, win_ref, lbl_ref, gh_ref, gr_ref, cos_ref, sin_ref,
                         dmat_ref, qin_ref, kout_ref, gall_ref, tri_ref,
                         mixed_ref, sh_ref, sr_ref, proj_ref, f_ref, lc_ref):
    @pl.when(pl.program_id(1) == 0)
    def _():
        sh_ref[...] = jnp.zeros_like(sh_ref)
        sr_ref[...] = jnp.zeros_like(sr_ref)

    xn = _rms(x_ref[0], nm_ref[...]).astype(BF16)
    proj_ref[...] = jnp.dot(xn, win_ref[...], preferred_element_type=F32)
    lb = _lower_bound(lbl_ref)
    gh = gh_ref[...]
    gr = gr_ref[...]
    c = HGRN_CHUNK
    causal = (lax.broadcasted_iota(jnp.int32, (c, c), 0) >= lax.broadcasted_iota(jnp.int32, (c, c), 1))

    def gate_step(ci, worst):
        rows = pl.ds(pl.multiple_of(ci * c, c), c)
        f = lb + (1.0 - lb) * jax.nn.sigmoid(proj_ref[rows, GW:2 * GW])
        lc = _cumsum_rows(tri_ref[...], jnp.log(f))
        f_ref[rows, :] = f
        lc_ref[rows, :] = lc
        return jnp.maximum(worst, jnp.max(jnp.abs(lc - lc[c // 2 - 1:c // 2, :])))

    worst = lax.fori_loop(0, T_TILE // c, gate_step, jnp.float32(0.0), unroll=True)

    def scores_factored(hq, kk, lc, h, r0):
        sl = slice(h * DH, (h + 1) * DH)
        mid = lc[c // 2 - 1:c // 2, sl]
        qd = hq[:, sl] * jnp.exp(lc[:, sl] - mid)
        kd = kk[:, sl] * jnp.exp(mid - lc[:, sl])
        return jnp.where(causal, _dot_nt(qd, kd), 0.0)

    def scores_termwise(hq, kk, lc, h, r0):
        sl = slice(h * DH, (h + 1) * DH)
        col_id = lax.broadcasted_iota(jnp.int32, (c, c), 1)

        sub_id = lax.broadcasted_iota(jnp.int32, (8, DH), 0)

        def row_of(ref, s):
            grp = ref[pl.ds(pl.multiple_of(r0 + lax.shift_left(lax.shift_right_logical(s, 3), 3), 8), 8), sl]
            return jnp.sum(jnp.where(sub_id == (s & 7), grp, 0.0), axis=0, keepdims=True)

        def body(s, a):
            lcs = row_of(lc_ref, s)
            ks_row = 1.0 - row_of(f_ref, s)
            w = jnp.exp(jnp.minimum(lc[:, sl] - lcs, 0.0))
            col = jnp.sum(hq[:, sl] * ks_row * w, axis=1, keepdims=True)
            return a + jnp.where(col_id == s, col, 0.0)

        return jnp.where(causal, lax.fori_loop(0, c, body, jnp.zeros((c, c), F32)), 0.0)

    def hgrn_step(scores, ci, carry):
        r0 = pl.multiple_of(ci * c, c)
        rows = pl.ds(r0, c)
        hq = proj_ref[rows, 0:GW] * (DH ** -0.5)
        kk = 1.0 - f_ref[rows, :]
        lc = lc_ref[rows, :]
        last = lc[c - 1:c, :]
        qs = hq * jnp.exp(lc)
        ks = kk * jnp.exp(last - lc)
        dl = jnp.exp(last)
        for h in range(HEADS):
            sl = slice(h * DH, (h + 1) * DH)
            v = proj_ref[rows, 2 * GW + h * DH:2 * GW + (h + 1) * DH].astype(BF16)
            a = scores(hq, kk, lc, h, r0)
            s_old = sh_ref[0, h]
            o = _dot(a, v) + _dot(qs[:, sl], s_old)
            sh_ref[0, h] = _row_to_col(dl[:, sl]) * s_old + _dot_tn(ks[:, sl].astype(BF16), v)
            gate = proj_ref[rows, 3 * GW + h * DH:3 * GW + (h + 1) * DH]
            mixed_ref[0, rows, sl] = _hgrn_out(o, gh, gate).astype(BF16)
        return carry

    @pl.when(worst <= EXP_CLAMP)
    def _():
        lax.fori_loop(0, T_TILE // c, functools.partial(hgrn_step, scores_factored), 0, unroll=True)

    @pl.when(jnp.logical_not(worst <= EXP_CLAMP))
    def _():
        lax.fori_loop(0, T_TILE // c, functools.partial(hgrn_step, scores_termwise), 0)

    cr = RET_CHUNK

    def ret_step(ci, carry):
        rows = pl.ds(pl.multiple_of(ci * cr, cr), cr)
        cs = cos_ref[rows, :]
        sn = sin_ref[rows, :]
        for h in range(HEADS):
            base = 4 * GW + h * DH
            q = _rotary(proj_ref[rows, base:base + DH], cs, sn)
            k = _rotary(proj_ref[rows, base + GW:base + GW + DH], cs, sn) * (DH ** -0.5)
            v = proj_ref[rows, base + 2 * GW:base + 2 * GW + DH].astype(BF16)
            sc = _dot_nt(q, k) * dmat_ref[h]
            s_old = sr_ref[0, h]
            o = _dot(sc, v) + _dot(q * qin_ref[h], s_old)
            sr_ref[0, h] = gall_ref[h] * s_old + _dot_tn((k * kout_ref[h]).astype(BF16), v)
            gate = proj_ref[rows, base + 3 * GW:base + 3 * GW + DH]
            mixed_ref[0, rows, GW + h * DH:GW + (h + 1) * DH] = _ret_out(o, gr, gate).astype(BF16)
        return carry

    lax.fori_loop(0, T_TILE // cr, ret_step, 0, unroll=True)


def _mixer_prompt(x, nm, win, lbl, gh, gr, cos, sin, dmat, qin, kout, gall, tri):
    b, t, _ = x.shape
    full = lambda shape: pl.BlockSpec(shape, lambda i, j: (0,) * len(shape))
    state = pl.BlockSpec((1, HEADS, DH, DH), lambda i, j: (i, 0, 0, 0))
    return pl.pallas_call(
        _mixer_prompt_kernel,
        grid=(b, t // T_TILE),
        in_specs=[
            pl.BlockSpec((1, T_TILE, D), lambda i, j: (i, j, 0)),
            full((1, D)), full((D, IN_COLS)), full(lbl.shape), full((1, DH)), full((1, DH)),
            pl.BlockSpec((T_TILE, DH), lambda i, j: (j, 0)),
            pl.BlockSpec((T_TILE, DH), lambda i, j: (j, 0)),
            full(dmat.shape), full(qin.shape), full(kout.shape), full(gall.shape), full(tri.shape),
        ],
        out_specs=[pl.BlockSpec((1, T_TILE, D), lambda i, j: (i, j, 0)), state, state],
        out_shape=[jax.ShapeDtypeStruct((b, t, D), BF16),
                   jax.ShapeDtypeStruct((b, HEADS, DH, DH), F32),
                   jax.ShapeDtypeStruct((b, HEADS, DH, DH), F32)],
        scratch_shapes=[pltpu.VMEM((T_TILE, IN_COLS), F32), pltpu.VMEM((T_TILE, GW), F32),
                        pltpu.VMEM((T_TILE, GW), F32)],
        compiler_params=_cparams("parallel", "arbitrary"),
        name="mixer_prompt",
    )(x, nm, win, lbl, gh, gr, cos, sin, dmat, qin, kout, gall, tri)


def _mixer_sample_kernel(x_ref, nm_ref, win_ref, lbl_ref, gh_ref, gr_ref, cos_ref, sin_ref,
                         dmat_ref, qin_ref, kout_ref, gall_ref, tri_ref, ones_ref,
                         shin_ref, srin_ref, mixed_ref, sh_ref, sr_ref, proj_ref):
    nb, ts, _ = x_ref.shape
    rows_n = nb * ts
    xn = _rms(x_ref[...].reshape(rows_n, D), nm_ref[...]).astype(BF16)
    proj_ref[...] = jnp.dot(xn, win_ref[...], preferred_element_type=F32)
    lb = _lower_bound(lbl_ref)
    gh = gh_ref[...]
    gr = gr_ref[...]
    hq = proj_ref[:, 0:GW] * (DH ** -0.5)
    f = lb + (1.0 - lb) * jax.nn.sigmoid(proj_ref[:, GW:2 * GW])
    kk = 1.0 - f
    lf = jnp.log(f)
    lc = _cumsum_rows(tri_ref[...], lf)
    last = _cumsum_rows(ones_ref[...], lf)
    qd = hq * jnp.exp(lc)
    ks = kk * jnp.exp(last - lc)
    dl = jnp.exp(last)
    t_pos = lax.broadcasted_iota(jnp.int32, (nb, ts, 1), 1)

    def intra_termwise(h):
        sl = slice(h * DH, (h + 1) * DH)
        q3 = hq[:, sl].reshape(nb, ts, DH)
        k3 = kk[:, sl].reshape(nb, ts, DH)
        lc3 = lc[:, sl].reshape(nb, ts, DH)
        v3 = proj_ref[:, 2 * GW + h * DH:2 * GW + (h + 1) * DH].reshape(nb, ts, DH)
        o3 = jnp.zeros((nb, ts, DH), F32)
        for s in range(ts):
            w = jnp.exp(jnp.minimum(lc3 - lc3[:, s:s + 1, :], 0.0))
            score = jnp.sum(q3 * k3[:, s:s + 1, :] * w, axis=-1, keepdims=True)
            o3 = o3 + jnp.where(t_pos >= s, score, 0.0) * v3[:, s:s + 1, :]
        return o3.reshape(rows_n, DH)

    for h in range(HEADS):
        sl = slice(h * DH, (h + 1) * DH)
        v = proj_ref[:, 2 * GW + h * DH:2 * GW + (h + 1) * DH]
        o_intra = intra_termwise(h)
        gate = proj_ref[:, 3 * GW + h * DH:3 * GW + (h + 1) * DH]
        for b in range(nb):
            r = slice(b * ts, (b + 1) * ts)
            s_old = shin_ref[b, h]
            o = o_intra[r] + _dot(qd[r, sl], s_old)
            sh_ref[b, h] = (_row_to_col(dl[b * ts:b * ts + 1, sl]) * s_old
                            + _dot_tn(ks[r, sl], v[r]))
            mixed_ref[r, sl] = _hgrn_out(o, gh, gate[r])

    cs = cos_ref[...]
    sn = sin_ref[...]
    for h in range(HEADS):
        base = 4 * GW + h * DH
        q = _rotary(proj_ref[:, base:base + DH], cs, sn)
        k = _rotary(proj_ref[:, base + GW:base + GW + DH], cs, sn) * (DH ** -0.5)
        v = proj_ref[:, base + 2 * GW:base + 2 * GW + DH]
        sc = _dot_nt(q, k) * dmat_ref[h]
        o_intra = _dot(sc, v)
        qi = q * qin_ref[h]
        ko = k * kout_ref[h]
        gate = proj_ref[:, base + 3 * GW:base + 3 * GW + DH]
        for b in range(nb):
            r = slice(b * ts, (b + 1) * ts)
            s_old = srin_ref[b, h]
            o = o_intra[r] + _dot(qi[r], s_old)
            sr_ref[b, h] = gall_ref[h] * s_old + _dot_tn(ko[r], v[r])
            mixed_ref[r, GW + h * DH:GW + (h + 1) * DH] = _ret_out(o, gr, gate[r])


def _mixer_sample(x, nm, win, lbl, gh, gr, cos, sin, dmat, qin, kout, gall, tri, ones, sh, sr):
    b, t, _ = x.shape
    nb = S_BATCH
    full = lambda shape: pl.BlockSpec(shape, lambda i: (0,) * len(shape))
    state = pl.BlockSpec((nb, HEADS, DH, DH), lambda i: (i, 0, 0, 0))
    return pl.pallas_call(
        _mixer_sample_kernel,
        grid=(b // nb,),
        in_specs=[
            pl.BlockSpec((nb, t, D), lambda i: (i, 0, 0)),
            full((1, D)), full((D, IN_COLS)), full(lbl.shape), full((1, DH)), full((1, DH)),
            full(cos.shape), full(sin.shape),
            full(dmat.shape), full(qin.shape), full(kout.shape), full(gall.shape),
            full(tri.shape), full(ones.shape), state, state,
        ],
        out_specs=[pl.BlockSpec((nb * t, D), lambda i: (i, 0)), state, state],
        out_shape=[jax.ShapeDtypeStruct((b * t, D), F32),
                   jax.ShapeDtypeStruct(sh.shape, F32),
                   jax.ShapeDtypeStruct(sr.shape, F32)],
        scratch_shapes=[pltpu.VMEM((nb * t, IN_COLS), F32)],
        compiler_params=_cparams("parallel"),
        name="mixer_sample",
    )(x, nm, win, lbl, gh, gr, cos, sin, dmat, qin, kout, gall, tri, ones, sh, sr)


def _memkv_kernel(mem_ref, g_ref, wk_ref, wv_ref, k_ref, v_ref):
    mn = _rms(mem_ref[0], g_ref[...]).astype(BF16)
    k = jnp.dot(mn, wk_ref[...], preferred_element_type=F32)
    v = jnp.dot(mn, wv_ref[...], preferred_element_type=F32)
    for h in range(X_HEADS):
        k_ref[0, h] = k[:, h * X_DH:(h + 1) * X_DH]
        v_ref[0, h] = v[:, h * X_DH:(h + 1) * X_DH]


def _memkv(mem, g, wk, wv):
    b = mem.shape[0]
    full = lambda shape: pl.BlockSpec(shape, lambda i: (0,) * len(shape))
    kv = pl.BlockSpec((1, X_HEADS, N_MEM, X_DH), lambda i: (i, 0, 0, 0))
    shape = jax.ShapeDtypeStruct((b, X_HEADS, N_MEM, X_DH), F32)
    return pl.pallas_call(
        _memkv_kernel,
        grid=(b,),
        in_specs=[pl.BlockSpec((1, N_MEM, D), lambda i: (i, 0, 0)), full((1, D)), full((D, D)), full((D, D))],
        out_specs=[kv, kv],
        out_shape=[shape, shape],
        compiler_params=_cparams("parallel"),
        name="mem_kv",
    )(mem, g, wk, wv)


def _pre_attn_kernel(x_ref, mixed_ref, wout_ref, nx_ref, wxq_ref, h1_ref, q_ref):
    h1 = x_ref[...] + _dot(mixed_ref[...], wout_ref[...])
    h1_ref[...] = h1
    hn = _rms(h1, nx_ref[...])
    q_ref[...] = _dot(hn, wxq_ref[...]).astype(q_ref.dtype)


def _pre_attn(x, mixed, wout, nx, wxq, q_dtype):
    n = x.shape[0]
    row = pl.BlockSpec((ROW_TILE, D), lambda i: (i, 0))
    full = lambda shape: pl.BlockSpec(shape, lambda i: (0,) * len(shape))
    return pl.pallas_call(
        _pre_attn_kernel,
        grid=(n // ROW_TILE,),
        in_specs=[row, row, full((D, D)), full((1, D)), full((D, D))],
        out_specs=[row, row],
        out_shape=[jax.ShapeDtypeStruct((n, D), F32), jax.ShapeDtypeStruct((n, D), q_dtype)],
        compiler_params=_cparams("parallel"),
        name="pre_attn",
    )(x, mixed, wout, nx, wxq)


def _post_attn_kernel(h1_ref, o_ref, wxo_ref, nf_ref, wr_ref, br_ref, h2_ref, hn_ref, lg_ref):
    h2 = h1_ref[...] + _dot(o_ref[...], wxo_ref[...])
    h2_ref[...] = h2
    hn = _rms(h2, nf_ref[...])
    _store_row_tiles(hn_ref, hn)
    lg_ref[...] = _dot(hn, wr_ref[...]) + br_ref[...]


def _post_attn(h1, o, wxo, nf, wr, br):
    n = h1.shape[0]
    row = pl.BlockSpec((ROW_TILE, D), lambda i: (i, 0))
    full = lambda shape: pl.BlockSpec(shape, lambda i: (0,) * len(shape))
    return pl.pallas_call(
        _post_attn_kernel,
        grid=(n // ROW_TILE,),
        in_specs=[row, row, full((D, D)), full((1, D)), full((D, 128)), full((1, 128))],
        out_specs=[row, pl.BlockSpec((ROW_TILE * ROW_CHUNKS, 128), lambda i: (i, 0)),
                   pl.BlockSpec((ROW_TILE, 128), lambda i: (i, 0))],
        out_shape=[jax.ShapeDtypeStruct((n, D), F32), jax.ShapeDtypeStruct((n * ROW_CHUNKS, 128), F32),
                   jax.ShapeDtypeStruct((n, 128), F32)],
        compiler_params=_cparams("parallel"),
        name="post_attn",
    )(h1, o, wxo, nf, wr, br)


def _softmax_rows(s):
    m = jnp.max(s, axis=-1, keepdims=True)
    e = jnp.exp(s - m)
    return e / jnp.sum(e, axis=-1, keepdims=True)


def _xattn_prompt_kernel(x_ref, mixed_ref, k_ref, v_ref, wout_ref, nx_ref, wxq_ref, wxo_ref,
                         nf_ref, wr_ref, br_ref, h2_ref, hn_ref, lg_ref, o_scr):
    h1 = x_ref[0] + _dot(mixed_ref[0], wout_ref[...])
    q = _dot(_rms(h1, nx_ref[...]), wxq_ref[...]).astype(BF16)
    for h in range(X_HEADS):
        sl = slice(h * X_DH, (h + 1) * X_DH)
        s = _dot_nt(q[:, sl], k_ref[0, h]) * (X_DH ** -0.5)
        o_scr[:, sl] = _dot(_softmax_rows(s), v_ref[0, h]).astype(BF16)
    h2 = h1 + _dot(o_scr[...], wxo_ref[...])
    h2_ref[...] = h2
    hn = _rms(h2, nf_ref[...])
    _store_row_tiles(hn_ref, hn)
    lg_ref[...] = _dot(hn, wr_ref[...]) + br_ref[...]


def _xattn_prompt(x, mixed, mk, mv, wout, nx, wxq, wxo, nf, wr, br):
    b, seq, _ = x.shape
    per_seq = seq // Q_TILE
    full = lambda shape: pl.BlockSpec(shape, lambda i, j: (0,) * len(shape))
    tok = pl.BlockSpec((1, Q_TILE, D), lambda i, j: (i, j, 0))
    kv = pl.BlockSpec((1, X_HEADS, N_MEM, X_DH), lambda i, j: (i, 0, 0, 0))
    row = pl.BlockSpec((Q_TILE, D), lambda i, j: (i * per_seq + j, 0))
    n = b * seq
    return pl.pallas_call(
        _xattn_prompt_kernel,
        grid=(b, per_seq),
        in_specs=[tok, tok, kv, kv, full((D, D)), full((1, D)), full((D, D)), full((D, D)),
                  full((1, D)), full((D, 128)), full((1, 128))],
        out_specs=[row, pl.BlockSpec((Q_TILE * ROW_CHUNKS, 128), lambda i, j: (i * per_seq + j, 0)),
                   pl.BlockSpec((Q_TILE, 128), lambda i, j: (i * per_seq + j, 0))],
        out_shape=[jax.ShapeDtypeStruct((n, D), F32), jax.ShapeDtypeStruct((n * ROW_CHUNKS, 128), F32),
                   jax.ShapeDtypeStruct((n, 128), F32)],
        scratch_shapes=[pltpu.VMEM((Q_TILE, D), BF16)],
        compiler_params=_cparams("parallel", "arbitrary"),
        name="xattn_prompt",
    )(x, mixed, mk, mv, wout, nx, wxq, wxo, nf, wr, br)


def _attn_sample_kernel(q_ref, k_ref, v_ref, o_ref):
    nb = k_ref.shape[0]
    ts = q_ref.shape[0] // nb
    units = [(b, h) for b in range(nb) for h in range(X_HEADS)]
    s = jnp.concatenate(
        [_dot_nt(q_ref[b * ts:(b + 1) * ts, h * X_DH:(h + 1) * X_DH], k_ref[b, h]) for b, h in units], axis=0)
    p = _softmax_rows(s * (X_DH ** -0.5))
    for i, (b, h) in enumerate(units):
        o_ref[b * ts:(b + 1) * ts, h * X_DH:(h + 1) * X_DH] = _dot(p[i * ts:(i + 1) * ts], v_ref[b, h])


def _attn_sample(q, ck, cv, ts):
    n = q.shape[0]
    nb = A_BATCH
    kv = pl.BlockSpec((nb, X_HEADS, N_MEM, X_DH), lambda i: (i, 0, 0, 0))
    qs = pl.BlockSpec((nb * ts, D), lambda i: (i, 0))
    return pl.pallas_call(
        _attn_sample_kernel,
        grid=(n // (nb * ts),),
        in_specs=[qs, kv, kv],
        out_specs=qs,
        out_shape=jax.ShapeDtypeStruct((n, D), F32),
        compiler_params=_cparams("parallel"),
        name="attn_sample",
    )(q, ck, cv)


def _router_kernel(lg_ref, stril_ref, idx_ref, prep_ref, rank_ref, cst_ref, sub_ref, cnt_ref, carry_ref):
    @pl.when(pl.program_id(0) == 0)
    def _():
        carry_ref[...] = jnp.zeros_like(carry_ref)

    n = lg_ref.shape[0]
    lane = lax.broadcasted_iota(jnp.int32, (n, 128), 1)
    lane_f = lane.astype(F32)
    l = jnp.where(lane < N_EXPERTS, lg_ref[...], -jnp.inf)
    tops, idxs, hots = [], [], []
    for _ in range(TOP_K):
        m = jnp.max(l, axis=1, keepdims=True)
        idx = jnp.min(jnp.where(l == m, lane_f, 128.0), axis=1, keepdims=True)
        hot = lane_f == idx
        l = jnp.where(hot, -jnp.inf, l)
        tops.append(m)
        idxs.append(idx)
        hots.append(hot)
    sel = jnp.where(hots[0] | hots[1] | hots[2] | hots[3], 1.0, 0.0)
    before = jnp.dot(stril_ref[...], sel.astype(BF16), preferred_element_type=F32) + carry_ref[...]
    carry_ref[...] += jnp.sum(sel, axis=0, keepdims=True)
    cnt_ref[...] = carry_ref[...]
    subs = n // COMBINE_TILE
    cst_ref[...] = jnp.concatenate([before[j * COMBINE_TILE:j * COMBINE_TILE + 1, :] for j in range(subs)], axis=0)
    sub_ref[...] = jnp.concatenate(
        [jnp.sum(sel[j * COMBINE_TILE:(j + 1) * COMBINE_TILE, :], axis=0, keepdims=True) for j in range(subs)], axis=0)
    es = [jnp.exp(t - tops[0]) for t in tops]
    den = es[0] + es[1] + es[2] + es[3]
    idx_out = jnp.zeros((n, 128), jnp.int32)
    rank_out = jnp.zeros((n, 128), jnp.int32)
    for k in range(TOP_K):
        rank = jnp.sum(jnp.where(hots[k], before, 0.0), axis=1, keepdims=True).astype(jnp.int32)
        idx_out = jnp.where(lane == k, idxs[k].astype(jnp.int32), idx_out)
        rank_out = jnp.where(lane == k, rank, rank_out)
        prep_ref[pl.ds(k, n, stride=TOP_K), :] = jnp.broadcast_to(es[k] / den, (n, 128))
    idx_ref[...] = idx_out
    rank_ref[...] = rank_out


def _router(logits, stril):
    n = logits.shape[0]
    subs = R_TILE // COMBINE_TILE
    row = pl.BlockSpec((R_TILE, 128), lambda i: (i, 0))
    one = pl.BlockSpec((1, 128), lambda i: (0, 0))
    sub = pl.BlockSpec((subs, 128), lambda i: (i, 0))
    n_sub = n // COMBINE_TILE
    return pl.pallas_call(
        _router_kernel,
        grid=(n // R_TILE,),
        in_specs=[row, pl.BlockSpec((R_TILE, R_TILE), lambda i: (0, 0))],
        out_specs=[row, pl.BlockSpec((R_TILE * TOP_K, 128), lambda i: (i, 0)), row, sub, sub, one],
        out_shape=[jax.ShapeDtypeStruct((n, 128), jnp.int32), jax.ShapeDtypeStruct((n * TOP_K, 128), F32),
                   jax.ShapeDtypeStruct((n, 128), jnp.int32), jax.ShapeDtypeStruct((n_sub, 128), F32),
                   jax.ShapeDtypeStruct((n_sub, 128), F32), jax.ShapeDtypeStruct((1, 128), F32)],
        scratch_shapes=[pltpu.VMEM((1, 128), F32)],
        compiler_params=_cparams("arbitrary"),
        name="router",
    )(logits, stril)


def _dest_kernel(n_rows, idx_ref, rank_ref, offs_ref, cst_ref, sub_ref, dest_ref, pos_ref, src_ref, npc_ref):
    n = idx_ref.shape[0]
    subs = n // COMBINE_TILE
    lane = lax.broadcasted_iota(jnp.int32, (n, 128), 1)
    lane_row = lax.broadcasted_iota(jnp.int32, (subs, 128), 1).astype(F32)
    offs = offs_ref[...]
    before = cst_ref[...]
    start = offs + before
    pieces = jnp.floor((sub_ref[...] + (RUN_PIECE - 1)) * (1.0 / RUN_PIECE))
    shift = jnp.maximum(start + pieces * RUN_PIECE - n_rows, 0.0)
    src_ref[...] = ((start - shift) * ROW_CHUNKS).astype(jnp.int32)
    npc_ref[...] = pieces.astype(jnp.int32)
    slot = lane_row * RUN_ROWS - before + shift
    slot = jnp.concatenate([jnp.broadcast_to(slot[j:j + 1, :], (COMBINE_TILE, 128)) for j in range(subs)], axis=0)
    idx = idx_ref[...]
    d_out = jnp.zeros((n, 128), F32)
    p_out = jnp.zeros((n, 128), F32)
    for k in range(TOP_K):
        hot = lane == idx[:, k:k + 1]
        d_out = jnp.where(lane == k, jnp.sum(jnp.where(hot, offs, 0.0), axis=1, keepdims=True), d_out)
        p_out = jnp.where(lane == k, jnp.sum(jnp.where(hot, slot, 0.0), axis=1, keepdims=True), p_out)
    rank = rank_ref[...]
    dest_ref[...] = (d_out.astype(jnp.int32) + rank) * ROW_CHUNKS
    pos_ref[...] = (p_out.astype(jnp.int32) + rank) * ROW_CHUNKS


def _dest(idx, rank, offs, cst, sub, n_rows):
    n = idx.shape[0]
    subs = R_TILE // COMBINE_TILE
    row = pl.BlockSpec((R_TILE, 128), lambda i: (i, 0))
    tab = pl.BlockSpec((subs, 128), lambda i: (i, 0))
    n_sub = n // COMBINE_TILE
    return pl.pallas_call(
        functools.partial(_dest_kernel, n_rows),
        grid=(n // R_TILE,),
        in_specs=[row, row, pl.BlockSpec((1, 128), lambda i: (0, 0)), tab, tab],
        out_specs=[row, row, tab, tab],
        out_shape=[jax.ShapeDtypeStruct((n, 128), jnp.int32), jax.ShapeDtypeStruct((n, 128), jnp.int32),
                   jax.ShapeDtypeStruct((n_sub, 128), jnp.int32), jax.ShapeDtypeStruct((n_sub, 128), jnp.int32)],
        compiler_params=_cparams("parallel"),
        name="route_dest",
    )(idx, rank, offs, cst, sub)


def _dispatch_kernel(p_tiles, dest_p_ref, dest_s_ref, hn_p_ref, hn_s_ref, xs_ref, sem):
    nt = DISPATCH_TILE
    i = pl.program_id(0)

    def scatter(dest_ref, hn_ref, tile):
        base = tile * (nt * TOP_K)

        def copy(t, k):
            dst = pl.multiple_of(dest_ref[base + t * TOP_K + k], ROW_CHUNKS)
            return pltpu.make_async_copy(hn_ref.at[pl.ds(t * ROW_CHUNKS, ROW_CHUNKS)],
                                         xs_ref.at[pl.ds(dst, ROW_CHUNKS)], sem)

        def start(t, carry):
            for k in range(TOP_K):
                copy(t, k).start(priority=k % 2)
            return carry

        def wait(t, carry):
            for k in range(TOP_K):
                copy(t, k).wait()
            return carry

        for t in range(nt):
            start(t, 0)
        for t in range(nt):
            wait(t, 0)

    @pl.when(i < p_tiles)
    def _():
        scatter(dest_p_ref, hn_p_ref, i)

    @pl.when(i >= p_tiles)
    def _():
        scatter(dest_s_ref, hn_s_ref, i - p_tiles)


def _dispatch(dest_p, dest_s, hn_p, hn_s):
    p_tiles = hn_p.shape[0] // (DISPATCH_TILE * ROW_CHUNKS)
    s_tiles = hn_s.shape[0] // (DISPATCH_TILE * ROW_CHUNKS)
    n_rows = (hn_p.shape[0] + hn_s.shape[0]) * TOP_K
    return pl.pallas_call(
        functools.partial(_dispatch_kernel, p_tiles),
        grid_spec=pltpu.PrefetchScalarGridSpec(
            num_scalar_prefetch=2,
            grid=(p_tiles + s_tiles,),
            in_specs=[pl.BlockSpec((DISPATCH_TILE * ROW_CHUNKS, 128),
                                   lambda i, dp, ds: (jnp.minimum(i, p_tiles - 1), 0)),
                      pl.BlockSpec((DISPATCH_TILE * ROW_CHUNKS, 128),
                                   lambda i, dp, ds: (jnp.maximum(i - p_tiles, 0), 0))],
            out_specs=pl.BlockSpec(memory_space=pl.ANY),
            scratch_shapes=[pltpu.SemaphoreType.DMA],
        ),
        out_shape=jax.ShapeDtypeStruct((n_rows, 128), F32),
        compiler_params=_cparams("arbitrary"),
        name="moe_dispatch",
    )(dest_p, dest_s, hn_p, hn_s)


def _experts_kernel(tile_ref, exp_ref, lo_ref, hi_ref, first_ref, fresh_ref,
                    xs_ref, wgu_ref, bg_ref, bl_ref, wd_ref, bd_ref, perm_ref, ys_ref,
                    wg_s, wl_s, wd_s):
    i = pl.program_id(0)
    lo = lo_ref[i]
    hi = hi_ref[i]

    @pl.when(fresh_ref[i] == 1)
    def _():
        pw = perm_ref.shape[0]
        for c in range(2 * D // pw):
            blk = wgu_ref[0, :, c * pw:(c + 1) * pw].astype(BF16)
            sep = jnp.dot(blk, perm_ref[...], preferred_element_type=F32)
            wg_s[:, c * pw // 2:(c + 1) * pw // 2] = sep[:, :pw // 2].astype(BF16)
            wl_s[:, c * pw // 2:(c + 1) * pw // 2] = sep[:, pw // 2:].astype(BF16)
        wd_s[...] = wd_ref[0].astype(BF16)

    @pl.when(hi > lo)
    def _():
        x = _load_row_tiles(xs_ref, M_TILE).astype(BF16)
        glu = jnp.minimum(jnp.dot(x, wg_s[...], preferred_element_type=F32) + bg_ref[0], LIMIT)
        lin = jnp.clip(jnp.dot(x, wl_s[...], preferred_element_type=F32) + bl_ref[0], -LIMIT, LIMIT)
        hmid = glu * jax.nn.sigmoid(ALPHA * glu) * (lin + 1.0)
        y = _dot(hmid, wd_s[...]) + bd_ref[0]
        row = tile_ref[i] * M_TILE + lax.broadcasted_iota(jnp.int32, (M_TILE, 1), 0)
        mine = (row >= lo) & (row < hi)

        @pl.when(first_ref[i] == 1)
        def _():
            _store_row_tiles(ys_ref, jnp.where(mine, y, 0.0))

        @pl.when(first_ref[i] == 0)
        def _():
            _store_row_tiles(ys_ref, jnp.where(mine, y, _load_row_tiles(ys_ref, M_TILE)))


def _experts(meta, xs, wgu, bg, bl, wd, bd, perm):
    tile, expert, lo, hi, first, fresh = meta
    rows = pl.BlockSpec((M_TILE * ROW_CHUNKS, 128), lambda i, t, e, *_: (t[i], 0))
    bspec = pl.BlockSpec((1, 1, D), lambda i, t, e, *_: (e[i], 0, 0))
    return pl.pallas_call(
        _experts_kernel,
        grid_spec=pltpu.PrefetchScalarGridSpec(
            num_scalar_prefetch=6,
            grid=(tile.shape[0],),
            in_specs=[rows,
                      pl.BlockSpec((1, D, 2 * D), lambda i, t, e, *_: (e[i], 0, 0)), bspec, bspec,
                      pl.BlockSpec((1, D, D), lambda i, t, e, *_: (e[i], 0, 0)), bspec,
                      pl.BlockSpec(perm.shape, lambda i, *_: (0, 0))],
            out_specs=rows,
            scratch_shapes=[pltpu.VMEM((D, D), BF16)] * 3,
        ),
        out_shape=jax.ShapeDtypeStruct(xs.shape, F32),
        compiler_params=_cparams("arbitrary"),
        name="moe_experts",
    )(tile, expert, lo, hi, first, fresh, xs, wgu, bg, bl, wd, bd, perm)


def _combine_kernel(tile0, pos_ref, src_ref, npc_ref, ys_ref, h2_ref, prep_ref, nf_ref, y_ref,
                    buf_ref, moe_ref, sem):
    nt = COMBINE_TILE
    i = pl.program_id(0)
    piece_rows = RUN_PIECE * ROW_CHUNKS

    def piece(tile, e, j):
        tab = (tile + tile0) * 128
        half = tile & 1
        src = pl.multiple_of(src_ref[tab + e] + j * piece_rows, ROW_CHUNKS)
        dst = pl.multiple_of(e * (RUN_ROWS * ROW_CHUNKS) + j * piece_rows, ROW_CHUNKS)
        return pltpu.make_async_copy(ys_ref.at[pl.ds(src, piece_rows)],
                                     buf_ref.at[half, pl.ds(dst, piece_rows)], sem.at[half])

    def fetch(tile):
        for e in range(N_EXPERTS):
            def start(j, carry, e=e):
                piece(tile, e, j).start(priority=e % 2)
                return carry
            lax.fori_loop(0, npc_ref[(tile + tile0) * 128 + e], start, 0)

    @pl.when(i == 0)
    def _():
        fetch(i)

    @pl.when(i + 1 < pl.num_programs(0))
    def _():
        fetch(i + 1)

    for e in range(N_EXPERTS):
        def wait(j, carry, e=e):
            piece(i, e, j).wait()
            return carry
        lax.fori_loop(0, npc_ref[(i + tile0) * 128 + e], wait, 0)

    base = i * (nt * TOP_K)
    runs = buf_ref.at[i & 1]
    for t in range(nt):
        acc = None
        for k in range(TOP_K):
            row = pl.multiple_of(pos_ref[base + t * TOP_K + k], ROW_CHUNKS)
            term = runs[pl.ds(row, ROW_CHUNKS), :] * prep_ref[t * TOP_K + k:t * TOP_K + k + 1, :]
            acc = term if acc is None else acc + term
        moe_ref[t * ROW_CHUNKS:(t + 1) * ROW_CHUNKS, :] = acc
    h3 = h2_ref[...] + _load_row_tiles(moe_ref, nt)
    y_ref[...] = _rms(h3, nf_ref[...])


def _combine(pos_flat, src_tab, npc_tab, tile0, n, ys, h2, prep, nf):
    row = pl.BlockSpec((COMBINE_TILE, D), lambda i, *_: (i, 0))
    return pl.pallas_call(
        functools.partial(_combine_kernel, tile0),
        grid_spec=pltpu.PrefetchScalarGridSpec(
            num_scalar_prefetch=3,
            grid=(n // COMBINE_TILE,),
            in_specs=[pl.BlockSpec(memory_space=pl.ANY), row,
                      pl.BlockSpec((COMBINE_TILE * TOP_K, 128), lambda i, *_: (i + tile0, 0)),
                      pl.BlockSpec((1, D), lambda i, *_: (0, 0))],
            out_specs=row,
            scratch_shapes=[pltpu.VMEM((2, N_EXPERTS * RUN_ROWS * ROW_CHUNKS, 128), F32),
                            pltpu.VMEM((COMBINE_TILE * ROW_CHUNKS, 128), F32),
                            pltpu.SemaphoreType.DMA((2,))],
        ),
        out_shape=jax.ShapeDtypeStruct((n, D), F32),
        compiler_params=_cparams("arbitrary"),
        name="moe_combine",
    )(pos_flat, src_tab, npc_tab, ys, h2, prep, nf)


def _rope_tables(pos):
    half = DH // 2
    inv = np.power(ROPE_BASE, -np.arange(half, dtype=np.float64) / half)
    ang = pos.astype(np.float64)[:, None] * inv[None, :]
    cos = np.concatenate([np.cos(ang), np.cos(ang)], axis=1)
    sin = np.concatenate([-np.sin(ang), np.sin(ang)], axis=1)
    return jnp.asarray(cos, F32), jnp.asarray(sin, F32)


def _retention_tables(c, reps):
    log_g = np.log1p(-np.exp2(-5.0 - np.arange(HEADS, dtype=np.float64)))
    idx = np.arange(c, dtype=np.float64)
    rel = idx[:, None] - idx[None, :]
    dmat = np.where(rel >= 0, np.exp(log_g[:, None, None] * np.maximum(rel, 0.0)), 0.0)
    big = np.zeros((HEADS, c * reps, c * reps))
    for r in range(reps):
        big[:, r * c:(r + 1) * c, r * c:(r + 1) * c] = dmat
    q_in = np.tile(np.exp(log_g[:, None] * (idx + 1.0)), (1, reps))
    k_out = np.tile(np.exp(log_g[:, None] * (c - 1.0 - idx)), (1, reps))
    g_all = np.exp(log_g * c)
    bc = lambda a: jnp.asarray(np.broadcast_to(a[..., None], a.shape + (DH,)), F32)
    return jnp.asarray(big, F32), bc(q_in), bc(k_out), bc(g_all[:, None])


def _block_tri(c, reps, strict=False):
    idx = np.arange(c * reps)
    same = (idx[:, None] // c) == (idx[None, :] // c)
    low = idx[:, None] > idx[None, :] if strict else idx[:, None] >= idx[None, :]
    return jnp.asarray(same & low, BF16), jnp.asarray(same, BF16)


def _expert_schedule(counts, n_rows):
    n_tiles = n_rows // M_TILE
    n_items = n_tiles + N_EXPERTS - 1
    ends = jnp.cumsum(counts)
    starts = ends - counts
    first_tile = starts // M_TILE
    tiles_e = jnp.where(counts > 0, (ends - 1) // M_TILE - first_tile + 1, 0)
    item_end = jnp.cumsum(tiles_e)
    item_start = item_end - tiles_e
    total = item_end[-1]
    i = jnp.arange(n_items, dtype=jnp.int32)
    live = i < total
    ic = jnp.minimum(i, total - 1)
    e = jnp.sum((item_end[None, :] <= ic[:, None]).astype(jnp.int32), axis=1)
    e = jnp.minimum(e, N_EXPERTS - 1)
    hot = e[:, None] == jnp.arange(N_EXPERTS, dtype=jnp.int32)[None, :]
    at_e = lambda table: jnp.sum(jnp.where(hot, table[None, :], 0), axis=1)
    tile = (at_e(first_tile) + ic - at_e(item_start)).astype(jnp.int32)
    lo = jnp.where(live, jnp.maximum(at_e(starts), tile * M_TILE), 0).astype(jnp.int32)
    hi = jnp.where(live, jnp.minimum(at_e(ends), (tile + 1) * M_TILE), 0).astype(jnp.int32)
    shifted = lambda a: jnp.concatenate([jnp.full((1,), -1, jnp.int32), a[:-1]])
    first = (live & (tile != shifted(tile))).astype(jnp.int32)
    fresh = (live & (e != shifted(e))).astype(jnp.int32)
    return tile, e, lo, hi, first, fresh


def kernel(x_prompt, x_sample, state_hgrn, state_ret, cache_mem_k, cache_mem_v, mem_prompt, norm_mix, w_in, hgrn_lb_logits, hgrn_out_norm, ret_out_norm, w_out, norm_x, norm_mem, w_xq, w_mk, w_mv, w_xo, norm_ffn, w_router, b_router, w_gate_up, b_gate_up, w_down, b_down, norm_final):
    bp, tp, _ = x_prompt.shape
    bs, ts, _ = x_sample.shape
    n_p, n_s = bp * tp, bs * ts
    past_len = 16384

    row = lambda a: a.reshape(1, -1).astype(F32)
    win = w_in[0].astype(BF16)
    wout = w_out[0].astype(BF16)
    wxq = w_xq[0].astype(BF16)
    wxo = w_xo[0].astype(BF16)
    wmk = w_mk[0].astype(BF16)
    wmv = w_mv[0].astype(BF16)
    wr = jnp.pad(w_router[0], ((0, 0), (0, 128 - N_EXPERTS))).astype(BF16)
    br = jnp.pad(b_router[0], (0, 128 - N_EXPERTS)).reshape(1, 128).astype(F32)
    bg = b_gate_up[0][:, 0::2].reshape(N_EXPERTS, 1, D)
    bl = b_gate_up[0][:, 1::2].reshape(N_EXPERTS, 1, D)
    bd = b_down[0].reshape(N_EXPERTS, 1, D)
    pw = 256
    perm_np = np.zeros((pw, pw), np.float32)
    perm_np[np.arange(pw), (np.arange(pw) % 2) * (pw // 2) + np.arange(pw) // 2] = 1.0
    perm = jnp.asarray(perm_np, BF16)
    nm, nx, nmem, nffn, nfin = row(norm_mix[0]), row(norm_x[0]), row(norm_mem[0]), row(norm_ffn[0]), row(norm_final)
    gh, gr = row(hgrn_out_norm[0]), row(ret_out_norm[0])
    lbl = hgrn_lb_logits.astype(F32)

    cos_p, sin_p = _rope_tables(np.arange(tp))
    dmat_p, qin_p, kout_p, gall_p = _retention_tables(RET_CHUNK, 1)
    tri_p, _ = _block_tri(HGRN_CHUNK, 1)
    mixed_p, sh_p, sr_p = _mixer_prompt(x_prompt, nm, win, lbl, gh, gr, cos_p, sin_p,
                                        dmat_p, qin_p, kout_p, gall_p, tri_p)
    cos_s, sin_s = _rope_tables(np.tile(past_len + np.arange(ts), S_BATCH))
    dmat_s, qin_s, kout_s, gall_s = _retention_tables(ts, S_BATCH)
    tri_s, ones_s = _block_tri(ts, S_BATCH)
    mixed_s, sh_s, sr_s = _mixer_sample(x_sample, nm, win, lbl, gh, gr, cos_s, sin_s,
                                        dmat_s, qin_s, kout_s, gall_s, tri_s, ones_s,
                                        state_hgrn[0], state_ret[0])

    mk_p, mv_p = _memkv(mem_prompt, nmem, wmk, wmv)
    h2_p, hn_p, lg_p = _xattn_prompt(x_prompt, mixed_p, mk_p, mv_p, wout, nx, wxq, wxo, nffn, wr, br)
    h1_s, q_s = _pre_attn(x_sample.reshape(n_s, D), mixed_s, wout, nx, wxq, F32)
    o_s = _attn_sample(q_s, cache_mem_k[0], cache_mem_v[0], ts)
    h2_s, hn_s, lg_s = _post_attn(h1_s, o_s, wxo, nffn, wr, br)

    stril, _ = _block_tri(R_TILE, 1, strict=True)
    idx, prep, rank, cst, sub, counts = _router(jnp.concatenate([lg_p, lg_s], axis=0), stril)
    counts = counts[0, :N_EXPERTS].astype(jnp.int32)
    offs = jnp.pad(jnp.cumsum(counts) - counts, (0, 128 - N_EXPERTS)).reshape(1, 128).astype(F32)
    n_rows = (n_p + n_s) * TOP_K
    dest, pos, src_tab, npc_tab = _dest(idx, rank, offs, cst, sub, n_rows)
    dest = dest[:, :TOP_K].reshape(-1)
    pos = pos[:, :TOP_K].reshape(-1)
    src_tab, npc_tab = src_tab.reshape(-1), npc_tab.reshape(-1)
    cut = n_p * TOP_K

    xs = _dispatch(dest[:cut], dest[cut:], hn_p, hn_s)
    ys = _experts(_expert_schedule(counts, n_rows), xs, w_gate_up[0], bg, bl, w_down[0], bd, perm)
    y_p = _combine(pos[:cut], src_tab, npc_tab, 0, n_p, ys, h2_p, prep, nfin)
    y_s = _combine(pos[cut:], src_tab, npc_tab, n_p // COMBINE_TILE, n_s, ys, h2_s, prep, nfin)

    return (y_p.reshape(bp, tp, D), y_s.reshape(bs, ts, D),
            sh_p[None], sr_p[None], mk_p[None], mv_p[None], sh_s[None], sr_s[None])
```

```python
import functools

import numpy as np
import jax
import jax.numpy as jnp
from jax import lax
from jax.experimental import pallas as pl
from jax.experimental.pallas import tpu as pltpu

F32 = jnp.float32
BF16 = jnp.bfloat16

D = 1024
HEADS = 4
DH = 128
GW = HEADS * DH
IN_COLS = 8 * GW
ROPE_BASE = 10000.0
N_MEM = 256
X_HEADS = 4
X_DH = D // X_HEADS
N_EXPERTS = 32
TOP_K = 4
LIMIT = 7.0
ALPHA = 1.702
EPS = 1e-6
EXP_CLAMP = 80.0

HGRN_CHUNK = 64
RET_CHUNK = 128
T_TILE = 512
S_BATCH = 8
ROW_TILE = 512
Q_TILE = 512
A_BATCH = 8
R_TILE = 1024
M_TILE = 512
DISPATCH_TILE = 256
COMBINE_TILE = 128
RUN_PIECE = 16
RUN_ROWS = COMBINE_TILE + RUN_PIECE
VMEM_LIMIT = 56 * 1024 * 1024


def _cparams(*sem):
    return pltpu.CompilerParams(dimension_semantics=sem, vmem_limit_bytes=VMEM_LIMIT)


def _dot(a, b):
    return jnp.dot(a.astype(BF16), b.astype(BF16), preferred_element_type=F32)


def _dot_nt(a, b):
    return lax.dot_general(a.astype(BF16), b.astype(BF16), (((1,), (1,)), ((), ())),
                           preferred_element_type=F32)


def _dot_tn(a, b):
    return lax.dot_general(a, b, (((0,), (0,)), ((), ())), preferred_element_type=F32)


def _rms(x, g):
    return x * lax.rsqrt(jnp.mean(x * x, axis=-1, keepdims=True) + EPS) * g


ROW_CHUNKS = D // 128


def _load_row_tiles(ref, n, first=0):
    base = first * ROW_CHUNKS
    return jnp.concatenate([ref[pl.ds(base + c, n, stride=ROW_CHUNKS), :] for c in range(ROW_CHUNKS)], axis=1)


def _store_row_tiles(ref, x, first=0):
    n = x.shape[0]
    base = first * ROW_CHUNKS
    for c in range(ROW_CHUNKS):
        ref[pl.ds(base + c, n, stride=ROW_CHUNKS), :] = x[:, c * 128:(c + 1) * 128]


def _cumsum_rows(tri, v):
    hi = v.astype(BF16)
    lo = (v - hi.astype(F32)).astype(BF16)
    return (jnp.dot(tri, hi, preferred_element_type=F32)
            + jnp.dot(tri, lo, preferred_element_type=F32))


def _lower_bound(lbl_ref):
    l = lbl_ref[...]
    m = jnp.max(l, axis=0, keepdims=True)
    e = jnp.exp(l - m)
    return e[0:1, :] / jnp.sum(e, axis=0, keepdims=True)


def _row_to_col(row):
    n = row.shape[1]
    eye = lax.broadcasted_iota(jnp.int32, (n, n), 0) == lax.broadcasted_iota(jnp.int32, (n, n), 1)
    return jnp.sum(jnp.where(eye, jnp.broadcast_to(row, (n, n)), 0.0), axis=1, keepdims=True)


def _rotary(x, cs, sn):
    return x * cs + pltpu.roll(x, DH // 2, 1) * sn


def _hgrn_out(o, gh, gate):
    on = o * lax.rsqrt(jnp.mean(o * o, axis=-1, keepdims=True) + EPS) * gh
    return on * jax.nn.sigmoid(gate)


def _ret_out(o, gr, gate):
    mu = jnp.mean(o, axis=-1, keepdims=True)
    xc = o - mu
    var = jnp.mean(xc * xc, axis=-1, keepdims=True)
    return xc * lax.rsqrt(var + EPS) * gr * (gate * jax.nn.sigmoid(gate))


def _mixer_prompt_kernel(x_ref, nm_ref, win_ref, lbl_ref, gh_ref, gr_ref, cos_ref, sin_ref,
                         dmat_ref, qin_ref, kout_ref, gall_ref, tri_ref,
                         mixed_ref, sh_ref, sr_ref, proj_ref, f_ref, lc_ref):
    @pl.when(pl.program_id(1) == 0)
    def _():
        sh_ref[...] = jnp.zeros_like(sh_ref)
        sr_ref[...] = jnp.zeros_like(sr_ref)

    xn = _rms(x_ref[0], nm_ref[...]).astype(BF16)
    proj_ref[...] = jnp.dot(xn, win_ref[...], preferred_element_type=F32)
    lb = _lower_bound(lbl_ref)
    gh = gh_ref[...]
    gr = gr_ref[...]
    c = HGRN_CHUNK
    causal = (lax.broadcasted_iota(jnp.int32, (c, c), 0) >= lax.broadcasted_iota(jnp.int32, (c, c), 1))

    def gate_step(ci, worst):
        rows = pl.ds(pl.multiple_of(ci * c, c), c)
        f = lb + (1.0 - lb) * jax.nn.sigmoid(proj_ref[rows, GW:2 * GW])
        lc = _cumsum_rows(tri_ref[...], jnp.log(f))
        f_ref[rows, :] = f
        lc_ref[rows, :] = lc
        return jnp.maximum(worst, jnp.max(jnp.abs(lc - lc[c // 2 - 1:c // 2, :])))

    worst = lax.fori_loop(0, T_TILE // c, gate_step, jnp.float32(0.0), unroll=True)

    def scores_factored(hq, kk, lc, h, r0):
        sl = slice(h * DH, (h + 1) * DH)
        mid = lc[c // 2 - 1:c // 2, sl]
        qd = hq[:, sl] * jnp.exp(lc[:, sl] - mid)
        kd = kk[:, sl] * jnp.exp(mid - lc[:, sl])
        return jnp.where(causal, _dot_nt(qd, kd), 0.0)

    def scores_termwise(hq, kk, lc, h, r0):
        sl = slice(h * DH, (h + 1) * DH)
        col_id = lax.broadcasted_iota(jnp.int32, (c, c), 1)

        sub_id = lax.broadcasted_iota(jnp.int32, (8, DH), 0)

        def row_of(ref, s):
            grp = ref[pl.ds(pl.multiple_of(r0 + lax.shift_left(lax.shift_right_logical(s, 3), 3), 8), 8), sl]
            return jnp.sum(jnp.where(sub_id == (s & 7), grp, 0.0), axis=0, keepdims=True)

        def body(s, a):
            lcs = row_of(lc_ref, s)
            ks_row = 1.0 - row_of(f_ref, s)
            w = jnp.exp(jnp.minimum(lc[:, sl] - lcs, 0.0))
            col = jnp.sum(hq[:, sl] * ks_row * w, axis=1, keepdims=True)
            return a + jnp.where(col_id == s, col, 0.0)

        return jnp.where(causal, lax.fori_loop(0, c, body, jnp.zeros((c, c), F32)), 0.0)

    def hgrn_step(scores, ci, carry):
        r0 = pl.multiple_of(ci * c, c)
        rows = pl.ds(r0, c)
        hq = proj_ref[rows, 0:GW] * (DH ** -0.5)
        kk = 1.0 - f_ref[rows, :]
        lc = lc_ref[rows, :]
        last = lc[c - 1:c, :]
        qs = hq * jnp.exp(lc)
        ks = kk * jnp.exp(last - lc)
        dl = jnp.exp(last)
        for h in range(HEADS):
            sl = slice(h * DH, (h + 1) * DH)
            v = proj_ref[rows, 2 * GW + h * DH:2 * GW + (h + 1) * DH].astype(BF16)
            a = scores(hq, kk, lc, h, r0)
            s_old = sh_ref[0, h]
            o = _dot(a, v) + _dot(qs[:, sl], s_old)
            sh_ref[0, h] = _row_to_col(dl[:, sl]) * s_old + _dot_tn(ks[:, sl].astype(BF16), v)
            gate = proj_ref[rows, 3 * GW + h * DH:3 * GW + (h + 1) * DH]
            mixed_ref[0, rows, sl] = _hgrn_out(o, gh, gate).astype(BF16)
        return carry

    @pl.when(worst <= EXP_CLAMP)
    def _():
        lax.fori_loop(0, T_TILE // c, functools.partial(hgrn_step, scores_factored), 0, unroll=True)

    @pl.when(jnp.logical_not(worst <= EXP_CLAMP))
    def _():
        lax.fori_loop(0, T_TILE // c, functools.partial(hgrn_step, scores_termwise), 0)

    cr = RET_CHUNK

    def ret_step(ci, carry):
        rows = pl.ds(pl.multiple_of(ci * cr, cr), cr)
        cs = cos_ref[rows, :]
        sn = sin_ref[rows, :]
        for h in range(HEADS):
            base = 4 * GW + h * DH
            q = _rotary(proj_ref[rows, base:base + DH], cs, sn)
            k = _rotary(proj_ref[rows, base + GW:base + GW + DH], cs, sn) * (DH ** -0.5)
            v = proj_ref[rows, base + 2 * GW:base + 2 * GW + DH].astype(BF16)
            sc = _dot_nt(q, k) * dmat_ref[h]
            s_old = sr_ref[0, h]
            o = _dot(sc, v) + _dot(q * qin_ref[h], s_old)
            sr_ref[0, h] = gall_ref[h] * s_old + _dot_tn((k * kout_ref[h]).astype(BF16), v)
            gate = proj_ref[rows, base + 3 * GW:base + 3 * GW + DH]
            mixed_ref[0, rows, GW + h * DH:GW + (h + 1) * DH] = _ret_out(o, gr, gate).astype(BF16)
        return carry

    lax.fori_loop(0, T_TILE // cr, ret_step, 0, unroll=True)


def _mixer_prompt(x, nm, win, lbl, gh, gr, cos, sin, dmat, qin, kout, gall, tri):
    b, t, _ = x.shape
    full = lambda shape: pl.BlockSpec(shape, lambda i, j: (0,) * len(shape))
    state = pl.BlockSpec((1, HEADS, DH, DH), lambda i, j: (i, 0, 0, 0))
    return pl.pallas_call(
        _mixer_prompt_kernel,
        grid=(b, t // T_TILE),
        in_specs=[
            pl.BlockSpec((1, T_TILE, D), lambda i, j: (i, j, 0)),
            full((1, D)), full((D, IN_COLS)), full(lbl.shape), full((1, DH)), full((1, DH)),
            pl.BlockSpec((T_TILE, DH), lambda i, j: (j, 0)),
            pl.BlockSpec((T_TILE, DH), lambda i, j: (j, 0)),
            full(dmat.shape), full(qin.shape), full(kout.shape), full(gall.shape), full(tri.shape),
        ],
        out_specs=[pl.BlockSpec((1, T_TILE, D), lambda i, j: (i, j, 0)), state, state],
        out_shape=[jax.ShapeDtypeStruct((b, t, D), BF16),
                   jax.ShapeDtypeStruct((b, HEADS, DH, DH), F32),
                   jax.ShapeDtypeStruct((b, HEADS, DH, DH), F32)],
        scratch_shapes=[pltpu.VMEM((T_TILE, IN_COLS), F32), pltpu.VMEM((T_TILE, GW), F32),
                        pltpu.VMEM((T_TILE, GW), F32)],
        compiler_params=_cparams("parallel", "arbitrary"),
        name="mixer_prompt",
    )(x, nm, win, lbl, gh, gr, cos, sin, dmat, qin, kout, gall, tri)


def _mixer_sample_kernel(x_ref, nm_ref, win_ref, lbl_ref, gh_ref, gr_ref, cos_ref, sin_ref,
                         dmat_ref, qin_ref, kout_ref, gall_ref, tri_ref, ones_ref,
                         shin_ref, srin_ref, mixed_ref, sh_ref, sr_ref, proj_ref):
    nb, ts, _ = x_ref.shape
    rows_n = nb * ts
    xn = _rms(x_ref[...].reshape(rows_n, D), nm_ref[...]).astype(BF16)
    proj_ref[...] = jnp.dot(xn, win_ref[...], preferred_element_type=F32)
    lb = _lower_bound(lbl_ref)
    gh = gh_ref[...]
    gr = gr_ref[...]
    hq = proj_ref[:, 0:GW] * (DH ** -0.5)
    f = lb + (1.0 - lb) * jax.nn.sigmoid(proj_ref[:, GW:2 * GW])
    kk = 1.0 - f
    lf = jnp.log(f)
    lc = _cumsum_rows(tri_ref[...], lf)
    last = _cumsum_rows(ones_ref[...], lf)
    qd = hq * jnp.exp(lc)
    ks = kk * jnp.exp(last - lc)
    dl = jnp.exp(last)
    t_pos = lax.broadcasted_iota(jnp.int32, (nb, ts, 1), 1)

    def intra_termwise(h):
        sl = slice(h * DH, (h + 1) * DH)
        q3 = hq[:, sl].reshape(nb, ts, DH)
        k3 = kk[:, sl].reshape(nb, ts, DH)
        lc3 = lc[:, sl].reshape(nb, ts, DH)
        v3 = proj_ref[:, 2 * GW + h * DH:2 * GW + (h + 1) * DH].reshape(nb, ts, DH)
        o3 = jnp.zeros((nb, ts, DH), F32)
        for s in range(ts):
            w = jnp.exp(jnp.minimum(lc3 - lc3[:, s:s + 1, :], 0.0))
            score = jnp.sum(q3 * k3[:, s:s + 1, :] * w, axis=-1, keepdims=True)
            o3 = o3 + jnp.where(t_pos >= s, score, 0.0) * v3[:, s:s + 1, :]
        return o3.reshape(rows_n, DH)

    for h in range(HEADS):
        sl = slice(h * DH, (h + 1) * DH)
        v = proj_ref[:, 2 * GW + h * DH:2 * GW + (h + 1) * DH]
        o_intra = intra_termwise(h)
        gate = proj_ref[:, 3 * GW + h * DH:3 * GW + (h + 1) * DH]
        for b in range(nb):
            r = slice(b * ts, (b + 1) * ts)
            s_old = shin_ref[b, h]
            o = o_intra[r] + _dot(qd[r, sl], s_old)
            sh_ref[b, h] = (_row_to_col(dl[b * ts:b * ts + 1, sl]) * s_old
                            + _dot_tn(ks[r, sl], v[r]))
            mixed_ref[r, sl] = _hgrn_out(o, gh, gate[r])

    cs = cos_ref[...]
    sn = sin_ref[...]
    for h in range(HEADS):
        base = 4 * GW + h * DH
        q = _rotary(proj_ref[:, base:base + DH], cs, sn)
        k = _rotary(proj_ref[:, base + GW:base + GW + DH], cs, sn) * (DH ** -0.5)
        v = proj_ref[:, base + 2 * GW:base + 2 * GW + DH]
        sc = _dot_nt(q, k) * dmat_ref[h]
        o_intra = _dot(sc, v)
        qi = q * qin_ref[h]
        ko = k * kout_ref[h]
        gate = proj_ref[:, base + 3 * GW:base + 3 * GW + DH]
        for b in range(nb):
            r = slice(b * ts, (b + 1) * ts)
            s_old = srin_ref[b, h]
            o = o_intra[r] + _dot(qi[r], s_old)
            sr_ref[b, h] = gall_ref[h] * s_old + _dot_tn(ko[r], v[r])
            mixed_ref[r, GW + h * DH:GW + (h + 1) * DH] = _ret_out(o, gr, gate[r])


def _mixer_sample(x, nm, win, lbl, gh, gr, cos, sin, dmat, qin, kout, gall, tri, ones, sh, sr):
    b, t, _ = x.shape
    nb = S_BATCH
    full = lambda shape: pl.BlockSpec(shape, lambda i: (0,) * len(shape))
    state = pl.BlockSpec((nb, HEADS, DH, DH), lambda i: (i, 0, 0, 0))
    return pl.pallas_call(
        _mixer_sample_kernel,
        grid=(b // nb,),
        in_specs=[
            pl.BlockSpec((nb, t, D), lambda i: (i, 0, 0)),
            full((1, D)), full((D, IN_COLS)), full(lbl.shape), full((1, DH)), full((1, DH)),
            full(cos.shape), full(sin.shape),
            full(dmat.shape), full(qin.shape), full(kout.shape), full(gall.shape),
            full(tri.shape), full(ones.shape), state, state,
        ],
        out_specs=[pl.BlockSpec((nb * t, D), lambda i: (i, 0)), state, state],
        out_shape=[jax.ShapeDtypeStruct((b * t, D), F32),
                   jax.ShapeDtypeStruct(sh.shape, F32),
                   jax.ShapeDtypeStruct(sr.shape, F32)],
        scratch_shapes=[pltpu.VMEM((nb * t, IN_COLS), F32)],
        compiler_params=_cparams("parallel"),
        name="mixer_sample",
    )(x, nm, win, lbl, gh, gr, cos, sin, dmat, qin, kout, gall, tri, ones, sh, sr)


def _memkv_kernel(mem_ref, g_ref, wk_ref, wv_ref, k_ref, v_ref):
    mn = _rms(mem_ref[0], g_ref[...]).astype(BF16)
    k = jnp.dot(mn, wk_ref[...], preferred_element_type=F32)
    v = jnp.dot(mn, wv_ref[...], preferred_element_type=F32)
    for h in range(X_HEADS):
        k_ref[0, h] = k[:, h * X_DH:(h + 1) * X_DH]
        v_ref[0, h] = v[:, h * X_DH:(h + 1) * X_DH]


def _memkv(mem, g, wk, wv):
    b = mem.shape[0]
    full = lambda shape: pl.BlockSpec(shape, lambda i: (0,) * len(shape))
    kv = pl.BlockSpec((1, X_HEADS, N_MEM, X_DH), lambda i: (i, 0, 0, 0))
    shape = jax.ShapeDtypeStruct((b, X_HEADS, N_MEM, X_DH), F32)
    return pl.pallas_call(
        _memkv_kernel,
        grid=(b,),
        in_specs=[pl.BlockSpec((1, N_MEM, D), lambda i: (i, 0, 0)), full((1, D)), full((D, D)), full((D, D))],
        out_specs=[kv, kv],
        out_shape=[shape, shape],
        compiler_params=_cparams("parallel"),
        name="mem_kv",
    )(mem, g, wk, wv)


def _pre_attn_kernel(x_ref, mixed_ref, wout_ref, nx_ref, wxq_ref, h1_ref, q_ref):
    h1 = x_ref[...] + _dot(mixed_ref[...], wout_ref[...])
    h1_ref[...] = h1
    hn = _rms(h1, nx_ref[...])
    q_ref[...] = _dot(hn, wxq_ref[...]).astype(q_ref.dtype)


def _pre_attn(x, mixed, wout, nx, wxq, q_dtype):
    n = x.shape[0]
    row = pl.BlockSpec((ROW_TILE, D), lambda i: (i, 0))
    full = lambda shape: pl.BlockSpec(shape, lambda i: (0,) * len(shape))
    return pl.pallas_call(
        _pre_attn_kernel,
        grid=(n // ROW_TILE,),
        in_specs=[row, row, full((D, D)), full((1, D)), full((D, D))],
        out_specs=[row, row],
        out_shape=[jax.ShapeDtypeStruct((n, D), F32), jax.ShapeDtypeStruct((n, D), q_dtype)],
        compiler_params=_cparams("parallel"),
        name="pre_attn",
    )(x, mixed, wout, nx, wxq)


def _post_attn_kernel(h1_ref, o_ref, wxo_ref, nf_ref, wr_ref, br_ref, h2_ref, hn_ref, lg_ref):
    h2 = h1_ref[...] + _dot(o_ref[...], wxo_ref[...])
    h2_ref[...] = h2
    hn = _rms(h2, nf_ref[...])
    _store_row_tiles(hn_ref, hn)
    lg_ref[...] = _dot(hn, wr_ref[...]) + br_ref[...]


def _post_attn(h1, o, wxo, nf, wr, br):
    n = h1.shape[0]
    row = pl.BlockSpec((ROW_TILE, D), lambda i: (i, 0))
    full = lambda shape: pl.BlockSpec(shape, lambda i: (0,) * len(shape))
    return pl.pallas_call(
        _post_attn_kernel,
        grid=(n // ROW_TILE,),
        in_specs=[row, row, full((D, D)), full((1, D)), full((D, 128)), full((1, 128))],
        out_specs=[row, pl.BlockSpec((ROW_TILE * ROW_CHUNKS, 128), lambda i: (i, 0)),
                   pl.BlockSpec((ROW_TILE, 128), lambda i: (i, 0))],
        out_shape=[jax.ShapeDtypeStruct((n, D), F32), jax.ShapeDtypeStruct((n * ROW_CHUNKS, 128), F32),
                   jax.ShapeDtypeStruct((n, 128), F32)],
        compiler_params=_cparams("parallel"),
        name="post_attn",
    )(h1, o, wxo, nf, wr, br)


def _softmax_rows(s):
    m = jnp.max(s, axis=-1, keepdims=True)
    e = jnp.exp(s - m)
    return e / jnp.sum(e, axis=-1, keepdims=True)


def _xattn_prompt_kernel(x_ref, mixed_ref, k_ref, v_ref, wout_ref, nx_ref, wxq_ref, wxo_ref,
                         nf_ref, wr_ref, br_ref, h2_ref, hn_ref, lg_ref, o_scr):
    h1 = x_ref[0] + _dot(mixed_ref[0], wout_ref[...])
    q = _dot(_rms(h1, nx_ref[...]), wxq_ref[...]).astype(BF16)
    for h in range(X_HEADS):
        sl = slice(h * X_DH, (h + 1) * X_DH)
        s = _dot_nt(q[:, sl], k_ref[0, h]) * (X_DH ** -0.5)
        o_scr[:, sl] = _dot(_softmax_rows(s), v_ref[0, h]).astype(BF16)
    h2 = h1 + _dot(o_scr[...], wxo_ref[...])
    h2_ref[...] = h2
    hn = _rms(h2, nf_ref[...])
    _store_row_tiles(hn_ref, hn)
    lg_ref[...] = _dot(hn, wr_ref[...]) + br_ref[...]


def _xattn_prompt(x, mixed, mk, mv, wout, nx, wxq, wxo, nf, wr, br):
    b, seq, _ = x.shape
    per_seq = seq // Q_TILE
    full = lambda shape: pl.BlockSpec(shape, lambda i, j: (0,) * len(shape))
    tok = pl.BlockSpec((1, Q_TILE, D), lambda i, j: (i, j, 0))
    kv = pl.BlockSpec((1, X_HEADS, N_MEM, X_DH), lambda i, j: (i, 0, 0, 0))
    row = pl.BlockSpec((Q_TILE, D), lambda i, j: (i * per_seq + j, 0))
    n = b * seq
    return pl.pallas_call(
        _xattn_prompt_kernel,
        grid=(b, per_seq),
        in_specs=[tok, tok, kv, kv, full((D, D)), full((1, D)), full((D, D)), full((D, D)),
                  full((1, D)), full((D, 128)), full((1, 128))],
        out_specs=[row, pl.BlockSpec((Q_TILE * ROW_CHUNKS, 128), lambda i, j: (i * per_seq + j, 0)),
                   pl.BlockSpec((Q_TILE, 128), lambda i, j: (i * per_seq + j, 0))],
        out_shape=[jax.ShapeDtypeStruct((n, D), F32), jax.ShapeDtypeStruct((n * ROW_CHUNKS, 128), F32),
                   jax.ShapeDtypeStruct((n, 128), F32)],
        scratch_shapes=[pltpu.VMEM((Q_TILE, D), BF16)],
        compiler_params=_cparams("parallel", "arbitrary"),
        name="xattn_prompt",
    )(x, mixed, mk, mv, wout, nx, wxq, wxo, nf, wr, br)


def _attn_sample_kernel(q_ref, k_ref, v_ref, o_ref):
    nb = k_ref.shape[0]
    ts = q_ref.shape[0] // nb
    units = [(b, h) for b in range(nb) for h in range(X_HEADS)]
    s = jnp.concatenate(
        [_dot_nt(q_ref[b * ts:(b + 1) * ts, h * X_DH:(h + 1) * X_DH], k_ref[b, h]) for b, h in units], axis=0)
    p = _softmax_rows(s * (X_DH ** -0.5))
    for i, (b, h) in enumerate(units):
        o_ref[b * ts:(b + 1) * ts, h * X_DH:(h + 1) * X_DH] = _dot(p[i * ts:(i + 1) * ts], v_ref[b, h])


def _attn_sample(q, ck, cv, ts):
    n = q.shape[0]
    nb = A_BATCH
    kv = pl.BlockSpec((nb, X_HEADS, N_MEM, X_DH), lambda i: (i, 0, 0, 0))
    qs = pl.BlockSpec((nb * ts, D), lambda i: (i, 0))
    return pl.pallas_call(
        _attn_sample_kernel,
        grid=(n // (nb * ts),),
        in_specs=[qs, kv, kv],
        out_specs=qs,
        out_shape=jax.ShapeDtypeStruct((n, D), F32),
        compiler_params=_cparams("parallel"),
        name="attn_sample",
    )(q, ck, cv)


def _router_kernel(lg_ref, stril_ref, idx_ref, prep_ref, rank_ref, cst_ref, sub_ref, cnt_ref, carry_ref):
    @pl.when(pl.program_id(0) == 0)
    def _():
        carry_ref[...] = jnp.zeros_like(carry_ref)

    n = lg_ref.shape[0]
    lane = lax.broadcasted_iota(jnp.int32, (n, 128), 1)
    lane_f = lane.astype(F32)
    l = jnp.where(lane < N_EXPERTS, lg_ref[...], -jnp.inf)
    tops, idxs, hots = [], [], []
    for _ in range(TOP_K):
        m = jnp.max(l, axis=1, keepdims=True)
        idx = jnp.min(jnp.where(l == m, lane_f, 128.0), axis=1, keepdims=True)
        hot = lane_f == idx
        l = jnp.where(hot, -jnp.inf, l)
        tops.append(m)
        idxs.append(idx)
        hots.append(hot)
    sel = jnp.where(hots[0] | hots[1] | hots[2] | hots[3], 1.0, 0.0)
    before = jnp.dot(stril_ref[...], sel.astype(BF16), preferred_element_type=F32) + carry_ref[...]
    carry_ref[...] += jnp.sum(sel, axis=0, keepdims=True)
    cnt_ref[...] = carry_ref[...]
    subs = n // COMBINE_TILE
    cst_ref[...] = jnp.concatenate([before[j * COMBINE_TILE:j * COMBINE_TILE + 1, :] for j in range(subs)], axis=0)
    sub_ref[...] = jnp.concatenate(
        [jnp.sum(sel[j * COMBINE_TILE:(j + 1) * COMBINE_TILE, :], axis=0, keepdims=True) for j in range(subs)], axis=0)
    es = [jnp.exp(t - tops[0]) for t in tops]
    den = es[0] + es[1] + es[2] + es[3]
    idx_out = jnp.zeros((n, 128), jnp.int32)
    rank_out = jnp.zeros((n, 128), jnp.int32)
    for k in range(TOP_K):
        rank = jnp.sum(jnp.where(hots[k], before, 0.0), axis=1, keepdims=True).astype(jnp.int32)
        idx_out = jnp.where(lane == k, idxs[k].astype(jnp.int32), idx_out)
        rank_out = jnp.where(lane == k, rank, rank_out)
        prep_ref[pl.ds(k, n, stride=TOP_K), :] = jnp.broadcast_to(es[k] / den, (n, 128))
    idx_ref[...] = idx_out
    rank_ref[...] = rank_out


def _router(logits, stril):
    n = logits.shape[0]
    subs = R_TILE // COMBINE_TILE
    row = pl.BlockSpec((R_TILE, 128), lambda i: (i, 0))
    one = pl.BlockSpec((1, 128), lambda i: (0, 0))
    sub = pl.BlockSpec((subs, 128), lambda i: (i, 0))
    n_sub = n // COMBINE_TILE
    return pl.pallas_call(
        _router_kernel,
        grid=(n // R_TILE,),
        in_specs=[row, pl.BlockSpec((R_TILE, R_TILE), lambda i: (0, 0))],
        out_specs=[row, pl.BlockSpec((R_TILE * TOP_K, 128), lambda i: (i, 0)), row, sub, sub, one],
        out_shape=[jax.ShapeDtypeStruct((n, 128), jnp.int32), jax.ShapeDtypeStruct((n * TOP_K, 128), F32),
                   jax.ShapeDtypeStruct((n, 128), jnp.int32), jax.ShapeDtypeStruct((n_sub, 128), F32),
                   jax.ShapeDtypeStruct((n_sub, 128), F32), jax.ShapeDtypeStruct((1, 128), F32)],
        scratch_shapes=[pltpu.VMEM((1, 128), F32)],
        compiler_params=_cparams("arbitrary"),
        name="router",
    )(logits, stril)


def _dest_kernel(n_rows, idx_ref, rank_ref, offs_ref, cst_ref, sub_ref, dest_ref, pos_ref, src_ref, npc_ref):
    n = idx_ref.shape[0]
    subs = n // COMBINE_TILE
    lane = lax.broadcasted_iota(jnp.int32, (n, 128), 1)
    lane_row = lax.broadcasted_iota(jnp.int32, (subs, 128), 1).astype(F32)
    offs = offs_ref[...]
    before = cst_ref[...]
    start = offs + before
    pieces = jnp.floor((sub_ref[...] + (RUN_PIECE - 1)) * (1.0 / RUN_PIECE))
    shift = jnp.maximum(start + pieces * RUN_PIECE - n_rows, 0.0)
    src_ref[...] = ((start - shift) * ROW_CHUNKS).astype(jnp.int32)
    npc_ref[...] = pieces.astype(jnp.int32)
    slot = lane_row * RUN_ROWS - before + shift
    slot = jnp.concatenate([jnp.broadcast_to(slot[j:j + 1, :], (COMBINE_TILE, 128)) for j in range(subs)], axis=0)
    idx = idx_ref[...]
    d_out = jnp.zeros((n, 128), F32)
    p_out = jnp.zeros((n, 128), F32)
    for k in range(TOP_K):
        hot = lane == idx[:, k:k + 1]
        d_out = jnp.where(lane == k, jnp.sum(jnp.where(hot, offs, 0.0), axis=1, keepdims=True), d_out)
        p_out = jnp.where(lane == k, jnp.sum(jnp.where(hot, slot, 0.0), axis=1, keepdims=True), p_out)
    rank = rank_ref[...]
    dest_ref[...] = (d_out.astype(jnp.int32) + rank) * ROW_CHUNKS
    pos_ref[...] = (p_out.astype(jnp.int32) + rank) * ROW_CHUNKS


def _dest(idx, rank, offs, cst, sub, n_rows):
    n = idx.shape[0]
    subs = R_TILE // COMBINE_TILE
    row = pl.BlockSpec((R_TILE, 128), lambda i: (i, 0))
    tab = pl.BlockSpec((subs, 128), lambda i: (i, 0))
    n_sub = n // COMBINE_TILE
    return pl.pallas_call(
        functools.partial(_dest_kernel, n_rows),
        grid=(n // R_TILE,),
        in_specs=[row, row, pl.BlockSpec((1, 128), lambda i: (0, 0)), tab, tab],
        out_specs=[row, row, tab, tab],
        out_shape=[jax.ShapeDtypeStruct((n, 128), jnp.int32), jax.ShapeDtypeStruct((n, 128), jnp.int32),
                   jax.ShapeDtypeStruct((n_sub, 128), jnp.int32), jax.ShapeDtypeStruct((n_sub, 128), jnp.int32)],
        compiler_params=_cparams("parallel"),
        name="route_dest",
    )(idx, rank, offs, cst, sub)


def _dispatch_kernel(p_tiles, dest_p_ref, dest_s_ref, hn_p_ref, hn_s_ref, xs_ref, sem):
    nt = DISPATCH_TILE
    i = pl.program_id(0)

    def scatter(dest_ref, hn_ref, tile):
        base = tile * (nt * TOP_K)

        def copy(t, k):
            dst = pl.multiple_of(dest_ref[base + t * TOP_K + k], ROW_CHUNKS)
            return pltpu.make_async_copy(hn_ref.at[pl.ds(t * ROW_CHUNKS, ROW_CHUNKS)],
                                         xs_ref.at[pl.ds(dst, ROW_CHUNKS)], sem)

        def start(t, carry):
            for k in range(TOP_K):
                copy(t, k).start(priority=k % 2)
            return carry

        def wait(t, carry):
            for k in range(TOP_K):
                copy(t, k).wait()
            return carry

        for t in range(nt):
            start(t, 0)
        for t in range(nt):
            wait(t, 0)

    @pl.when(i < p_tiles)
    def _():
        scatter(dest_p_ref, hn_p_ref, i)

    @pl.when(i >= p_tiles)
    def _():
        scatter(dest_s_ref, hn_s_ref, i - p_tiles)


def _dispatch(dest_p, dest_s, hn_p, hn_s):
    p_tiles = hn_p.shape[0] // (DISPATCH_TILE * ROW_CHUNKS)
    s_tiles = hn_s.shape[0] // (DISPATCH_TILE * ROW_CHUNKS)
    n_rows = (hn_p.shape[0] + hn_s.shape[0]) * TOP_K
    return pl.pallas_call(
        functools.partial(_dispatch_kernel, p_tiles),
        grid_spec=pltpu.PrefetchScalarGridSpec(
            num_scalar_prefetch=2,
            grid=(p_tiles + s_tiles,),
            in_specs=[pl.BlockSpec((DISPATCH_TILE * ROW_CHUNKS, 128),
                                   lambda i, dp, ds: (jnp.minimum(i, p_tiles - 1), 0)),
                      pl.BlockSpec((DISPATCH_TILE * ROW_CHUNKS, 128),
                                   lambda i, dp, ds: (jnp.maximum(i - p_tiles, 0), 0))],
            out_specs=pl.BlockSpec(memory_space=pl.ANY),
            scratch_shapes=[pltpu.SemaphoreType.DMA],
        ),
        out_shape=jax.ShapeDtypeStruct((n_rows, 128), F32),
        compiler_params=_cparams("arbitrary"),
        name="moe_dispatch",
    )(dest_p, dest_s, hn_p, hn_s)


def _experts_kernel(tile_ref, exp_ref, lo_ref, hi_ref, first_ref, fresh_ref, mode_ref,
                    xs_ref, wgu_ref, bg_ref, bl_ref, wd_ref, bd_ref, perm_ref, ys_ref,
                    wg_s, wl_s, wd_s):
    i = pl.program_id(0)
    lo = lo_ref[i]
    hi = hi_ref[i]
    half = M_TILE // 2

    @pl.when(fresh_ref[i] == 1)
    def _():
        pw = perm_ref.shape[0]
        for c in range(2 * D // pw):
            blk = wgu_ref[0, :, c * pw:(c + 1) * pw].astype(BF16)
            sep = jnp.dot(blk, perm_ref[...], preferred_element_type=F32)
            wg_s[:, c * pw // 2:(c + 1) * pw // 2] = sep[:, :pw // 2].astype(BF16)
            wl_s[:, c * pw // 2:(c + 1) * pw // 2] = sep[:, pw // 2:].astype(BF16)
        wd_s[...] = wd_ref[0].astype(BF16)

    def rows_of_tile(r0, n):
        x = _load_row_tiles(xs_ref, n, r0).astype(BF16)
        glu = jnp.minimum(jnp.dot(x, wg_s[...], preferred_element_type=F32) + bg_ref[0], LIMIT)
        lin = jnp.clip(jnp.dot(x, wl_s[...], preferred_element_type=F32) + bl_ref[0], -LIMIT, LIMIT)
        hmid = glu * jax.nn.sigmoid(ALPHA * glu) * (lin + 1.0)
        y = _dot(hmid, wd_s[...]) + bd_ref[0]
        row = tile_ref[i] * M_TILE + r0 + lax.broadcasted_iota(jnp.int32, (n, 1), 0)
        mine = (row >= lo) & (row < hi)

        @pl.when(first_ref[i] == 1)
        def _():
            _store_row_tiles(ys_ref, jnp.where(mine, y, 0.0), r0)
            if n < M_TILE:
                other = (half - r0) * ROW_CHUNKS
                ys_ref[other:other + half * ROW_CHUNKS, :] = jnp.zeros((half * ROW_CHUNKS, 128), F32)

        @pl.when(first_ref[i] == 0)
        def _():
            _store_row_tiles(ys_ref, jnp.where(mine, y, _load_row_tiles(ys_ref, n, r0)), r0)

    @pl.when(mode_ref[i] == 0)
    def _():
        rows_of_tile(0, M_TILE)

    @pl.when(mode_ref[i] == 1)
    def _():
        rows_of_tile(0, half)

    @pl.when(mode_ref[i] == 2)
    def _():
        rows_of_tile(half, half)


def _experts(meta, xs, wgu, bg, bl, wd, bd, perm):
    tile, expert, lo, hi, first, fresh, mode = meta
    rows = pl.BlockSpec((M_TILE * ROW_CHUNKS, 128), lambda i, t, e, *_: (t[i], 0))
    bspec = pl.BlockSpec((1, 1, D), lambda i, t, e, *_: (e[i], 0, 0))
    return pl.pallas_call(
        _experts_kernel,
        grid_spec=pltpu.PrefetchScalarGridSpec(
            num_scalar_prefetch=7,
            grid=(tile.shape[0],),
            in_specs=[rows,
                      pl.BlockSpec((1, D, 2 * D), lambda i, t, e, *_: (e[i], 0, 0)), bspec, bspec,
                      pl.BlockSpec((1, D, D), lambda i, t, e, *_: (e[i], 0, 0)), bspec,
                      pl.BlockSpec(perm.shape, lambda i, *_: (0, 0))],
            out_specs=rows,
            scratch_shapes=[pltpu.VMEM((D, D), BF16)] * 3,
        ),
        out_shape=jax.ShapeDtypeStruct(xs.shape, F32),
        compiler_params=_cparams("arbitrary"),
        name="moe_experts",
    )(tile, expert, lo, hi, first, fresh, mode, xs, wgu, bg, bl, wd, bd, perm)


def _combine_kernel(tile0, pos_ref, src_ref, npc_ref, ys_ref, h2_ref, prep_ref, nf_ref, y_ref,
                    buf_ref, moe_ref, sem):
    nt = COMBINE_TILE
    i = pl.program_id(0)
    piece_rows = RUN_PIECE * ROW_CHUNKS

    def piece(tile, e, j):
        tab = (tile + tile0) * 128
        half = tile & 1
        src = pl.multiple_of(src_ref[tab + e] + j * piece_rows, ROW_CHUNKS)
        dst = pl.multiple_of(e * (RUN_ROWS * ROW_CHUNKS) + j * piece_rows, ROW_CHUNKS)
        return pltpu.make_async_copy(ys_ref.at[pl.ds(src, piece_rows)],
                                     buf_ref.at[half, pl.ds(dst, piece_rows)], sem.at[half])

    def fetch(tile):
        for e in range(N_EXPERTS):
            def start(j, carry, e=e):
                piece(tile, e, j).start(priority=e % 2)
                return carry
            lax.fori_loop(0, npc_ref[(tile + tile0) * 128 + e], start, 0)

    @pl.when(i == 0)
    def _():
        fetch(i)

    @pl.when(i + 1 < pl.num_programs(0))
    def _():
        fetch(i + 1)

    for e in range(N_EXPERTS):
        def wait(j, carry, e=e):
            piece(i, e, j).wait()
            return carry
        lax.fori_loop(0, npc_ref[(i + tile0) * 128 + e], wait, 0)

    base = i * (nt * TOP_K)
    runs = buf_ref.at[i & 1]
    for t in range(nt):
        acc = None
        for k in range(TOP_K):
            row = pl.multiple_of(pos_ref[base + t * TOP_K + k], ROW_CHUNKS)
            term = runs[pl.ds(row, ROW_CHUNKS), :] * prep_ref[t * TOP_K + k:t * TOP_K + k + 1, :]
            acc = term if acc is None else acc + term
        moe_ref[t * ROW_CHUNKS:(t + 1) * ROW_CHUNKS, :] = acc
    h3 = h2_ref[...] + _load_row_tiles(moe_ref, nt)
    y_ref[...] = _rms(h3, nf_ref[...])


def _combine(pos_flat, src_tab, npc_tab, tile0, n, ys, h2, prep, nf):
    row = pl.BlockSpec((COMBINE_TILE, D), lambda i, *_: (i, 0))
    return pl.pallas_call(
        functools.partial(_combine_kernel, tile0),
        grid_spec=pltpu.PrefetchScalarGridSpec(
            num_scalar_prefetch=3,
            grid=(n // COMBINE_TILE,),
            in_specs=[pl.BlockSpec(memory_space=pl.ANY), row,
                      pl.BlockSpec((COMBINE_TILE * TOP_K, 128), lambda i, *_: (i + tile0, 0)),
                      pl.BlockSpec((1, D), lambda i, *_: (0, 0))],
            out_specs=row,
            scratch_shapes=[pltpu.VMEM((2, N_EXPERTS * RUN_ROWS * ROW_CHUNKS, 128), F32),
                            pltpu.VMEM((COMBINE_TILE * ROW_CHUNKS, 128), F32),
                            pltpu.SemaphoreType.DMA((2,))],
        ),
        out_shape=jax.ShapeDtypeStruct((n, D), F32),
        compiler_params=_cparams("arbitrary"),
        name="moe_combine",
    )(pos_flat, src_tab, npc_tab, ys, h2, prep, nf)


def _rope_tables(pos):
    half = DH // 2
    inv = np.power(ROPE_BASE, -np.arange(half, dtype=np.float64) / half)
    ang = pos.astype(np.float64)[:, None] * inv[None, :]
    cos = np.concatenate([np.cos(ang), np.cos(ang)], axis=1)
    sin = np.concatenate([-np.sin(ang), np.sin(ang)], axis=1)
    return jnp.asarray(cos, F32), jnp.asarray(sin, F32)


def _retention_tables(c, reps):
    log_g = np.log1p(-np.exp2(-5.0 - np.arange(HEADS, dtype=np.float64)))
    idx = np.arange(c, dtype=np.float64)
    rel = idx[:, None] - idx[None, :]
    dmat = np.where(rel >= 0, np.exp(log_g[:, None, None] * np.maximum(rel, 0.0)), 0.0)
    big = np.zeros((HEADS, c * reps, c * reps))
    for r in range(reps):
        big[:, r * c:(r + 1) * c, r * c:(r + 1) * c] = dmat
    q_in = np.tile(np.exp(log_g[:, None] * (idx + 1.0)), (1, reps))
    k_out = np.tile(np.exp(log_g[:, None] * (c - 1.0 - idx)), (1, reps))
    g_all = np.exp(log_g * c)
    bc = lambda a: jnp.asarray(np.broadcast_to(a[..., None], a.shape + (DH,)), F32)
    return jnp.asarray(big, F32), bc(q_in), bc(k_out), bc(g_all[:, None])


def _block_tri(c, reps, strict=False):
    idx = np.arange(c * reps)
    same = (idx[:, None] // c) == (idx[None, :] // c)
    low = idx[:, None] > idx[None, :] if strict else idx[:, None] >= idx[None, :]
    return jnp.asarray(same & low, BF16), jnp.asarray(same, BF16)


def _expert_schedule(counts, n_rows):
    n_tiles = n_rows // M_TILE
    n_items = n_tiles + N_EXPERTS - 1
    ends = jnp.cumsum(counts)
    starts = ends - counts
    first_tile = starts // M_TILE
    tiles_e = jnp.where(counts > 0, (ends - 1) // M_TILE - first_tile + 1, 0)
    item_end = jnp.cumsum(tiles_e)
    item_start = item_end - tiles_e
    total = item_end[-1]
    i = jnp.arange(n_items, dtype=jnp.int32)
    live = i < total
    ic = jnp.minimum(i, total - 1)
    e = jnp.sum((item_end[None, :] <= ic[:, None]).astype(jnp.int32), axis=1)
    e = jnp.minimum(e, N_EXPERTS - 1)
    hot = e[:, None] == jnp.arange(N_EXPERTS, dtype=jnp.int32)[None, :]
    at_e = lambda table: jnp.sum(jnp.where(hot, table[None, :], 0), axis=1)
    tile = (at_e(first_tile) + ic - at_e(item_start)).astype(jnp.int32)
    lo = jnp.where(live, jnp.maximum(at_e(starts), tile * M_TILE), 0).astype(jnp.int32)
    hi = jnp.where(live, jnp.minimum(at_e(ends), (tile + 1) * M_TILE), 0).astype(jnp.int32)
    shifted = lambda a: jnp.concatenate([jnp.full((1,), -1, jnp.int32), a[:-1]])
    first = (live & (tile != shifted(tile))).astype(jnp.int32)
    fresh = (live & (e != shifted(e))).astype(jnp.int32)
    half = M_TILE // 2
    mode = jnp.where(hi <= lo, 3, jnp.where(hi - tile * M_TILE <= half, 1,
                                            jnp.where(lo - tile * M_TILE >= half, 2, 0))).astype(jnp.int32)
    return tile, e, lo, hi, first, fresh, mode


def kernel(x_prompt, x_sample, state_hgrn, state_ret, cache_mem_k, cache_mem_v, mem_prompt, norm_mix, w_in, hgrn_lb_logits, hgrn_out_norm, ret_out_norm, w_out, norm_x, norm_mem, w_xq, w_mk, w_mv, w_xo, norm_ffn, w_router, b_router, w_gate_up, b_gate_up, w_down, b_down, norm_final):
    bp, tp, _ = x_prompt.shape
    bs, ts, _ = x_sample.shape
    n_p, n_s = bp * tp, bs * ts
    past_len = 16384

    row = lambda a: a.reshape(1, -1).astype(F32)
    win = w_in[0].astype(BF16)
    wout = w_out[0].astype(BF16)
    wxq = w_xq[0].astype(BF16)
    wxo = w_xo[0].astype(BF16)
    wmk = w_mk[0].astype(BF16)
    wmv = w_mv[0].astype(BF16)
    wr = jnp.pad(w_router[0], ((0, 0), (0, 128 - N_EXPERTS))).astype(BF16)
    br = jnp.pad(b_router[0], (0, 128 - N_EXPERTS)).reshape(1, 128).astype(F32)
    bg = b_gate_up[0][:, 0::2].reshape(N_EXPERTS, 1, D)
    bl = b_gate_up[0][:, 1::2].reshape(N_EXPERTS, 1, D)
    bd = b_down[0].reshape(N_EXPERTS, 1, D)
    pw = 256
    perm_np = np.zeros((pw, pw), np.float32)
    perm_np[np.arange(pw), (np.arange(pw) % 2) * (pw // 2) + np.arange(pw) // 2] = 1.0
    perm = jnp.asarray(perm_np, BF16)
    nm, nx, nmem, nffn, nfin = row(norm_mix[0]), row(norm_x[0]), row(norm_mem[0]), row(norm_ffn[0]), row(norm_final)
    gh, gr = row(hgrn_out_norm[0]), row(ret_out_norm[0])
    lbl = hgrn_lb_logits.astype(F32)

    cos_p, sin_p = _rope_tables(np.arange(tp))
    dmat_p, qin_p, kout_p, gall_p = _retention_tables(RET_CHUNK, 1)
    tri_p, _ = _block_tri(HGRN_CHUNK, 1)
    mixed_p, sh_p, sr_p = _mixer_prompt(x_prompt, nm, win, lbl, gh, gr, cos_p, sin_p,
                                        dmat_p, qin_p, kout_p, gall_p, tri_p)
    cos_s, sin_s = _rope_tables(np.tile(past_len + np.arange(ts), S_BATCH))
    dmat_s, qin_s, kout_s, gall_s = _retention_tables(ts, S_BATCH)
    tri_s, ones_s = _block_tri(ts, S_BATCH)
    mixed_s, sh_s, sr_s = _mixer_sample(x_sample, nm, win, lbl, gh, gr, cos_s, sin_s,
                                        dmat_s, qin_s, kout_s, gall_s, tri_s, ones_s,
                                        state_hgrn[0], state_ret[0])

    mk_p, mv_p = _memkv(mem_prompt, nmem, wmk, wmv)
    h2_p, hn_p, lg_p = _xattn_prompt(x_prompt, mixed_p, mk_p, mv_p, wout, nx, wxq, wxo, nffn, wr, br)
    h1_s, q_s = _pre_attn(x_sample.reshape(n_s, D), mixed_s, wout, nx, wxq, F32)
    o_s = _attn_sample(q_s, cache_mem_k[0], cache_mem_v[0], ts)
    h2_s, hn_s, lg_s = _post_attn(h1_s, o_s, wxo, nffn, wr, br)

    stril, _ = _block_tri(R_TILE, 1, strict=True)
    idx, prep, rank, cst, sub, counts = _router(jnp.concatenate([lg_p, lg_s], axis=0), stril)
    counts = counts[0, :N_EXPERTS].astype(jnp.int32)
    offs = jnp.pad(jnp.cumsum(counts) - counts, (0, 128 - N_EXPERTS)).reshape(1, 128).astype(F32)
    n_rows = (n_p + n_s) * TOP_K
    dest, pos, src_tab, npc_tab = _dest(idx, rank, offs, cst, sub, n_rows)
    dest = dest[:, :TOP_K].reshape(-1)
    pos = pos[:, :TOP_K].reshape(-1)
    src_tab, npc_tab = src_tab.reshape(-1), npc_tab.reshape(-1)
    cut = n_p * TOP_K

    xs = _dispatch(dest[:cut], dest[cut:], hn_p, hn_s)
    ys = _experts(_expert_schedule(counts, n_rows), xs, w_gate_up[0], bg, bl, w_down[0], bd, perm)
    y_p = _combine(pos[:cut], src_tab, npc_tab, 0, n_p, ys, h2_p, prep, nfin)
    y_s = _combine(pos[cut:], src_tab, npc_tab, n_p // COMBINE_TILE, n_s, ys, h2_s, prep, nfin)

    return (y_p.reshape(bp, tp, D), y_s.reshape(bs, ts, D),
            sh_p[None], sr_p[None], mk_p[None], mv_p[None], sh_s[None], sr_s[None])
```

```python
import functools

import numpy as np
import jax
import jax.numpy as jnp
from jax import lax
from jax.experimental import pallas as pl
from jax.experimental.pallas import tpu as pltpu

F32 = jnp.float32
BF16 = jnp.bfloat16

D = 1024
HEADS = 4
DH = 128
GW = HEADS * DH
IN_COLS = 8 * GW
ROPE_BASE = 10000.0
N_MEM = 256
X_HEADS = 4
X_DH = D // X_HEADS
N_EXPERTS = 32
TOP_K = 4
LIMIT = 7.0
ALPHA = 1.702
EPS = 1e-6
EXP_CLAMP = 80.0

HGRN_CHUNK = 64
RET_CHUNK = 128
T_TILE = 512
S_BATCH = 8
ROW_TILE = 512
Q_TILE = 1024
A_BATCH = 8
R_TILE = 1024
M_TILE = 512
DISPATCH_TILE = 256
COMBINE_TILE = 128
RUN_PIECE = 16
RUN_ROWS = COMBINE_TILE + RUN_PIECE
VMEM_LIMIT = 56 * 1024 * 1024


def _cparams(*sem):
    return pltpu.CompilerParams(dimension_semantics=sem, vmem_limit_bytes=VMEM_LIMIT)


def _dot(a, b):
    return jnp.dot(a.astype(BF16), b.astype(BF16), preferred_element_type=F32)


def _dot_nt(a, b):
    return lax.dot_general(a.astype(BF16), b.astype(BF16), (((1,), (1,)), ((), ())),
                           preferred_element_type=F32)


def _dot_tn(a, b):
    return lax.dot_general(a, b, (((0,), (0,)), ((), ())), preferred_element_type=F32)


def _rms(x, g):
    return x * lax.rsqrt(jnp.mean(x * x, axis=-1, keepdims=True) + EPS) * g


ROW_CHUNKS = D // 128


def _load_row_tiles(ref, n, first=0):
    base = first * ROW_CHUNKS
    return jnp.concatenate([ref[pl.ds(base + c, n, stride=ROW_CHUNKS), :] for c in range(ROW_CHUNKS)], axis=1)


def _store_row_tiles(ref, x, first=0):
    n = x.shape[0]
    base = first * ROW_CHUNKS
    for c in range(ROW_CHUNKS):
        ref[pl.ds(base + c, n, stride=ROW_CHUNKS), :] = x[:, c * 128:(c + 1) * 128]


def _cumsum_rows(tri, v):
    hi = v.astype(BF16)
    lo = (v - hi.astype(F32)).astype(BF16)
    return (jnp.dot(tri, hi, preferred_element_type=F32)
            + jnp.dot(tri, lo, preferred_element_type=F32))


def _lower_bound(lbl_ref):
    l = lbl_ref[...]
    m = jnp.max(l, axis=0, keepdims=True)
    e = jnp.exp(l - m)
    return e[0:1, :] / jnp.sum(e, axis=0, keepdims=True)


def _row_to_col(row):
    n = row.shape[1]
    eye = lax.broadcasted_iota(jnp.int32, (n, n), 0) == lax.broadcasted_iota(jnp.int32, (n, n), 1)
    return jnp.sum(jnp.where(eye, jnp.broadcast_to(row, (n, n)), 0.0), axis=1, keepdims=True)


def _rotary(x, cs, sn):
    return x * cs + pltpu.roll(x, DH // 2, 1) * sn


def _hgrn_out(o, gh, gate):
    on = o * lax.rsqrt(jnp.mean(o * o, axis=-1, keepdims=True) + EPS) * gh
    return on * jax.nn.sigmoid(gate)


def _ret_out(o, gr, gate):
    mu = jnp.mean(o, axis=-1, keepdims=True)
    xc = o - mu
    var = jnp.mean(xc * xc, axis=-1, keepdims=True)
    return xc * lax.rsqrt(var + EPS) * gr * (gate * jax.nn.sigmoid(gate))


def _mixer_prompt_kernel(x_ref, nm_ref, win_ref, lbl_ref, gh_ref, gr_ref, cos_ref, sin_ref,
                         dmat_ref, qin_ref, kout_ref, gall_ref, tri_ref,
                         mixed_ref, sh_ref, sr_ref, proj_ref, f_ref, lc_ref):
    @pl.when(pl.program_id(1) == 0)
    def _():
        sh_ref[...] = jnp.zeros_like(sh_ref)
        sr_ref[...] = jnp.zeros_like(sr_ref)

    xn = _rms(x_ref[0], nm_ref[...]).astype(BF16)
    proj_ref[...] = jnp.dot(xn, win_ref[...], preferred_element_type=F32)
    lb = _lower_bound(lbl_ref)
    gh = gh_ref[...]
    gr = gr_ref[...]
    c = HGRN_CHUNK
    causal = (lax.broadcasted_iota(jnp.int32, (c, c), 0) >= lax.broadcasted_iota(jnp.int32, (c, c), 1))

    def gate_step(ci, worst):
        rows = pl.ds(pl.multiple_of(ci * c, c), c)
        f = lb + (1.0 - lb) * jax.nn.sigmoid(proj_ref[rows, GW:2 * GW])
        lc = _cumsum_rows(tri_ref[...], jnp.log(f))
        f_ref[rows, :] = f
        lc_ref[rows, :] = lc
        return jnp.maximum(worst, jnp.max(jnp.abs(lc - lc[c // 2 - 1:c // 2, :])))

    worst = lax.fori_loop(0, T_TILE // c, gate_step, jnp.float32(0.0), unroll=True)

    def scores_factored(hq, kk, lc, h, r0):
        sl = slice(h * DH, (h + 1) * DH)
        mid = lc[c // 2 - 1:c // 2, sl]
        qd = hq[:, sl] * jnp.exp(lc[:, sl] - mid)
        kd = kk[:, sl] * jnp.exp(mid - lc[:, sl])
        return jnp.where(causal, _dot_nt(qd, kd), 0.0)

    def scores_termwise(hq, kk, lc, h, r0):
        sl = slice(h * DH, (h + 1) * DH)
        col_id = lax.broadcasted_iota(jnp.int32, (c, c), 1)

        sub_id = lax.broadcasted_iota(jnp.int32, (8, DH), 0)

        def row_of(ref, s):
            grp = ref[pl.ds(pl.multiple_of(r0 + lax.shift_left(lax.shift_right_logical(s, 3), 3), 8), 8), sl]
            return jnp.sum(jnp.where(sub_id == (s & 7), grp, 0.0), axis=0, keepdims=True)

        def body(s, a):
            lcs = row_of(lc_ref, s)
            ks_row = 1.0 - row_of(f_ref, s)
            w = jnp.exp(jnp.minimum(lc[:, sl] - lcs, 0.0))
            col = jnp.sum(hq[:, sl] * ks_row * w, axis=1, keepdims=True)
            return a + jnp.where(col_id == s, col, 0.0)

        return jnp.where(causal, lax.fori_loop(0, c, body, jnp.zeros((c, c), F32)), 0.0)

    def hgrn_step(scores, ci, carry):
        r0 = pl.multiple_of(ci * c, c)
        rows = pl.ds(r0, c)
        hq = proj_ref[rows, 0:GW] * (DH ** -0.5)
        kk = 1.0 - f_ref[rows, :]
        lc = lc_ref[rows, :]
        last = lc[c - 1:c, :]
        qs = hq * jnp.exp(lc)
        ks = kk * jnp.exp(last - lc)
        dl = jnp.exp(last)
        for h in range(HEADS):
            sl = slice(h * DH, (h + 1) * DH)
            v = proj_ref[rows, 2 * GW + h * DH:2 * GW + (h + 1) * DH].astype(BF16)
            a = scores(hq, kk, lc, h, r0)
            s_old = sh_ref[0, h]
            o = _dot(a, v) + _dot(qs[:, sl], s_old)
            sh_ref[0, h] = _row_to_col(dl[:, sl]) * s_old + _dot_tn(ks[:, sl].astype(BF16), v)
            gate = proj_ref[rows, 3 * GW + h * DH:3 * GW + (h + 1) * DH]
            mixed_ref[0, rows, sl] = _hgrn_out(o, gh, gate).astype(BF16)
        return carry

    @pl.when(worst <= EXP_CLAMP)
    def _():
        lax.fori_loop(0, T_TILE // c, functools.partial(hgrn_step, scores_factored), 0, unroll=True)

    @pl.when(jnp.logical_not(worst <= EXP_CLAMP))
    def _():
        lax.fori_loop(0, T_TILE // c, functools.partial(hgrn_step, scores_termwise), 0)

    cr = RET_CHUNK

    def ret_step(ci, carry):
        rows = pl.ds(pl.multiple_of(ci * cr, cr), cr)
        cs = cos_ref[rows, :]
        sn = sin_ref[rows, :]
        for h in range(HEADS):
            base = 4 * GW + h * DH
            q = _rotary(proj_ref[rows, base:base + DH], cs, sn)
            k = _rotary(proj_ref[rows, base + GW:base + GW + DH], cs, sn) * (DH ** -0.5)
            v = proj_ref[rows, base + 2 * GW:base + 2 * GW + DH].astype(BF16)
            sc = _dot_nt(q, k) * dmat_ref[h]
            s_old = sr_ref[0, h]
            o = _dot(sc, v) + _dot(q * qin_ref[h], s_old)
            sr_ref[0, h] = gall_ref[h] * s_old + _dot_tn((k * kout_ref[h]).astype(BF16), v)
            gate = proj_ref[rows, base + 3 * GW:base + 3 * GW + DH]
            mixed_ref[0, rows, GW + h * DH:GW + (h + 1) * DH] = _ret_out(o, gr, gate).astype(BF16)
        return carry

    lax.fori_loop(0, T_TILE // cr, ret_step, 0, unroll=True)


def _mixer_prompt(x, nm, win, lbl, gh, gr, cos, sin, dmat, qin, kout, gall, tri):
    b, t, _ = x.shape
    full = lambda shape: pl.BlockSpec(shape, lambda i, j: (0,) * len(shape))
    state = pl.BlockSpec((1, HEADS, DH, DH), lambda i, j: (i, 0, 0, 0))
    return pl.pallas_call(
        _mixer_prompt_kernel,
        grid=(b, t // T_TILE),
        in_specs=[
            pl.BlockSpec((1, T_TILE, D), lambda i, j: (i, j, 0)),
            full((1, D)), full((D, IN_COLS)), full(lbl.shape), full((1, DH)), full((1, DH)),
            pl.BlockSpec((T_TILE, DH), lambda i, j: (j, 0)),
            pl.BlockSpec((T_TILE, DH), lambda i, j: (j, 0)),
            full(dmat.shape), full(qin.shape), full(kout.shape), full(gall.shape), full(tri.shape),
        ],
        out_specs=[pl.BlockSpec((1, T_TILE, D), lambda i, j: (i, j, 0)), state, state],
        out_shape=[jax.ShapeDtypeStruct((b, t, D), BF16),
                   jax.ShapeDtypeStruct((b, HEADS, DH, DH), F32),
                   jax.ShapeDtypeStruct((b, HEADS, DH, DH), F32)],
        scratch_shapes=[pltpu.VMEM((T_TILE, IN_COLS), F32), pltpu.VMEM((T_TILE, GW), F32),
                        pltpu.VMEM((T_TILE, GW), F32)],
        compiler_params=_cparams("parallel", "arbitrary"),
        name="mixer_prompt",
    )(x, nm, win, lbl, gh, gr, cos, sin, dmat, qin, kout, gall, tri)


def _mixer_sample_kernel(x_ref, nm_ref, win_ref, lbl_ref, gh_ref, gr_ref, cos_ref, sin_ref,
                         dmat_ref, qin_ref, kout_ref, gall_ref, tri_ref, ones_ref,
                         shin_ref, srin_ref, mixed_ref, sh_ref, sr_ref, proj_ref):
    nb, ts, _ = x_ref.shape
    rows_n = nb * ts
    xn = _rms(x_ref[...].reshape(rows_n, D), nm_ref[...]).astype(BF16)
    proj_ref[...] = jnp.dot(xn, win_ref[...], preferred_element_type=F32)
    lb = _lower_bound(lbl_ref)
    gh = gh_ref[...]
    gr = gr_ref[...]
    hq = proj_ref[:, 0:GW] * (DH ** -0.5)
    f = lb + (1.0 - lb) * jax.nn.sigmoid(proj_ref[:, GW:2 * GW])
    kk = 1.0 - f
    lf = jnp.log(f)
    lc = _cumsum_rows(tri_ref[...], lf)
    last = _cumsum_rows(ones_ref[...], lf)
    qd = hq * jnp.exp(lc)
    ks = kk * jnp.exp(last - lc)
    dl = jnp.exp(last)
    t_pos = lax.broadcasted_iota(jnp.int32, (nb, ts, 1), 1)

    def intra_termwise(h):
        sl = slice(h * DH, (h + 1) * DH)
        q3 = hq[:, sl].reshape(nb, ts, DH)
        k3 = kk[:, sl].reshape(nb, ts, DH)
        lc3 = lc[:, sl].reshape(nb, ts, DH)
        v3 = proj_ref[:, 2 * GW + h * DH:2 * GW + (h + 1) * DH].reshape(nb, ts, DH)
        o3 = jnp.zeros((nb, ts, DH), F32)
        for s in range(ts):
            w = jnp.exp(jnp.minimum(lc3 - lc3[:, s:s + 1, :], 0.0))
            score = jnp.sum(q3 * k3[:, s:s + 1, :] * w, axis=-1, keepdims=True)
            o3 = o3 + jnp.where(t_pos >= s, score, 0.0) * v3[:, s:s + 1, :]
        return o3

    for h in range(HEADS):
        sl = slice(h * DH, (h + 1) * DH)
        v = proj_ref[:, 2 * GW + h * DH:2 * GW + (h + 1) * DH]
        o_intra = intra_termwise(h)
        gate = proj_ref[:, 3 * GW + h * DH:3 * GW + (h + 1) * DH]
        for b in range(nb):
            r = slice(b * ts, (b + 1) * ts)
            s_old = shin_ref[b, h]
            o = o_intra[b] + _dot(qd[r, sl], s_old)
            sh_ref[b, h] = (_row_to_col(dl[b * ts:b * ts + 1, sl]) * s_old
                            + _dot_tn(ks[r, sl], v[r]))
            mixed_ref[r, sl] = _hgrn_out(o, gh, gate[r])

    cs = cos_ref[...]
    sn = sin_ref[...]
    for h in range(HEADS):
        base = 4 * GW + h * DH
        q = _rotary(proj_ref[:, base:base + DH], cs, sn)
        k = _rotary(proj_ref[:, base + GW:base + GW + DH], cs, sn) * (DH ** -0.5)
        v = proj_ref[:, base + 2 * GW:base + 2 * GW + DH]
        sc = _dot_nt(q, k) * dmat_ref[h]
        o_intra = _dot(sc, v)
        qi = q * qin_ref[h]
        ko = k * kout_ref[h]
        gate = proj_ref[:, base + 3 * GW:base + 3 * GW + DH]
        for b in range(nb):
            r = slice(b * ts, (b + 1) * ts)
            s_old = srin_ref[b, h]
            o = o_intra[r] + _dot(qi[r], s_old)
            sr_ref[b, h] = gall_ref[h] * s_old + _dot_tn(ko[r], v[r])
            mixed_ref[r, GW + h * DH:GW + (h + 1) * DH] = _ret_out(o, gr, gate[r])


def _mixer_sample(x, nm, win, lbl, gh, gr, cos, sin, dmat, qin, kout, gall, tri, ones, sh, sr):
    b, t, _ = x.shape
    nb = S_BATCH
    full = lambda shape: pl.BlockSpec(shape, lambda i: (0,) * len(shape))
    state = pl.BlockSpec((nb, HEADS, DH, DH), lambda i: (i, 0, 0, 0))
    return pl.pallas_call(
        _mixer_sample_kernel,
        grid=(b // nb,),
        in_specs=[
            pl.BlockSpec((nb, t, D), lambda i: (i, 0, 0)),
            full((1, D)), full((D, IN_COLS)), full(lbl.shape), full((1, DH)), full((1, DH)),
            full(cos.shape), full(sin.shape),
            full(dmat.shape), full(qin.shape), full(kout.shape), full(gall.shape),
            full(tri.shape), full(ones.shape), state, state,
        ],
        out_specs=[pl.BlockSpec((nb * t, D), lambda i: (i, 0)), state, state],
        out_shape=[jax.ShapeDtypeStruct((b * t, D), F32),
                   jax.ShapeDtypeStruct(sh.shape, F32),
                   jax.ShapeDtypeStruct(sr.shape, F32)],
        scratch_shapes=[pltpu.VMEM((nb * t, IN_COLS), F32)],
        compiler_params=_cparams("parallel"),
        name="mixer_sample",
    )(x, nm, win, lbl, gh, gr, cos, sin, dmat, qin, kout, gall, tri, ones, sh, sr)


def _memkv_kernel(mem_ref, g_ref, wk_ref, wv_ref, k_ref, v_ref):
    mn = _rms(mem_ref[0], g_ref[...]).astype(BF16)
    k = jnp.dot(mn, wk_ref[...], preferred_element_type=F32)
    v = jnp.dot(mn, wv_ref[...], preferred_element_type=F32)
    for h in range(X_HEADS):
        k_ref[0, h] = k[:, h * X_DH:(h + 1) * X_DH]
        v_ref[0, h] = v[:, h * X_DH:(h + 1) * X_DH]


def _memkv(mem, g, wk, wv):
    b = mem.shape[0]
    full = lambda shape: pl.BlockSpec(shape, lambda i: (0,) * len(shape))
    kv = pl.BlockSpec((1, X_HEADS, N_MEM, X_DH), lambda i: (i, 0, 0, 0))
    shape = jax.ShapeDtypeStruct((b, X_HEADS, N_MEM, X_DH), F32)
    return pl.pallas_call(
        _memkv_kernel,
        grid=(b,),
        in_specs=[pl.BlockSpec((1, N_MEM, D), lambda i: (i, 0, 0)), full((1, D)), full((D, D)), full((D, D))],
        out_specs=[kv, kv],
        out_shape=[shape, shape],
        compiler_params=_cparams("parallel"),
        name="mem_kv",
    )(mem, g, wk, wv)


def _pre_attn_kernel(x_ref, mixed_ref, wout_ref, nx_ref, wxq_ref, h1_ref, q_ref):
    h1 = x_ref[...] + _dot(mixed_ref[...], wout_ref[...])
    h1_ref[...] = h1
    hn = _rms(h1, nx_ref[...])
    q_ref[...] = _dot(hn, wxq_ref[...]).astype(q_ref.dtype)


def _pre_attn(x, mixed, wout, nx, wxq, q_dtype):
    n = x.shape[0]
    row = pl.BlockSpec((ROW_TILE, D), lambda i: (i, 0))
    full = lambda shape: pl.BlockSpec(shape, lambda i: (0,) * len(shape))
    return pl.pallas_call(
        _pre_attn_kernel,
        grid=(n // ROW_TILE,),
        in_specs=[row, row, full((D, D)), full((1, D)), full((D, D))],
        out_specs=[row, row],
        out_shape=[jax.ShapeDtypeStruct((n, D), F32), jax.ShapeDtypeStruct((n, D), q_dtype)],
        compiler_params=_cparams("parallel"),
        name="pre_attn",
    )(x, mixed, wout, nx, wxq)


def _post_attn_kernel(h1_ref, o_ref, wxo_ref, nf_ref, wr_ref, br_ref, h2_ref, hn_ref, lg_ref):
    h2 = h1_ref[...] + _dot(o_ref[...], wxo_ref[...])
    h2_ref[...] = h2
    hn = _rms(h2, nf_ref[...])
    _store_row_tiles(hn_ref, hn)
    lg_ref[...] = _dot(hn, wr_ref[...]) + br_ref[...]


def _post_attn(h1, o, wxo, nf, wr, br):
    n = h1.shape[0]
    row = pl.BlockSpec((ROW_TILE, D), lambda i: (i, 0))
    full = lambda shape: pl.BlockSpec(shape, lambda i: (0,) * len(shape))
    return pl.pallas_call(
        _post_attn_kernel,
        grid=(n // ROW_TILE,),
        in_specs=[row, row, full((D, D)), full((1, D)), full((D, 128)), full((1, 128))],
        out_specs=[row, pl.BlockSpec((ROW_TILE * ROW_CHUNKS, 128), lambda i: (i, 0)),
                   pl.BlockSpec((ROW_TILE, 128), lambda i: (i, 0))],
        out_shape=[jax.ShapeDtypeStruct((n, D), F32), jax.ShapeDtypeStruct((n * ROW_CHUNKS, 128), F32),
                   jax.ShapeDtypeStruct((n, 128), F32)],
        compiler_params=_cparams("parallel"),
        name="post_attn",
    )(h1, o, wxo, nf, wr, br)


def _softmax_rows(s):
    m = jnp.max(s, axis=-1, keepdims=True)
    e = jnp.exp(s - m)
    return e / jnp.sum(e, axis=-1, keepdims=True)


def _xattn_prompt_kernel(x_ref, mixed_ref, k_ref, v_ref, wout_ref, nx_ref, wxq_ref, wxo_ref,
                         nf_ref, wr_ref, br_ref, h2_ref, hn_ref, lg_ref, o_scr):
    h1 = x_ref[0] + _dot(mixed_ref[0], wout_ref[...])
    q = _dot(_rms(h1, nx_ref[...]), wxq_ref[...]).astype(BF16)
    for h in range(X_HEADS):
        sl = slice(h * X_DH, (h + 1) * X_DH)
        s = _dot_nt(q[:, sl], k_ref[0, h]) * (X_DH ** -0.5)
        o_scr[:, sl] = _dot(_softmax_rows(s), v_ref[0, h]).astype(BF16)
    h2 = h1 + _dot(o_scr[...], wxo_ref[...])
    h2_ref[...] = h2
    hn = _rms(h2, nf_ref[...])
    _store_row_tiles(hn_ref, hn)
    lg_ref[...] = _dot(hn, wr_ref[...]) + br_ref[...]


def _xattn_prompt(x, mixed, mk, mv, wout, nx, wxq, wxo, nf, wr, br):
    b, seq, _ = x.shape
    per_seq = seq // Q_TILE
    full = lambda shape: pl.BlockSpec(shape, lambda i, j: (0,) * len(shape))
    tok = pl.BlockSpec((1, Q_TILE, D), lambda i, j: (i, j, 0))
    kv = pl.BlockSpec((1, X_HEADS, N_MEM, X_DH), lambda i, j: (i, 0, 0, 0))
    row = pl.BlockSpec((Q_TILE, D), lambda i, j: (i * per_seq + j, 0))
    n = b * seq
    return pl.pallas_call(
        _xattn_prompt_kernel,
        grid=(b, per_seq),
        in_specs=[tok, tok, kv, kv, full((D, D)), full((1, D)), full((D, D)), full((D, D)),
                  full((1, D)), full((D, 128)), full((1, 128))],
        out_specs=[row, pl.BlockSpec((Q_TILE * ROW_CHUNKS, 128), lambda i, j: (i * per_seq + j, 0)),
                   pl.BlockSpec((Q_TILE, 128), lambda i, j: (i * per_seq + j, 0))],
        out_shape=[jax.ShapeDtypeStruct((n, D), F32), jax.ShapeDtypeStruct((n * ROW_CHUNKS, 128), F32),
                   jax.ShapeDtypeStruct((n, 128), F32)],
        scratch_shapes=[pltpu.VMEM((Q_TILE, D), BF16)],
        compiler_params=_cparams("parallel", "arbitrary"),
        name="xattn_prompt",
    )(x, mixed, mk, mv, wout, nx, wxq, wxo, nf, wr, br)


def _attn_sample_kernel(q_ref, k_ref, v_ref, o_ref):
    nb = k_ref.shape[0]
    ts = q_ref.shape[0] // nb
    units = [(b, h) for b in range(nb) for h in range(X_HEADS)]
    s = jnp.concatenate(
        [_dot_nt(q_ref[b * ts:(b + 1) * ts, h * X_DH:(h + 1) * X_DH], k_ref[b, h]) for b, h in units], axis=0)
    p = _softmax_rows(s * (X_DH ** -0.5))
    for i, (b, h) in enumerate(units):
        o_ref[b * ts:(b + 1) * ts, h * X_DH:(h + 1) * X_DH] = _dot(p[i * ts:(i + 1) * ts], v_ref[b, h])


def _attn_sample(q, ck, cv, ts):
    n = q.shape[0]
    nb = A_BATCH
    kv = pl.BlockSpec((nb, X_HEADS, N_MEM, X_DH), lambda i: (i, 0, 0, 0))
    qs = pl.BlockSpec((nb * ts, D), lambda i: (i, 0))
    return pl.pallas_call(
        _attn_sample_kernel,
        grid=(n // (nb * ts),),
        in_specs=[qs, kv, kv],
        out_specs=qs,
        out_shape=jax.ShapeDtypeStruct((n, D), F32),
        compiler_params=_cparams("parallel"),
        name="attn_sample",
    )(q, ck, cv)


def _router_kernel(lg_ref, stril_ref, idx_ref, prep_ref, rank_ref, cst_ref, sub_ref, cnt_ref, carry_ref):
    @pl.when(pl.program_id(0) == 0)
    def _():
        carry_ref[...] = jnp.zeros_like(carry_ref)

    n = lg_ref.shape[0]
    lane = lax.broadcasted_iota(jnp.int32, (n, 128), 1)
    lane_f = lane.astype(F32)
    l = jnp.where(lane < N_EXPERTS, lg_ref[...], -jnp.inf)
    tops, idxs, hots = [], [], []
    for _ in range(TOP_K):
        m = jnp.max(l, axis=1, keepdims=True)
        idx = jnp.min(jnp.where(l == m, lane_f, 128.0), axis=1, keepdims=True)
        hot = lane_f == idx
        l = jnp.where(hot, -jnp.inf, l)
        tops.append(m)
        idxs.append(idx)
        hots.append(hot)
    sel = jnp.where(hots[0] | hots[1] | hots[2] | hots[3], 1.0, 0.0)
    before = jnp.dot(stril_ref[...], sel.astype(BF16), preferred_element_type=F32) + carry_ref[...]
    carry_ref[...] += jnp.sum(sel, axis=0, keepdims=True)
    cnt_ref[...] = carry_ref[...]
    subs = n // COMBINE_TILE
    cst_ref[...] = jnp.concatenate([before[j * COMBINE_TILE:j * COMBINE_TILE + 1, :] for j in range(subs)], axis=0)
    sub_ref[...] = jnp.concatenate(
        [jnp.sum(sel[j * COMBINE_TILE:(j + 1) * COMBINE_TILE, :], axis=0, keepdims=True) for j in range(subs)], axis=0)
    es = [jnp.exp(t - tops[0]) for t in tops]
    den = es[0] + es[1] + es[2] + es[3]
    idx_out = jnp.zeros((n, 128), jnp.int32)
    rank_out = jnp.zeros((n, 128), jnp.int32)
    for k in range(TOP_K):
        rank = jnp.sum(jnp.where(hots[k], before, 0.0), axis=1, keepdims=True).astype(jnp.int32)
        idx_out = jnp.where(lane == k, idxs[k].astype(jnp.int32), idx_out)
        rank_out = jnp.where(lane == k, rank, rank_out)
        prep_ref[pl.ds(k, n, stride=TOP_K), :] = jnp.broadcast_to(es[k] / den, (n, 128))
    idx_ref[...] = idx_out
    rank_ref[...] = rank_out


def _router(logits, stril):
    n = logits.shape[0]
    subs = R_TILE // COMBINE_TILE
    row = pl.BlockSpec((R_TILE, 128), lambda i: (i, 0))
    one = pl.BlockSpec((1, 128), lambda i: (0, 0))
    sub = pl.BlockSpec((subs, 128), lambda i: (i, 0))
    n_sub = n // COMBINE_TILE
    return pl.pallas_call(
        _router_kernel,
        grid=(n // R_TILE,),
        in_specs=[row, pl.BlockSpec((R_TILE, R_TILE), lambda i: (0, 0))],
        out_specs=[row, pl.BlockSpec((R_TILE * TOP_K, 128), lambda i: (i, 0)), row, sub, sub, one],
        out_shape=[jax.ShapeDtypeStruct((n, 128), jnp.int32), jax.ShapeDtypeStruct((n * TOP_K, 128), F32),
                   jax.ShapeDtypeStruct((n, 128), jnp.int32), jax.ShapeDtypeStruct((n_sub, 128), F32),
                   jax.ShapeDtypeStruct((n_sub, 128), F32), jax.ShapeDtypeStruct((1, 128), F32)],
        scratch_shapes=[pltpu.VMEM((1, 128), F32)],
        compiler_params=_cparams("arbitrary"),
        name="router",
    )(logits, stril)


def _dest_kernel(n_rows, idx_ref, rank_ref, offs_ref, cst_ref, sub_ref, dest_ref, pos_ref, src_ref, npc_ref):
    n = idx_ref.shape[0]
    subs = n // COMBINE_TILE
    lane = lax.broadcasted_iota(jnp.int32, (n, 128), 1)
    lane_row = lax.broadcasted_iota(jnp.int32, (subs, 128), 1).astype(F32)
    offs = offs_ref[...]
    before = cst_ref[...]
    start = offs + before
    pieces = jnp.floor((sub_ref[...] + (RUN_PIECE - 1)) * (1.0 / RUN_PIECE))
    shift = jnp.maximum(start + pieces * RUN_PIECE - n_rows, 0.0)
    src_ref[...] = ((start - shift) * ROW_CHUNKS).astype(jnp.int32)
    npc_ref[...] = pieces.astype(jnp.int32)
    slot = lane_row * RUN_ROWS - before + shift
    slot = jnp.concatenate([jnp.broadcast_to(slot[j:j + 1, :], (COMBINE_TILE, 128)) for j in range(subs)], axis=0)
    idx = idx_ref[...]
    d_out = jnp.zeros((n, 128), F32)
    p_out = jnp.zeros((n, 128), F32)
    for k in range(TOP_K):
        hot = lane == idx[:, k:k + 1]
        d_out = jnp.where(lane == k, jnp.sum(jnp.where(hot, offs, 0.0), axis=1, keepdims=True), d_out)
        p_out = jnp.where(lane == k, jnp.sum(jnp.where(hot, slot, 0.0), axis=1, keepdims=True), p_out)
    rank = rank_ref[...]
    dest_ref[...] = (d_out.astype(jnp.int32) + rank) * ROW_CHUNKS
    pos_ref[...] = (p_out.astype(jnp.int32) + rank) * ROW_CHUNKS


def _dest(idx, rank, offs, cst, sub, n_rows):
    n = idx.shape[0]
    subs = R_TILE // COMBINE_TILE
    row = pl.BlockSpec((R_TILE, 128), lambda i: (i, 0))
    tab = pl.BlockSpec((subs, 128), lambda i: (i, 0))
    n_sub = n // COMBINE_TILE
    return pl.pallas_call(
        functools.partial(_dest_kernel, n_rows),
        grid=(n // R_TILE,),
        in_specs=[row, row, pl.BlockSpec((1, 128), lambda i: (0, 0)), tab, tab],
        out_specs=[row, row, tab, tab],
        out_shape=[jax.ShapeDtypeStruct((n, 128), jnp.int32), jax.ShapeDtypeStruct((n, 128), jnp.int32),
                   jax.ShapeDtypeStruct((n_sub, 128), jnp.int32), jax.ShapeDtypeStruct((n_sub, 128), jnp.int32)],
        compiler_params=_cparams("parallel"),
        name="route_dest",
    )(idx, rank, offs, cst, sub)


def _dispatch_kernel(p_tiles, dest_p_ref, dest_s_ref, hn_p_ref, hn_s_ref, xs_ref, sem):
    nt = DISPATCH_TILE
    i = pl.program_id(0)

    def scatter(dest_ref, hn_ref, tile):
        base = tile * (nt * TOP_K)

        def copy(t, k):
            dst = pl.multiple_of(dest_ref[base + t * TOP_K + k], ROW_CHUNKS)
            return pltpu.make_async_copy(hn_ref.at[pl.ds(t * ROW_CHUNKS, ROW_CHUNKS)],
                                         xs_ref.at[pl.ds(dst, ROW_CHUNKS)], sem)

        def start(t, carry):
            for k in range(TOP_K):
                copy(t, k).start(priority=k % 2)
            return carry

        def wait(t, carry):
            for k in range(TOP_K):
                copy(t, k).wait()
            return carry

        for t in range(nt):
            start(t, 0)
        for t in range(nt):
            wait(t, 0)

    @pl.when(i < p_tiles)
    def _():
        scatter(dest_p_ref, hn_p_ref, i)

    @pl.when(i >= p_tiles)
    def _():
        scatter(dest_s_ref, hn_s_ref, i - p_tiles)


def _dispatch(dest_p, dest_s, hn_p, hn_s):
    p_tiles = hn_p.shape[0] // (DISPATCH_TILE * ROW_CHUNKS)
    s_tiles = hn_s.shape[0] // (DISPATCH_TILE * ROW_CHUNKS)
    n_rows = (hn_p.shape[0] + hn_s.shape[0]) * TOP_K
    return pl.pallas_call(
        functools.partial(_dispatch_kernel, p_tiles),
        grid_spec=pltpu.PrefetchScalarGridSpec(
            num_scalar_prefetch=2,
            grid=(p_tiles + s_tiles,),
            in_specs=[pl.BlockSpec((DISPATCH_TILE * ROW_CHUNKS, 128),
                                   lambda i, dp, ds: (jnp.minimum(i, p_tiles - 1), 0)),
                      pl.BlockSpec((DISPATCH_TILE * ROW_CHUNKS, 128),
                                   lambda i, dp, ds: (jnp.maximum(i - p_tiles, 0), 0))],
            out_specs=pl.BlockSpec(memory_space=pl.ANY),
            scratch_shapes=[pltpu.SemaphoreType.DMA],
        ),
        out_shape=jax.ShapeDtypeStruct((n_rows, 128), F32),
        compiler_params=_cparams("arbitrary"),
        name="moe_dispatch",
    )(dest_p, dest_s, hn_p, hn_s)


def _experts_kernel(tile_ref, exp_ref, lo_ref, hi_ref, first_ref, fresh_ref, mode_ref,
                    xs_ref, wgu_ref, bg_ref, bl_ref, wd_ref, bd_ref, perm_ref, ys_ref,
                    wg_s, wl_s, wd_s):
    i = pl.program_id(0)
    lo = lo_ref[i]
    hi = hi_ref[i]
    half = M_TILE // 2

    @pl.when(fresh_ref[i] == 1)
    def _():
        pw = perm_ref.shape[0]
        for c in range(2 * D // pw):
            blk = wgu_ref[0, :, c * pw:(c + 1) * pw].astype(BF16)
            sep = jnp.dot(blk, perm_ref[...], preferred_element_type=F32)
            wg_s[:, c * pw // 2:(c + 1) * pw // 2] = sep[:, :pw // 2].astype(BF16)
            wl_s[:, c * pw // 2:(c + 1) * pw // 2] = sep[:, pw // 2:].astype(BF16)
        wd_s[...] = wd_ref[0].astype(BF16)

    def rows_of_tile(r0, n):
        x = _load_row_tiles(xs_ref, n, r0).astype(BF16)
        glu = jnp.minimum(jnp.dot(x, wg_s[...], preferred_element_type=F32) + bg_ref[0], LIMIT)
        lin = jnp.clip(jnp.dot(x, wl_s[...], preferred_element_type=F32) + bl_ref[0], -LIMIT, LIMIT)
        hmid = glu * jax.nn.sigmoid(ALPHA * glu) * (lin + 1.0)
        y = _dot(hmid, wd_s[...]) + bd_ref[0]
        row = tile_ref[i] * M_TILE + r0 + lax.broadcasted_iota(jnp.int32, (n, 1), 0)
        mine = (row >= lo) & (row < hi)

        @pl.when(first_ref[i] == 1)
        def _():
            _store_row_tiles(ys_ref, jnp.where(mine, y, 0.0), r0)
            if n < M_TILE:
                other = (half - r0) * ROW_CHUNKS
                ys_ref[other:other + half * ROW_CHUNKS, :] = jnp.zeros((half * ROW_CHUNKS, 128), F32)

        @pl.when(first_ref[i] == 0)
        def _():
            _store_row_tiles(ys_ref, jnp.where(mine, y, _load_row_tiles(ys_ref, n, r0)), r0)

    @pl.when(mode_ref[i] == 0)
    def _():
        rows_of_tile(0, M_TILE)

    @pl.when(mode_ref[i] == 1)
    def _():
        rows_of_tile(0, half)

    @pl.when(mode_ref[i] == 2)
    def _():
        rows_of_tile(half, half)


def _experts(meta, xs, wgu, bg, bl, wd, bd, perm):
    tile, expert, lo, hi, first, fresh, mode = meta
    rows = pl.BlockSpec((M_TILE * ROW_CHUNKS, 128), lambda i, t, e, *_: (t[i], 0))
    bspec = pl.BlockSpec((1, 1, D), lambda i, t, e, *_: (e[i], 0, 0))
    return pl.pallas_call(
        _experts_kernel,
        grid_spec=pltpu.PrefetchScalarGridSpec(
            num_scalar_prefetch=7,
            grid=(tile.shape[0],),
            in_specs=[rows,
                      pl.BlockSpec((1, D, 2 * D), lambda i, t, e, *_: (e[i], 0, 0)), bspec, bspec,
                      pl.BlockSpec((1, D, D), lambda i, t, e, *_: (e[i], 0, 0)), bspec,
                      pl.BlockSpec(perm.shape, lambda i, *_: (0, 0))],
            out_specs=rows,
            scratch_shapes=[pltpu.VMEM((D, D), BF16)] * 3,
        ),
        out_shape=jax.ShapeDtypeStruct(xs.shape, F32),
        compiler_params=_cparams("arbitrary"),
        name="moe_experts",
    )(tile, expert, lo, hi, first, fresh, mode, xs, wgu, bg, bl, wd, bd, perm)


def _combine_kernel(tile0, pos_ref, src_ref, npc_ref, ys_ref, h2_ref, prep_ref, nf_ref, y_ref,
                    buf_ref, moe_ref, sem):
    nt = COMBINE_TILE
    i = pl.program_id(0)
    piece_rows = RUN_PIECE * ROW_CHUNKS

    def piece(tile, e, j):
        tab = (tile + tile0) * 128
        half = tile & 1
        src = pl.multiple_of(src_ref[tab + e] + j * piece_rows, ROW_CHUNKS)
        dst = pl.multiple_of(e * (RUN_ROWS * ROW_CHUNKS) + j * piece_rows, ROW_CHUNKS)
        return pltpu.make_async_copy(ys_ref.at[pl.ds(src, piece_rows)],
                                     buf_ref.at[half, pl.ds(dst, piece_rows)], sem.at[half])

    def fetch(tile):
        for e in range(N_EXPERTS):
            def start(j, carry, e=e):
                piece(tile, e, j).start(priority=e % 2)
                return carry
            lax.fori_loop(0, npc_ref[(tile + tile0) * 128 + e], start, 0)

    @pl.when(i == 0)
    def _():
        fetch(i)

    @pl.when(i + 1 < pl.num_programs(0))
    def _():
        fetch(i + 1)

    for e in range(N_EXPERTS):
        def wait(j, carry, e=e):
            piece(i, e, j).wait()
            return carry
        lax.fori_loop(0, npc_ref[(i + tile0) * 128 + e], wait, 0)

    base = i * (nt * TOP_K)
    runs = buf_ref.at[i & 1]
    for t in range(nt):
        acc = None
        for k in range(TOP_K):
            row = pl.multiple_of(pos_ref[base + t * TOP_K + k], ROW_CHUNKS)
            term = runs[pl.ds(row, ROW_CHUNKS), :] * prep_ref[t * TOP_K + k:t * TOP_K + k + 1, :]
            acc = term if acc is None else acc + term
        moe_ref[t * ROW_CHUNKS:(t + 1) * ROW_CHUNKS, :] = acc
    h3 = h2_ref[...] + _load_row_tiles(moe_ref, nt)
    y_ref[...] = _rms(h3, nf_ref[...])


def _combine(pos_flat, src_tab, npc_tab, tile0, n, ys, h2, prep, nf):
    row = pl.BlockSpec((COMBINE_TILE, D), lambda i, *_: (i, 0))
    return pl.pallas_call(
        functools.partial(_combine_kernel, tile0),
        grid_spec=pltpu.PrefetchScalarGridSpec(
            num_scalar_prefetch=3,
            grid=(n // COMBINE_TILE,),
            in_specs=[pl.BlockSpec(memory_space=pl.ANY), row,
                      pl.BlockSpec((COMBINE_TILE * TOP_K, 128), lambda i, *_: (i + tile0, 0)),
                      pl.BlockSpec((1, D), lambda i, *_: (0, 0))],
            out_specs=row,
            scratch_shapes=[pltpu.VMEM((2, N_EXPERTS * RUN_ROWS * ROW_CHUNKS, 128), F32),
                            pltpu.VMEM((COMBINE_TILE * ROW_CHUNKS, 128), F32),
                            pltpu.SemaphoreType.DMA((2,))],
        ),
        out_shape=jax.ShapeDtypeStruct((n, D), F32),
        compiler_params=_cparams("arbitrary"),
        name="moe_combine",
    )(pos_flat, src_tab, npc_tab, ys, h2, prep, nf)


def _rope_tables(pos):
    half = DH // 2
    inv = np.power(ROPE_BASE, -np.arange(half, dtype=np.float64) / half)
    ang = pos.astype(np.float64)[:, None] * inv[None, :]
    cos = np.concatenate([np.cos(ang), np.cos(ang)], axis=1)
    sin = np.concatenate([-np.sin(ang), np.sin(ang)], axis=1)
    return jnp.asarray(cos, F32), jnp.asarray(sin, F32)


def _retention_tables(c, reps):
    log_g = np.log1p(-np.exp2(-5.0 - np.arange(HEADS, dtype=np.float64)))
    idx = np.arange(c, dtype=np.float64)
    rel = idx[:, None] - idx[None, :]
    dmat = np.where(rel >= 0, np.exp(log_g[:, None, None] * np.maximum(rel, 0.0)), 0.0)
    big = np.zeros((HEADS, c * reps, c * reps))
    for r in range(reps):
        big[:, r * c:(r + 1) * c, r * c:(r + 1) * c] = dmat
    q_in = np.tile(np.exp(log_g[:, None] * (idx + 1.0)), (1, reps))
    k_out = np.tile(np.exp(log_g[:, None] * (c - 1.0 - idx)), (1, reps))
    g_all = np.exp(log_g * c)
    bc = lambda a: jnp.asarray(np.broadcast_to(a[..., None], a.shape + (DH,)), F32)
    return jnp.asarray(big, F32), bc(q_in), bc(k_out), bc(g_all[:, None])


def _block_tri(c, reps, strict=False):
    idx = np.arange(c * reps)
    same = (idx[:, None] // c) == (idx[None, :] // c)
    low = idx[:, None] > idx[None, :] if strict else idx[:, None] >= idx[None, :]
    return jnp.asarray(same & low, BF16), jnp.asarray(same, BF16)


def _expert_schedule(counts, n_rows):
    n_tiles = n_rows // M_TILE
    n_items = n_tiles + N_EXPERTS - 1
    ends = jnp.cumsum(counts)
    starts = ends - counts
    first_tile = starts // M_TILE
    tiles_e = jnp.where(counts > 0, (ends - 1) // M_TILE - first_tile + 1, 0)
    item_end = jnp.cumsum(tiles_e)
    item_start = item_end - tiles_e
    total = item_end[-1]
    i = jnp.arange(n_items, dtype=jnp.int32)
    live = i < total
    ic = jnp.minimum(i, total - 1)
    e = jnp.sum((item_end[None, :] <= ic[:, None]).astype(jnp.int32), axis=1)
    e = jnp.minimum(e, N_EXPERTS - 1)
    hot = e[:, None] == jnp.arange(N_EXPERTS, dtype=jnp.int32)[None, :]
    at_e = lambda table: jnp.sum(jnp.where(hot, table[None, :], 0), axis=1)
    tile = (at_e(first_tile) + ic - at_e(item_start)).astype(jnp.int32)
    lo = jnp.where(live, jnp.maximum(at_e(starts), tile * M_TILE), 0).astype(jnp.int32)
    hi = jnp.where(live, jnp.minimum(at_e(ends), (tile + 1) * M_TILE), 0).astype(jnp.int32)
    shifted = lambda a: jnp.concatenate([jnp.full((1,), -1, jnp.int32), a[:-1]])
    first = (live & (tile != shifted(tile))).astype(jnp.int32)
    fresh = (live & (e != shifted(e))).astype(jnp.int32)
    half = M_TILE // 2
    mode = jnp.where(hi <= lo, 3, jnp.where(hi - tile * M_TILE <= half, 1,
                                            jnp.where(lo - tile * M_TILE >= half, 2, 0))).astype(jnp.int32)
    return tile, e, lo, hi, first, fresh, mode


def kernel(x_prompt, x_sample, state_hgrn, state_ret, cache_mem_k, cache_mem_v, mem_prompt, norm_mix, w_in, hgrn_lb_logits, hgrn_out_norm, ret_out_norm, w_out, norm_x, norm_mem, w_xq, w_mk, w_mv, w_xo, norm_ffn, w_router, b_router, w_gate_up, b_gate_up, w_down, b_down, norm_final):
    bp, tp, _ = x_prompt.shape
    bs, ts, _ = x_sample.shape
    n_p, n_s = bp * tp, bs * ts
    past_len = 16384

    row = lambda a: a.reshape(1, -1).astype(F32)
    win = w_in[0].astype(BF16)
    wout = w_out[0].astype(BF16)
    wxq = w_xq[0].astype(BF16)
    wxo = w_xo[0].astype(BF16)
    wmk = w_mk[0].astype(BF16)
    wmv = w_mv[0].astype(BF16)
    wr = jnp.pad(w_router[0], ((0, 0), (0, 128 - N_EXPERTS))).astype(BF16)
    br = jnp.pad(b_router[0], (0, 128 - N_EXPERTS)).reshape(1, 128).astype(F32)
    bg = b_gate_up[0][:, 0::2].reshape(N_EXPERTS, 1, D)
    bl = b_gate_up[0][:, 1::2].reshape(N_EXPERTS, 1, D)
    bd = b_down[0].reshape(N_EXPERTS, 1, D)
    pw = 256
    perm_np = np.zeros((pw, pw), np.float32)
    perm_np[np.arange(pw), (np.arange(pw) % 2) * (pw // 2) + np.arange(pw) // 2] = 1.0
    perm = jnp.asarray(perm_np, BF16)
    nm, nx, nmem, nffn, nfin = row(norm_mix[0]), row(norm_x[0]), row(norm_mem[0]), row(norm_ffn[0]), row(norm_final)
    gh, gr = row(hgrn_out_norm[0]), row(ret_out_norm[0])
    lbl = hgrn_lb_logits.astype(F32)

    cos_p, sin_p = _rope_tables(np.arange(tp))
    dmat_p, qin_p, kout_p, gall_p = _retention_tables(RET_CHUNK, 1)
    tri_p, _ = _block_tri(HGRN_CHUNK, 1)
    mixed_p, sh_p, sr_p = _mixer_prompt(x_prompt, nm, win, lbl, gh, gr, cos_p, sin_p,
                                        dmat_p, qin_p, kout_p, gall_p, tri_p)
    cos_s, sin_s = _rope_tables(np.tile(past_len + np.arange(ts), S_BATCH))
    dmat_s, qin_s, kout_s, gall_s = _retention_tables(ts, S_BATCH)
    tri_s, ones_s = _block_tri(ts, S_BATCH)
    mixed_s, sh_s, sr_s = _mixer_sample(x_sample, nm, win, lbl, gh, gr, cos_s, sin_s,
                                        dmat_s, qin_s, kout_s, gall_s, tri_s, ones_s,
                                        state_hgrn[0], state_ret[0])

    mk_p, mv_p = _memkv(mem_prompt, nmem, wmk, wmv)
    h2_p, hn_p, lg_p = _xattn_prompt(x_prompt, mixed_p, mk_p, mv_p, wout, nx, wxq, wxo, nffn, wr, br)
    h1_s, q_s = _pre_attn(x_sample.reshape(n_s, D), mixed_s, wout, nx, wxq, F32)
    o_s = _attn_sample(q_s, cache_mem_k[0], cache_mem_v[0], ts)
    h2_s, hn_s, lg_s = _post_attn(h1_s, o_s, wxo, nffn, wr, br)

    stril, _ = _block_tri(R_TILE, 1, strict=True)
    idx, prep, rank, cst, sub, counts = _router(jnp.concatenate([lg_p, lg_s], axis=0), stril)
    counts = counts[0, :N_EXPERTS].astype(jnp.int32)
    offs = jnp.pad(jnp.cumsum(counts) - counts, (0, 128 - N_EXPERTS)).reshape(1, 128).astype(F32)
    n_rows = (n_p + n_s) * TOP_K
    dest, pos, src_tab, npc_tab = _dest(idx, rank, offs, cst, sub, n_rows)
    dest = dest[:, :TOP_K].reshape(-1)
    pos = pos[:, :TOP_K].reshape(-1)
    src_tab, npc_tab = src_tab.reshape(-1), npc_tab.reshape(-1)
    cut = n_p * TOP_K

    xs = _dispatch(dest[:cut], dest[cut:], hn_p, hn_s)
    ys = _experts(_expert_schedule(counts, n_rows), xs, w_gate_up[0], bg, bl, w_down[0], bd, perm)
    y_p = _combine(pos[:cut], src_tab, npc_tab, 0, n_p, ys, h2_p, prep, nfin)
    y_s = _combine(pos[cut:], src_tab, npc_tab, n_p // COMBINE_TILE, n_s, ys, h2_s, prep, nfin)

    return (y_p.reshape(bp, tp, D), y_s.reshape(bs, ts, D),
            sh_p[None], sr_p[None], mk_p[None], mv_p[None], sh_s[None], sr_s[None])
```

```python
import functools

import numpy as np
import jax
import jax.numpy as jnp
from jax import lax
from jax.experimental import pallas as pl
from jax.experimental.pallas import tpu as pltpu

F32 = jnp.float32
BF16 = jnp.bfloat16

D = 1024
HEADS = 4
DH = 128
GW = HEADS * DH
IN_COLS = 8 * GW
ROPE_BASE = 10000.0
N_MEM = 256
X_HEADS = 4
X_DH = D // X_HEADS
N_EXPERTS = 32
TOP_K = 4
LIMIT = 7.0
ALPHA = 1.702
EPS = 1e-6
EXP_CLAMP = 80.0

HGRN_CHUNK = 64
RET_CHUNK = 128
T_TILE = 512
S_BATCH = 8
ROW_TILE = 512
Q_TILE = 1024
A_BATCH = 8
R_TILE = 1024
M_TILE = 512
DISPATCH_TILE = 512
COMBINE_TILE = 128
RUN_PIECE = 16
RUN_ROWS = COMBINE_TILE + RUN_PIECE
VMEM_LIMIT = 56 * 1024 * 1024


def _cparams(*sem):
    return pltpu.CompilerParams(dimension_semantics=sem, vmem_limit_bytes=VMEM_LIMIT)


def _dot(a, b):
    return jnp.dot(a.astype(BF16), b.astype(BF16), preferred_element_type=F32)


def _dot_nt(a, b):
    return lax.dot_general(a.astype(BF16), b.astype(BF16), (((1,), (1,)), ((), ())),
                           preferred_element_type=F32)


def _dot_tn(a, b):
    return lax.dot_general(a, b, (((0,), (0,)), ((), ())), preferred_element_type=F32)


def _rms(x, g):
    return x * lax.rsqrt(jnp.mean(x * x, axis=-1, keepdims=True) + EPS) * g


ROW_CHUNKS = D // 128


def _load_row_tiles(ref, n, first=0):
    base = first * ROW_CHUNKS
    return jnp.concatenate([ref[pl.ds(base + c, n, stride=ROW_CHUNKS), :] for c in range(ROW_CHUNKS)], axis=1)


def _store_row_tiles(ref, x, first=0):
    n = x.shape[0]
    base = first * ROW_CHUNKS
    for c in range(ROW_CHUNKS):
        ref[pl.ds(base + c, n, stride=ROW_CHUNKS), :] = x[:, c * 128:(c + 1) * 128]


def _cumsum_rows(tri, v):
    hi = v.astype(BF16)
    lo = (v - hi.astype(F32)).astype(BF16)
    return (jnp.dot(tri, hi, preferred_element_type=F32)
            + jnp.dot(tri, lo, preferred_element_type=F32))


def _lower_bound(lbl_ref):
    l = lbl_ref[...]
    m = jnp.max(l, axis=0, keepdims=True)
    e = jnp.exp(l - m)
    return e[0:1, :] / jnp.sum(e, axis=0, keepdims=True)


def _row_to_col(row):
    n = row.shape[1]
    eye = lax.broadcasted_iota(jnp.int32, (n, n), 0) == lax.broadcasted_iota(jnp.int32, (n, n), 1)
    return jnp.sum(jnp.where(eye, jnp.broadcast_to(row, (n, n)), 0.0), axis=1, keepdims=True)


def _rotary(x, cs, sn):
    return x * cs + pltpu.roll(x, DH // 2, 1) * sn


def _hgrn_out(o, gh, gate):
    on = o * lax.rsqrt(jnp.mean(o * o, axis=-1, keepdims=True) + EPS) * gh
    return on * jax.nn.sigmoid(gate)


def _ret_out(o, gr, gate):
    mu = jnp.mean(o, axis=-1, keepdims=True)
    xc = o - mu
    var = jnp.mean(xc * xc, axis=-1, keepdims=True)
    return xc * lax.rsqrt(var + EPS) * gr * (gate * jax.nn.sigmoid(gate))


def _mixer_prompt_kernel(x_ref, nm_ref, win_ref, lbl_ref, gh_ref, gr_ref, cos_ref, sin_ref,
                         dmat_ref, qin_ref, kout_ref, gall_ref, tri_ref,
                         mixed_ref, sh_ref, sr_ref, proj_ref, f_ref, lc_ref):
    @pl.when(pl.program_id(1) == 0)
    def _():
        sh_ref[...] = jnp.zeros_like(sh_ref)
        sr_ref[...] = jnp.zeros_like(sr_ref)

    xn = _rms(x_ref[0], nm_ref[...]).astype(BF16)
    proj_ref[...] = jnp.dot(xn, win_ref[...], preferred_element_type=F32)
    lb = _lower_bound(lbl_ref)
    gh = gh_ref[...]
    gr = gr_ref[...]
    c = HGRN_CHUNK
    causal = (lax.broadcasted_iota(jnp.int32, (c, c), 0) >= lax.broadcasted_iota(jnp.int32, (c, c), 1))

    def gate_step(ci, worst):
        rows = pl.ds(pl.multiple_of(ci * c, c), c)
        f = lb + (1.0 - lb) * jax.nn.sigmoid(proj_ref[rows, GW:2 * GW])
        lc = _cumsum_rows(tri_ref[...], jnp.log(f))
        f_ref[rows, :] = f
        lc_ref[rows, :] = lc
        return jnp.maximum(worst, jnp.max(jnp.abs(lc - lc[c // 2 - 1:c // 2, :])))

    worst = lax.fori_loop(0, T_TILE // c, gate_step, jnp.float32(0.0), unroll=True)

    def scores_factored(hq, kk, lc, h, r0):
        sl = slice(h * DH, (h + 1) * DH)
        mid = lc[c // 2 - 1:c // 2, sl]
        qd = hq[:, sl] * jnp.exp(lc[:, sl] - mid)
        kd = kk[:, sl] * jnp.exp(mid - lc[:, sl])
        return jnp.where(causal, _dot_nt(qd, kd), 0.0)

    def scores_termwise(hq, kk, lc, h, r0):
        sl = slice(h * DH, (h + 1) * DH)
        col_id = lax.broadcasted_iota(jnp.int32, (c, c), 1)

        sub_id = lax.broadcasted_iota(jnp.int32, (8, DH), 0)

        def row_of(ref, s):
            grp = ref[pl.ds(pl.multiple_of(r0 + lax.shift_left(lax.shift_right_logical(s, 3), 3), 8), 8), sl]
            return jnp.sum(jnp.where(sub_id == (s & 7), grp, 0.0), axis=0, keepdims=True)

        def body(s, a):
            lcs = row_of(lc_ref, s)
            ks_row = 1.0 - row_of(f_ref, s)
            w = jnp.exp(jnp.minimum(lc[:, sl] - lcs, 0.0))
            col = jnp.sum(hq[:, sl] * ks_row * w, axis=1, keepdims=True)
            return a + jnp.where(col_id == s, col, 0.0)

        return jnp.where(causal, lax.fori_loop(0, c, body, jnp.zeros((c, c), F32)), 0.0)

    def hgrn_step(scores, ci, carry):
        r0 = pl.multiple_of(ci * c, c)
        rows = pl.ds(r0, c)
        hq = proj_ref[rows, 0:GW] * (DH ** -0.5)
        kk = 1.0 - f_ref[rows, :]
        lc = lc_ref[rows, :]
        last = lc[c - 1:c, :]
        qs = hq * jnp.exp(lc)
        ks = kk * jnp.exp(last - lc)
        dl = jnp.exp(last)
        for h in range(HEADS):
            sl = slice(h * DH, (h + 1) * DH)
            v = proj_ref[rows, 2 * GW + h * DH:2 * GW + (h + 1) * DH].astype(BF16)
            a = scores(hq, kk, lc, h, r0)
            s_old = sh_ref[0, h]
            o = _dot(a, v) + _dot(qs[:, sl], s_old)
            sh_ref[0, h] = _row_to_col(dl[:, sl]) * s_old + _dot_tn(ks[:, sl].astype(BF16), v)
            gate = proj_ref[rows, 3 * GW + h * DH:3 * GW + (h + 1) * DH]
            mixed_ref[0, rows, sl] = _hgrn_out(o, gh, gate).astype(BF16)
        return carry

    @pl.when(worst <= EXP_CLAMP)
    def _():
        lax.fori_loop(0, T_TILE // c, functools.partial(hgrn_step, scores_factored), 0, unroll=True)

    @pl.when(jnp.logical_not(worst <= EXP_CLAMP))
    def _():
        lax.fori_loop(0, T_TILE // c, functools.partial(hgrn_step, scores_termwise), 0)

    cr = RET_CHUNK

    def ret_step(ci, carry):
        rows = pl.ds(pl.multiple_of(ci * cr, cr), cr)
        cs = cos_ref[rows, :]
        sn = sin_ref[rows, :]
        for h in range(HEADS):
            base = 4 * GW + h * DH
            q = _rotary(proj_ref[rows, base:base + DH], cs, sn)
            k = _rotary(proj_ref[rows, base + GW:base + GW + DH], cs, sn) * (DH ** -0.5)
            v = proj_ref[rows, base + 2 * GW:base + 2 * GW + DH].astype(BF16)
            sc = _dot_nt(q, k) * dmat_ref[h]
            s_old = sr_ref[0, h]
            o = _dot(sc, v) + _dot(q * qin_ref[h], s_old)
            sr_ref[0, h] = gall_ref[h] * s_old + _dot_tn((k * kout_ref[h]).astype(BF16), v)
            gate = proj_ref[rows, base + 3 * GW:base + 3 * GW + DH]
            mixed_ref[0, rows, GW + h * DH:GW + (h + 1) * DH] = _ret_out(o, gr, gate).astype(BF16)
        return carry

    lax.fori_loop(0, T_TILE // cr, ret_step, 0, unroll=True)


def _mixer_prompt(x, nm, win, lbl, gh, gr, cos, sin, dmat, qin, kout, gall, tri):
    b, t, _ = x.shape
    full = lambda shape: pl.BlockSpec(shape, lambda i, j: (0,) * len(shape))
    state = pl.BlockSpec((1, HEADS, DH, DH), lambda i, j: (i, 0, 0, 0))
    return pl.pallas_call(
        _mixer_prompt_kernel,
        grid=(b, t // T_TILE),
        in_specs=[
            pl.BlockSpec((1, T_TILE, D), lambda i, j: (i, j, 0)),
            full((1, D)), full((D, IN_COLS)), full(lbl.shape), full((1, DH)), full((1, DH)),
            pl.BlockSpec((T_TILE, DH), lambda i, j: (j, 0)),
            pl.BlockSpec((T_TILE, DH), lambda i, j: (j, 0)),
            full(dmat.shape), full(qin.shape), full(kout.shape), full(gall.shape), full(tri.shape),
        ],
        out_specs=[pl.BlockSpec((1, T_TILE, D), lambda i, j: (i, j, 0)), state, state],
        out_shape=[jax.ShapeDtypeStruct((b, t, D), BF16),
                   jax.ShapeDtypeStruct((b, HEADS, DH, DH), F32),
                   jax.ShapeDtypeStruct((b, HEADS, DH, DH), F32)],
        scratch_shapes=[pltpu.VMEM((T_TILE, IN_COLS), F32), pltpu.VMEM((T_TILE, GW), F32),
                        pltpu.VMEM((T_TILE, GW), F32)],
        compiler_params=_cparams("parallel", "arbitrary"),
        name="mixer_prompt",
    )(x, nm, win, lbl, gh, gr, cos, sin, dmat, qin, kout, gall, tri)


def _mixer_sample_kernel(x_ref, nm_ref, win_ref, lbl_ref, gh_ref, gr_ref, cos_ref, sin_ref,
                         dmat_ref, qin_ref, kout_ref, gall_ref, tri_ref, ones_ref,
                         shin_ref, srin_ref, mixed_ref, sh_ref, sr_ref, proj_ref):
    nb, ts, _ = x_ref.shape
    rows_n = nb * ts
    xn = _rms(x_ref[...].reshape(rows_n, D), nm_ref[...]).astype(BF16)
    proj_ref[...] = jnp.dot(xn, win_ref[...], preferred_element_type=F32)
    lb = _lower_bound(lbl_ref)
    gh = gh_ref[...]
    gr = gr_ref[...]
    hq = proj_ref[:, 0:GW] * (DH ** -0.5)
    f = lb + (1.0 - lb) * jax.nn.sigmoid(proj_ref[:, GW:2 * GW])
    kk = 1.0 - f
    lf = jnp.log(f)
    lc = _cumsum_rows(tri_ref[...], lf)
    last = _cumsum_rows(ones_ref[...], lf)
    qd = hq * jnp.exp(lc)
    ks = kk * jnp.exp(last - lc)
    dl = jnp.exp(last)
    t_pos = lax.broadcasted_iota(jnp.int32, (nb, ts, 1), 1)

    def intra_termwise(h):
        sl = slice(h * DH, (h + 1) * DH)
        q3 = hq[:, sl].reshape(nb, ts, DH)
        k3 = kk[:, sl].reshape(nb, ts, DH)
        lc3 = lc[:, sl].reshape(nb, ts, DH)
        v3 = proj_ref[:, 2 * GW + h * DH:2 * GW + (h + 1) * DH].reshape(nb, ts, DH)
        o3 = jnp.zeros((nb, ts, DH), F32)
        for s in range(ts):
            w = jnp.exp(jnp.minimum(lc3 - lc3[:, s:s + 1, :], 0.0))
            score = jnp.sum(q3 * k3[:, s:s + 1, :] * w, axis=-1, keepdims=True)
            o3 = o3 + jnp.where(t_pos >= s, score, 0.0) * v3[:, s:s + 1, :]
        return o3

    for h in range(HEADS):
        sl = slice(h * DH, (h + 1) * DH)
        v = proj_ref[:, 2 * GW + h * DH:2 * GW + (h + 1) * DH]
        o_intra = intra_termwise(h)
        gate = proj_ref[:, 3 * GW + h * DH:3 * GW + (h + 1) * DH]
        for b in range(nb):
            r = slice(b * ts, (b + 1) * ts)
            s_old = shin_ref[b, h]
            o = o_intra[b] + _dot(qd[r, sl], s_old)
            sh_ref[b, h] = (_row_to_col(dl[b * ts:b * ts + 1, sl]) * s_old
                            + _dot_tn(ks[r, sl], v[r]))
            mixed_ref[r, sl] = _hgrn_out(o, gh, gate[r])

    cs = cos_ref[...]
    sn = sin_ref[...]
    for h in range(HEADS):
        base = 4 * GW + h * DH
        q = _rotary(proj_ref[:, base:base + DH], cs, sn)
        k = _rotary(proj_ref[:, base + GW:base + GW + DH], cs, sn) * (DH ** -0.5)
        v = proj_ref[:, base + 2 * GW:base + 2 * GW + DH]
        sc = _dot_nt(q, k) * dmat_ref[h]
        o_intra = _dot(sc, v)
        qi = q * qin_ref[h]
        ko = k * kout_ref[h]
        gate = proj_ref[:, base + 3 * GW:base + 3 * GW + DH]
        for b in range(nb):
            r = slice(b * ts, (b + 1) * ts)
            s_old = srin_ref[b, h]
            o = o_intra[r] + _dot(qi[r], s_old)
            sr_ref[b, h] = gall_ref[h] * s_old + _dot_tn(ko[r], v[r])
            mixed_ref[r, GW + h * DH:GW + (h + 1) * DH] = _ret_out(o, gr, gate[r])


def _mixer_sample(x, nm, win, lbl, gh, gr, cos, sin, dmat, qin, kout, gall, tri, ones, sh, sr):
    b, t, _ = x.shape
    nb = S_BATCH
    full = lambda shape: pl.BlockSpec(shape, lambda i: (0,) * len(shape))
    state = pl.BlockSpec((nb, HEADS, DH, DH), lambda i: (i, 0, 0, 0))
    return pl.pallas_call(
        _mixer_sample_kernel,
        grid=(b // nb,),
        in_specs=[
            pl.BlockSpec((nb, t, D), lambda i: (i, 0, 0)),
            full((1, D)), full((D, IN_COLS)), full(lbl.shape), full((1, DH)), full((1, DH)),
            full(cos.shape), full(sin.shape),
            full(dmat.shape), full(qin.shape), full(kout.shape), full(gall.shape),
            full(tri.shape), full(ones.shape), state, state,
        ],
        out_specs=[pl.BlockSpec((nb * t, D), lambda i: (i, 0)), state, state],
        out_shape=[jax.ShapeDtypeStruct((b * t, D), F32),
                   jax.ShapeDtypeStruct(sh.shape, F32),
                   jax.ShapeDtypeStruct(sr.shape, F32)],
        scratch_shapes=[pltpu.VMEM((nb * t, IN_COLS), F32)],
        compiler_params=_cparams("parallel"),
        name="mixer_sample",
    )(x, nm, win, lbl, gh, gr, cos, sin, dmat, qin, kout, gall, tri, ones, sh, sr)


def _memkv_kernel(mem_ref, g_ref, wk_ref, wv_ref, k_ref, v_ref):
    mn = _rms(mem_ref[0], g_ref[...]).astype(BF16)
    k = jnp.dot(mn, wk_ref[...], preferred_element_type=F32)
    v = jnp.dot(mn, wv_ref[...], preferred_element_type=F32)
    for h in range(X_HEADS):
        k_ref[0, h] = k[:, h * X_DH:(h + 1) * X_DH]
        v_ref[0, h] = v[:, h * X_DH:(h + 1) * X_DH]


def _memkv(mem, g, wk, wv):
    b = mem.shape[0]
    full = lambda shape: pl.BlockSpec(shape, lambda i: (0,) * len(shape))
    kv = pl.BlockSpec((1, X_HEADS, N_MEM, X_DH), lambda i: (i, 0, 0, 0))
    shape = jax.ShapeDtypeStruct((b, X_HEADS, N_MEM, X_DH), F32)
    return pl.pallas_call(
        _memkv_kernel,
        grid=(b,),
        in_specs=[pl.BlockSpec((1, N_MEM, D), lambda i: (i, 0, 0)), full((1, D)), full((D, D)), full((D, D))],
        out_specs=[kv, kv],
        out_shape=[shape, shape],
        compiler_params=_cparams("parallel"),
        name="mem_kv",
    )(mem, g, wk, wv)


def _pre_attn_kernel(x_ref, mixed_ref, wout_ref, nx_ref, wxq_ref, h1_ref, q_ref):
    h1 = x_ref[...] + _dot(mixed_ref[...], wout_ref[...])
    h1_ref[...] = h1
    hn = _rms(h1, nx_ref[...])
    q_ref[...] = _dot(hn, wxq_ref[...]).astype(q_ref.dtype)


def _pre_attn(x, mixed, wout, nx, wxq, q_dtype):
    n = x.shape[0]
    row = pl.BlockSpec((ROW_TILE, D), lambda i: (i, 0))
    full = lambda shape: pl.BlockSpec(shape, lambda i: (0,) * len(shape))
    return pl.pallas_call(
        _pre_attn_kernel,
        grid=(n // ROW_TILE,),
        in_specs=[row, row, full((D, D)), full((1, D)), full((D, D))],
        out_specs=[row, row],
        out_shape=[jax.ShapeDtypeStruct((n, D), F32), jax.ShapeDtypeStruct((n, D), q_dtype)],
        compiler_params=_cparams("parallel"),
        name="pre_attn",
    )(x, mixed, wout, nx, wxq)


def _post_attn_kernel(h1_ref, o_ref, wxo_ref, nf_ref, wr_ref, br_ref, h2_ref, hn_ref, lg_ref):
    h2 = h1_ref[...] + _dot(o_ref[...], wxo_ref[...])
    h2_ref[...] = h2
    hn = _rms(h2, nf_ref[...])
    _store_row_tiles(hn_ref, hn)
    lg_ref[...] = _dot(hn, wr_ref[...]) + br_ref[...]


def _post_attn(h1, o, wxo, nf, wr, br):
    n = h1.shape[0]
    row = pl.BlockSpec((ROW_TILE, D), lambda i: (i, 0))
    full = lambda shape: pl.BlockSpec(shape, lambda i: (0,) * len(shape))
    return pl.pallas_call(
        _post_attn_kernel,
        grid=(n // ROW_TILE,),
        in_specs=[row, row, full((D, D)), full((1, D)), full((D, 128)), full((1, 128))],
        out_specs=[row, pl.BlockSpec((ROW_TILE * ROW_CHUNKS, 128), lambda i: (i, 0)),
                   pl.BlockSpec((ROW_TILE, 128), lambda i: (i, 0))],
        out_shape=[jax.ShapeDtypeStruct((n, D), F32), jax.ShapeDtypeStruct((n * ROW_CHUNKS, 128), F32),
                   jax.ShapeDtypeStruct((n, 128), F32)],
        compiler_params=_cparams("parallel"),
        name="post_attn",
    )(h1, o, wxo, nf, wr, br)


def _softmax_rows(s):
    m = jnp.max(s, axis=-1, keepdims=True)
    e = jnp.exp(s - m)
    return e / jnp.sum(e, axis=-1, keepdims=True)


def _xattn_prompt_kernel(x_ref, mixed_ref, k_ref, v_ref, wout_ref, nx_ref, wxq_ref, wxo_ref,
                         nf_ref, wr_ref, br_ref, h2_ref, hn_ref, lg_ref, o_scr):
    h1 = x_ref[0] + _dot(mixed_ref[0], wout_ref[...])
    q = _dot(_rms(h1, nx_ref[...]), wxq_ref[...]).astype(BF16)
    for h in range(X_HEADS):
        sl = slice(h * X_DH, (h + 1) * X_DH)
        s = _dot_nt(q[:, sl], k_ref[0, h]) * (X_DH ** -0.5)
        o_scr[:, sl] = _dot(_softmax_rows(s), v_ref[0, h]).astype(BF16)
    h2 = h1 + _dot(o_scr[...], wxo_ref[...])
    h2_ref[...] = h2
    hn = _rms(h2, nf_ref[...])
    _store_row_tiles(hn_ref, hn)
    lg_ref[...] = _dot(hn, wr_ref[...]) + br_ref[...]


def _xattn_prompt(x, mixed, mk, mv, wout, nx, wxq, wxo, nf, wr, br):
    b, seq, _ = x.shape
    per_seq = seq // Q_TILE
    full = lambda shape: pl.BlockSpec(shape, lambda i, j: (0,) * len(shape))
    tok = pl.BlockSpec((1, Q_TILE, D), lambda i, j: (i, j, 0))
    kv = pl.BlockSpec((1, X_HEADS, N_MEM, X_DH), lambda i, j: (i, 0, 0, 0))
    row = pl.BlockSpec((Q_TILE, D), lambda i, j: (i * per_seq + j, 0))
    n = b * seq
    return pl.pallas_call(
        _xattn_prompt_kernel,
        grid=(b, per_seq),
        in_specs=[tok, tok, kv, kv, full((D, D)), full((1, D)), full((D, D)), full((D, D)),
                  full((1, D)), full((D, 128)), full((1, 128))],
        out_specs=[row, pl.BlockSpec((Q_TILE * ROW_CHUNKS, 128), lambda i, j: (i * per_seq + j, 0)),
                   pl.BlockSpec((Q_TILE, 128), lambda i, j: (i * per_seq + j, 0))],
        out_shape=[jax.ShapeDtypeStruct((n, D), F32), jax.ShapeDtypeStruct((n * ROW_CHUNKS, 128), F32),
                   jax.ShapeDtypeStruct((n, 128), F32)],
        scratch_shapes=[pltpu.VMEM((Q_TILE, D), BF16)],
        compiler_params=_cparams("parallel", "arbitrary"),
        name="xattn_prompt",
    )(x, mixed, mk, mv, wout, nx, wxq, wxo, nf, wr, br)


def _attn_sample_kernel(q_ref, k_ref, v_ref, o_ref):
    nb = k_ref.shape[0]
    ts = q_ref.shape[0] // nb
    units = [(b, h) for b in range(nb) for h in range(X_HEADS)]
    s = jnp.concatenate(
        [_dot_nt(q_ref[b * ts:(b + 1) * ts, h * X_DH:(h + 1) * X_DH], k_ref[b, h]) for b, h in units], axis=0)
    p = _softmax_rows(s * (X_DH ** -0.5))
    for i, (b, h) in enumerate(units):
        o_ref[b * ts:(b + 1) * ts, h * X_DH:(h + 1) * X_DH] = _dot(p[i * ts:(i + 1) * ts], v_ref[b, h])


def _attn_sample(q, ck, cv, ts):
    n = q.shape[0]
    nb = A_BATCH
    kv = pl.BlockSpec((nb, X_HEADS, N_MEM, X_DH), lambda i: (i, 0, 0, 0))
    qs = pl.BlockSpec((nb * ts, D), lambda i: (i, 0))
    return pl.pallas_call(
        _attn_sample_kernel,
        grid=(n // (nb * ts),),
        in_specs=[qs, kv, kv],
        out_specs=qs,
        out_shape=jax.ShapeDtypeStruct((n, D), F32),
        compiler_params=_cparams("parallel"),
        name="attn_sample",
    )(q, ck, cv)


def _router_kernel(lg_ref, stril_ref, idx_ref, prep_ref, rank_ref, cst_ref, sub_ref, cnt_ref, carry_ref):
    @pl.when(pl.program_id(0) == 0)
    def _():
        carry_ref[...] = jnp.zeros_like(carry_ref)

    n = lg_ref.shape[0]
    lane = lax.broadcasted_iota(jnp.int32, (n, 128), 1)
    lane_f = lane.astype(F32)
    l = jnp.where(lane < N_EXPERTS, lg_ref[...], -jnp.inf)
    tops, idxs, hots = [], [], []
    for _ in range(TOP_K):
        m = jnp.max(l, axis=1, keepdims=True)
        idx = jnp.min(jnp.where(l == m, lane_f, 128.0), axis=1, keepdims=True)
        hot = lane_f == idx
        l = jnp.where(hot, -jnp.inf, l)
        tops.append(m)
        idxs.append(idx)
        hots.append(hot)
    sel = jnp.where(hots[0] | hots[1] | hots[2] | hots[3], 1.0, 0.0)
    before = jnp.dot(stril_ref[...], sel.astype(BF16), preferred_element_type=F32) + carry_ref[...]
    carry_ref[...] += jnp.sum(sel, axis=0, keepdims=True)
    cnt_ref[...] = carry_ref[...]
    subs = n // COMBINE_TILE
    cst_ref[...] = jnp.concatenate([before[j * COMBINE_TILE:j * COMBINE_TILE + 1, :] for j in range(subs)], axis=0)
    sub_ref[...] = jnp.concatenate(
        [jnp.sum(sel[j * COMBINE_TILE:(j + 1) * COMBINE_TILE, :], axis=0, keepdims=True) for j in range(subs)], axis=0)
    es = [jnp.exp(t - tops[0]) for t in tops]
    den = es[0] + es[1] + es[2] + es[3]
    idx_out = jnp.zeros((n, 128), jnp.int32)
    rank_out = jnp.zeros((n, 128), jnp.int32)
    for k in range(TOP_K):
        rank = jnp.sum(jnp.where(hots[k], before, 0.0), axis=1, keepdims=True).astype(jnp.int32)
        idx_out = jnp.where(lane == k, idxs[k].astype(jnp.int32), idx_out)
        rank_out = jnp.where(lane == k, rank, rank_out)
        prep_ref[pl.ds(k, n, stride=TOP_K), :] = jnp.broadcast_to(es[k] / den, (n, 128))
    idx_ref[...] = idx_out
    rank_ref[...] = rank_out


def _router(logits, stril):
    n = logits.shape[0]
    subs = R_TILE // COMBINE_TILE
    row = pl.BlockSpec((R_TILE, 128), lambda i: (i, 0))
    one = pl.BlockSpec((1, 128), lambda i: (0, 0))
    sub = pl.BlockSpec((subs, 128), lambda i: (i, 0))
    n_sub = n // COMBINE_TILE
    return pl.pallas_call(
        _router_kernel,
        grid=(n // R_TILE,),
        in_specs=[row, pl.BlockSpec((R_TILE, R_TILE), lambda i: (0, 0))],
        out_specs=[row, pl.BlockSpec((R_TILE * TOP_K, 128), lambda i: (i, 0)), row, sub, sub, one],
        out_shape=[jax.ShapeDtypeStruct((n, 128), jnp.int32), jax.ShapeDtypeStruct((n * TOP_K, 128), F32),
                   jax.ShapeDtypeStruct((n, 128), jnp.int32), jax.ShapeDtypeStruct((n_sub, 128), F32),
                   jax.ShapeDtypeStruct((n_sub, 128), F32), jax.ShapeDtypeStruct((1, 128), F32)],
        scratch_shapes=[pltpu.VMEM((1, 128), F32)],
        compiler_params=_cparams("arbitrary"),
        name="router",
    )(logits, stril)


def _dest_kernel(n_rows, idx_ref, rank_ref, offs_ref, cst_ref, sub_ref, dest_ref, pos_ref, src_ref, npc_ref):
    n = idx_ref.shape[0]
    subs = n // COMBINE_TILE
    lane = lax.broadcasted_iota(jnp.int32, (n, 128), 1)
    lane_row = lax.broadcasted_iota(jnp.int32, (subs, 128), 1).astype(F32)
    offs = offs_ref[...]
    before = cst_ref[...]
    start = offs + before
    pieces = jnp.floor((sub_ref[...] + (RUN_PIECE - 1)) * (1.0 / RUN_PIECE))
    shift = jnp.maximum(start + pieces * RUN_PIECE - n_rows, 0.0)
    src_ref[...] = ((start - shift) * ROW_CHUNKS).astype(jnp.int32)
    npc_ref[...] = pieces.astype(jnp.int32)
    slot = lane_row * RUN_ROWS - before + shift
    slot = jnp.concatenate([jnp.broadcast_to(slot[j:j + 1, :], (COMBINE_TILE, 128)) for j in range(subs)], axis=0)
    idx = idx_ref[...]
    d_out = jnp.zeros((n, 128), F32)
    p_out = jnp.zeros((n, 128), F32)
    for k in range(TOP_K):
        hot = lane == idx[:, k:k + 1]
        d_out = jnp.where(lane == k, jnp.sum(jnp.where(hot, offs, 0.0), axis=1, keepdims=True), d_out)
        p_out = jnp.where(lane == k, jnp.sum(jnp.where(hot, slot, 0.0), axis=1, keepdims=True), p_out)
    rank = rank_ref[...]
    dest_ref[...] = (d_out.astype(jnp.int32) + rank) * ROW_CHUNKS
    pos_ref[...] = (p_out.astype(jnp.int32) + rank) * ROW_CHUNKS


def _dest(idx, rank, offs, cst, sub, n_rows):
    n = idx.shape[0]
    subs = R_TILE // COMBINE_TILE
    row = pl.BlockSpec((R_TILE, 128), lambda i: (i, 0))
    tab = pl.BlockSpec((subs, 128), lambda i: (i, 0))
    n_sub = n // COMBINE_TILE
    return pl.pallas_call(
        functools.partial(_dest_kernel, n_rows),
        grid=(n // R_TILE,),
        in_specs=[row, row, pl.BlockSpec((1, 128), lambda i: (0, 0)), tab, tab],
        out_specs=[row, row, tab, tab],
        out_shape=[jax.ShapeDtypeStruct((n, 128), jnp.int32), jax.ShapeDtypeStruct((n, 128), jnp.int32),
                   jax.ShapeDtypeStruct((n_sub, 128), jnp.int32), jax.ShapeDtypeStruct((n_sub, 128), jnp.int32)],
        compiler_params=_cparams("parallel"),
        name="route_dest",
    )(idx, rank, offs, cst, sub)


def _dispatch_kernel(p_tiles, dest_p_ref, dest_s_ref, hn_p_ref, hn_s_ref, xs_ref, sem):
    nt = DISPATCH_TILE
    i = pl.program_id(0)

    def scatter(dest_ref, hn_ref, tile):
        base = tile * (nt * TOP_K)

        def copy(t, k):
            dst = pl.multiple_of(dest_ref[base + t * TOP_K + k], ROW_CHUNKS)
            return pltpu.make_async_copy(hn_ref.at[pl.ds(t * ROW_CHUNKS, ROW_CHUNKS)],
                                         xs_ref.at[pl.ds(dst, ROW_CHUNKS)], sem)

        def start(t, carry):
            for k in range(TOP_K):
                copy(t, k).start(priority=k % 2)
            return carry

        def wait(t, carry):
            for k in range(TOP_K):
                copy(t, k).wait()
            return carry

        for t in range(nt):
            start(t, 0)
        for t in range(nt):
            wait(t, 0)

    @pl.when(i < p_tiles)
    def _():
        scatter(dest_p_ref, hn_p_ref, i)

    @pl.when(i >= p_tiles)
    def _():
        scatter(dest_s_ref, hn_s_ref, i - p_tiles)


def _dispatch(dest_p, dest_s, hn_p, hn_s):
    p_tiles = hn_p.shape[0] // (DISPATCH_TILE * ROW_CHUNKS)
    s_tiles = hn_s.shape[0] // (DISPATCH_TILE * ROW_CHUNKS)
    n_rows = (hn_p.shape[0] + hn_s.shape[0]) * TOP_K
    return pl.pallas_call(
        functools.partial(_dispatch_kernel, p_tiles),
        grid_spec=pltpu.PrefetchScalarGridSpec(
            num_scalar_prefetch=2,
            grid=(p_tiles + s_tiles,),
            in_specs=[pl.BlockSpec((DISPATCH_TILE * ROW_CHUNKS, 128),
                                   lambda i, dp, ds: (jnp.minimum(i, p_tiles - 1), 0)),
                      pl.BlockSpec((DISPATCH_TILE * ROW_CHUNKS, 128),
                                   lambda i, dp, ds: (jnp.maximum(i - p_tiles, 0), 0))],
            out_specs=pl.BlockSpec(memory_space=pl.ANY),
            scratch_shapes=[pltpu.SemaphoreType.DMA],
        ),
        out_shape=jax.ShapeDtypeStruct((n_rows, 128), F32),
        compiler_params=_cparams("arbitrary"),
        name="moe_dispatch",
    )(dest_p, dest_s, hn_p, hn_s)


def _experts_kernel(tile_ref, exp_ref, lo_ref, hi_ref, first_ref, fresh_ref, mode_ref,
                    xs_ref, wgu_ref, bg_ref, bl_ref, wd_ref, bd_ref, perm_ref, ys_ref,
                    wg_s, wl_s, wd_s):
    i = pl.program_id(0)
    lo = lo_ref[i]
    hi = hi_ref[i]
    half = M_TILE // 2

    @pl.when(fresh_ref[i] == 1)
    def _():
        pw = perm_ref.shape[0]
        for c in range(2 * D // pw):
            blk = wgu_ref[0, :, c * pw:(c + 1) * pw].astype(BF16)
            sep = jnp.dot(blk, perm_ref[...], preferred_element_type=F32)
            wg_s[:, c * pw // 2:(c + 1) * pw // 2] = sep[:, :pw // 2].astype(BF16)
            wl_s[:, c * pw // 2:(c + 1) * pw // 2] = sep[:, pw // 2:].astype(BF16)
        wd_s[...] = wd_ref[0].astype(BF16)

    def rows_of_tile(r0, n):
        x = _load_row_tiles(xs_ref, n, r0).astype(BF16)
        glu = jnp.minimum(jnp.dot(x, wg_s[...], preferred_element_type=F32) + bg_ref[0], LIMIT)
        lin = jnp.clip(jnp.dot(x, wl_s[...], preferred_element_type=F32) + bl_ref[0], -LIMIT, LIMIT)
        hmid = glu * jax.nn.sigmoid(ALPHA * glu) * (lin + 1.0)
        y = _dot(hmid, wd_s[...]) + bd_ref[0]
        row = tile_ref[i] * M_TILE + r0 + lax.broadcasted_iota(jnp.int32, (n, 1), 0)
        mine = (row >= lo) & (row < hi)

        @pl.when(first_ref[i] == 1)
        def _():
            _store_row_tiles(ys_ref, jnp.where(mine, y, 0.0), r0)
            if n < M_TILE:
                other = (half - r0) * ROW_CHUNKS
                ys_ref[other:other + half * ROW_CHUNKS, :] = jnp.zeros((half * ROW_CHUNKS, 128), F32)

        @pl.when(first_ref[i] == 0)
        def _():
            _store_row_tiles(ys_ref, jnp.where(mine, y, _load_row_tiles(ys_ref, n, r0)), r0)

    @pl.when(mode_ref[i] == 0)
    def _():
        rows_of_tile(0, M_TILE)

    @pl.when(mode_ref[i] == 1)
    def _():
        rows_of_tile(0, half)

    @pl.when(mode_ref[i] == 2)
    def _():
        rows_of_tile(half, half)


def _experts(meta, xs, wgu, bg, bl, wd, bd, perm):
    tile, expert, lo, hi, first, fresh, mode = meta
    rows = pl.BlockSpec((M_TILE * ROW_CHUNKS, 128), lambda i, t, e, *_: (t[i], 0))
    bspec = pl.BlockSpec((1, 1, D), lambda i, t, e, *_: (e[i], 0, 0))
    return pl.pallas_call(
        _experts_kernel,
        grid_spec=pltpu.PrefetchScalarGridSpec(
            num_scalar_prefetch=7,
            grid=(tile.shape[0],),
            in_specs=[rows,
                      pl.BlockSpec((1, D, 2 * D), lambda i, t, e, *_: (e[i], 0, 0)), bspec, bspec,
                      pl.BlockSpec((1, D, D), lambda i, t, e, *_: (e[i], 0, 0)), bspec,
                      pl.BlockSpec(perm.shape, lambda i, *_: (0, 0))],
            out_specs=rows,
            scratch_shapes=[pltpu.VMEM((D, D), BF16)] * 3,
        ),
        out_shape=jax.ShapeDtypeStruct(xs.shape, F32),
        compiler_params=_cparams("arbitrary"),
        name="moe_experts",
    )(tile, expert, lo, hi, first, fresh, mode, xs, wgu, bg, bl, wd, bd, perm)


def _combine_kernel(tile0, pos_ref, src_ref, npc_ref, ys_ref, h2_ref, prep_ref, nf_ref, y_ref,
                    buf_ref, moe_ref, sem):
    nt = COMBINE_TILE
    i = pl.program_id(0)
    piece_rows = RUN_PIECE * ROW_CHUNKS

    def piece(tile, e, j):
        tab = (tile + tile0) * 128
        half = tile & 1
        src = pl.multiple_of(src_ref[tab + e] + j * piece_rows, ROW_CHUNKS)
        dst = pl.multiple_of(e * (RUN_ROWS * ROW_CHUNKS) + j * piece_rows, ROW_CHUNKS)
        return pltpu.make_async_copy(ys_ref.at[pl.ds(src, piece_rows)],
                                     buf_ref.at[half, pl.ds(dst, piece_rows)], sem.at[half])

    def fetch(tile):
        for e in range(N_EXPERTS):
            def start(j, carry, e=e):
                piece(tile, e, j).start(priority=e % 2)
                return carry
            lax.fori_loop(0, npc_ref[(tile + tile0) * 128 + e], start, 0)

    @pl.when(i == 0)
    def _():
        fetch(i)

    @pl.when(i + 1 < pl.num_programs(0))
    def _():
        fetch(i + 1)

    for e in range(N_EXPERTS):
        def wait(j, carry, e=e):
            piece(i, e, j).wait()
            return carry
        lax.fori_loop(0, npc_ref[(i + tile0) * 128 + e], wait, 0)

    base = i * (nt * TOP_K)
    runs = buf_ref.at[i & 1]
    for t in range(nt):
        acc = None
        for k in range(TOP_K):
            row = pl.multiple_of(pos_ref[base + t * TOP_K + k], ROW_CHUNKS)
            term = runs[pl.ds(row, ROW_CHUNKS), :] * prep_ref[t * TOP_K + k:t * TOP_K + k + 1, :]
            acc = term if acc is None else acc + term
        moe_ref[t * ROW_CHUNKS:(t + 1) * ROW_CHUNKS, :] = acc
    h3 = h2_ref[...] + _load_row_tiles(moe_ref, nt)
    y_ref[...] = _rms(h3, nf_ref[...])


def _combine(pos_flat, src_tab, npc_tab, tile0, n, ys, h2, prep, nf):
    row = pl.BlockSpec((COMBINE_TILE, D), lambda i, *_: (i, 0))
    return pl.pallas_call(
        functools.partial(_combine_kernel, tile0),
        grid_spec=pltpu.PrefetchScalarGridSpec(
            num_scalar_prefetch=3,
            grid=(n // COMBINE_TILE,),
            in_specs=[pl.BlockSpec(memory_space=pl.ANY), row,
                      pl.BlockSpec((COMBINE_TILE * TOP_K, 128), lambda i, *_: (i + tile0, 0)),
                      pl.BlockSpec((1, D), lambda i, *_: (0, 0))],
            out_specs=row,
            scratch_shapes=[pltpu.VMEM((2, N_EXPERTS * RUN_ROWS * ROW_CHUNKS, 128), F32),
                            pltpu.VMEM((COMBINE_TILE * ROW_CHUNKS, 128), F32),
                            pltpu.SemaphoreType.DMA((2,))],
        ),
        out_shape=jax.ShapeDtypeStruct((n, D), F32),
        compiler_params=_cparams("arbitrary"),
        name="moe_combine",
    )(pos_flat, src_tab, npc_tab, ys, h2, prep, nf)


def _rope_tables(pos):
    half = DH // 2
    inv = np.power(ROPE_BASE, -np.arange(half, dtype=np.float64) / half)
    ang = pos.astype(np.float64)[:, None] * inv[None, :]
    cos = np.concatenate([np.cos(ang), np.cos(ang)], axis=1)
    sin = np.concatenate([-np.sin(ang), np.sin(ang)], axis=1)
    return jnp.asarray(cos, F32), jnp.asarray(sin, F32)


def _retention_tables(c, reps):
    log_g = np.log1p(-np.exp2(-5.0 - np.arange(HEADS, dtype=np.float64)))
    idx = np.arange(c, dtype=np.float64)
    rel = idx[:, None] - idx[None, :]
    dmat = np.where(rel >= 0, np.exp(log_g[:, None, None] * np.maximum(rel, 0.0)), 0.0)
    big = np.zeros((HEADS, c * reps, c * reps))
    for r in range(reps):
        big[:, r * c:(r + 1) * c, r * c:(r + 1) * c] = dmat
    q_in = np.tile(np.exp(log_g[:, None] * (idx + 1.0)), (1, reps))
    k_out = np.tile(np.exp(log_g[:, None] * (c - 1.0 - idx)), (1, reps))
    g_all = np.exp(log_g * c)
    bc = lambda a: jnp.asarray(np.broadcast_to(a[..., None], a.shape + (DH,)), F32)
    return jnp.asarray(big, F32), bc(q_in), bc(k_out), bc(g_all[:, None])


def _block_tri(c, reps, strict=False):
    idx = np.arange(c * reps)
    same = (idx[:, None] // c) == (idx[None, :] // c)
    low = idx[:, None] > idx[None, :] if strict else idx[:, None] >= idx[None, :]
    return jnp.asarray(same & low, BF16), jnp.asarray(same, BF16)


def _expert_schedule(counts, n_rows):
    n_tiles = n_rows // M_TILE
    n_items = n_tiles + N_EXPERTS - 1
    ends = jnp.cumsum(counts)
    starts = ends - counts
    first_tile = starts // M_TILE
    tiles_e = jnp.where(counts > 0, (ends - 1) // M_TILE - first_tile + 1, 0)
    item_end = jnp.cumsum(tiles_e)
    item_start = item_end - tiles_e
    total = item_end[-1]
    i = jnp.arange(n_items, dtype=jnp.int32)
    live = i < total
    ic = jnp.minimum(i, total - 1)
    e = jnp.sum((item_end[None, :] <= ic[:, None]).astype(jnp.int32), axis=1)
    e = jnp.minimum(e, N_EXPERTS - 1)
    hot = e[:, None] == jnp.arange(N_EXPERTS, dtype=jnp.int32)[None, :]
    at_e = lambda table: jnp.sum(jnp.where(hot, table[None, :], 0), axis=1)
    tile = (at_e(first_tile) + ic - at_e(item_start)).astype(jnp.int32)
    lo = jnp.where(live, jnp.maximum(at_e(starts), tile * M_TILE), 0).astype(jnp.int32)
    hi = jnp.where(live, jnp.minimum(at_e(ends), (tile + 1) * M_TILE), 0).astype(jnp.int32)
    shifted = lambda a: jnp.concatenate([jnp.full((1,), -1, jnp.int32), a[:-1]])
    first = (live & (tile != shifted(tile))).astype(jnp.int32)
    fresh = (live & (e != shifted(e))).astype(jnp.int32)
    half = M_TILE // 2
    mode = jnp.where(hi <= lo, 3, jnp.where(hi - tile * M_TILE <= half, 1,
                                            jnp.where(lo - tile * M_TILE >= half, 2, 0))).astype(jnp.int32)
    return tile, e, lo, hi, first, fresh, mode


def kernel(x_prompt, x_sample, state_hgrn, state_ret, cache_mem_k, cache_mem_v, mem_prompt, norm_mix, w_in, hgrn_lb_logits, hgrn_out_norm, ret_out_norm, w_out, norm_x, norm_mem, w_xq, w_mk, w_mv, w_xo, norm_ffn, w_router, b_router, w_gate_up, b_gate_up, w_down, b_down, norm_final):
    bp, tp, _ = x_prompt.shape
    bs, ts, _ = x_sample.shape
    n_p, n_s = bp * tp, bs * ts
    past_len = 16384

    row = lambda a: a.reshape(1, -1).astype(F32)
    win = w_in[0].astype(BF16)
    wout = w_out[0].astype(BF16)
    wxq = w_xq[0].astype(BF16)
    wxo = w_xo[0].astype(BF16)
    wmk = w_mk[0].astype(BF16)
    wmv = w_mv[0].astype(BF16)
    wr = jnp.pad(w_router[0], ((0, 0), (0, 128 - N_EXPERTS))).astype(BF16)
    br = jnp.pad(b_router[0], (0, 128 - N_EXPERTS)).reshape(1, 128).astype(F32)
    bg = b_gate_up[0][:, 0::2].reshape(N_EXPERTS, 1, D)
    bl = b_gate_up[0][:, 1::2].reshape(N_EXPERTS, 1, D)
    bd = b_down[0].reshape(N_EXPERTS, 1, D)
    pw = 256
    perm_np = np.zeros((pw, pw), np.float32)
    perm_np[np.arange(pw), (np.arange(pw) % 2) * (pw // 2) + np.arange(pw) // 2] = 1.0
    perm = jnp.asarray(perm_np, BF16)
    nm, nx, nmem, nffn, nfin = row(norm_mix[0]), row(norm_x[0]), row(norm_mem[0]), row(norm_ffn[0]), row(norm_final)
    gh, gr = row(hgrn_out_norm[0]), row(ret_out_norm[0])
    lbl = hgrn_lb_logits.astype(F32)

    cos_p, sin_p = _rope_tables(np.arange(tp))
    dmat_p, qin_p, kout_p, gall_p = _retention_tables(RET_CHUNK, 1)
    tri_p, _ = _block_tri(HGRN_CHUNK, 1)
    mixed_p, sh_p, sr_p = _mixer_prompt(x_prompt, nm, win, lbl, gh, gr, cos_p, sin_p,
                                        dmat_p, qin_p, kout_p, gall_p, tri_p)
    cos_s, sin_s = _rope_tables(np.tile(past_len + np.arange(ts), S_BATCH))
    dmat_s, qin_s, kout_s, gall_s = _retention_tables(ts, S_BATCH)
    tri_s, ones_s = _block_tri(ts, S_BATCH)
    mixed_s, sh_s, sr_s = _mixer_sample(x_sample, nm, win, lbl, gh, gr, cos_s, sin_s,
                                        dmat_s, qin_s, kout_s, gall_s, tri_s, ones_s,
                                        state_hgrn[0], state_ret[0])

    mk_p, mv_p = _memkv(mem_prompt, nmem, wmk, wmv)
    h2_p, hn_p, lg_p = _xattn_prompt(x_prompt, mixed_p, mk_p, mv_p, wout, nx, wxq, wxo, nffn, wr, br)
    h1_s, q_s = _pre_attn(x_sample.reshape(n_s, D), mixed_s, wout, nx, wxq, F32)
    o_s = _attn_sample(q_s, cache_mem_k[0], cache_mem_v[0], ts)
    h2_s, hn_s, lg_s = _post_attn(h1_s, o_s, wxo, nffn, wr, br)

    stril, _ = _block_tri(R_TILE, 1, strict=True)
    idx, prep, rank, cst, sub, counts = _router(jnp.concatenate([lg_p, lg_s], axis=0), stril)
    counts = counts[0, :N_EXPERTS].astype(jnp.int32)
    offs = jnp.pad(jnp.cumsum(counts) - counts, (0, 128 - N_EXPERTS)).reshape(1, 128).astype(F32)
    n_rows = (n_p + n_s) * TOP_K
    dest, pos, src_tab, npc_tab = _dest(idx, rank, offs, cst, sub, n_rows)
    dest = dest[:, :TOP_K].reshape(-1)
    pos = pos[:, :TOP_K].reshape(-1)
    src_tab, npc_tab = src_tab.reshape(-1), npc_tab.reshape(-1)
    cut = n_p * TOP_K

    xs = _dispatch(dest[:cut], dest[cut:], hn_p, hn_s)
    ys = _experts(_expert_schedule(counts, n_rows), xs, w_gate_up[0], bg, bl, w_down[0], bd, perm)
    y_p = _combine(pos[:cut], src_tab, npc_tab, 0, n_p, ys, h2_p, prep, nfin)
    y_s = _combine(pos[cut:], src_tab, npc_tab, n_p // COMBINE_TILE, n_s, ys, h2_s, prep, nfin)

    return (y_p.reshape(bp, tp, D), y_s.reshape(bs, ts, D),
            sh_p[None], sr_p[None], mk_p[None], mv_p[None], sh_s[None], sr_s[None])
```
